```python
import jax, jax.numpy as jnp
from jax import lax
import numpy as np

D_MODEL = 2048
BATCH = 8
SEQ = 2048
DEPTH = 1

N_META = 16
CONV_K = 4
EPS = 1e-6
SSD_D_INNER = D_MODEL
SSD_HEAD_DIM = 64
SSD_HEADS = SSD_D_INNER // SSD_HEAD_DIM
SSD_GROUPS = 8
SSD_HPG = SSD_HEADS // SSD_GROUPS
SSD_STATE = 128
SSD_CHUNK = 128
SSD_CONV_DIM = SSD_D_INNER + 2 * SSD_GROUPS * SSD_STATE
LRU_WIDTH = D_MODEL
LRU_HEADS = 8
LRU_BLOCK = LRU_WIDTH // LRU_HEADS
LRU_C = 8.0
N_BRANCH = 2
COL_Z = SSD_D_INNER
COL_XBC = COL_Z + SSD_CONV_DIM
COL_DT = COL_XBC + SSD_HEADS
COL_LX = COL_DT + LRU_WIDTH
COL_LY = COL_LX + LRU_WIDTH
IN_COLS = COL_LY + N_BRANCH * D_MODEL
MOE_GROUPS = 8
MOE_EXP_PER_GROUP = 8
MOE_EXPERTS = MOE_GROUPS * MOE_EXP_PER_GROUP
MOE_TOPK = 2
MOE_FF = D_MODEL // 2
MOE_BLOCK = 128

kernel_name = "hybrid_ssd_rglru_hmoe_block"


def _rmsnorm(x, g):
    xf = x.astype(jnp.float32)
    y = xf * lax.rsqrt(jnp.mean(xf * xf, axis=-1, keepdims=True) + EPS)
    return (y * g.astype(jnp.float32)).astype(x.dtype)


def _causal_dwconv(x, w, b):
    k = w.shape[0]
    l = x.shape[1]
    xp = jnp.pad(x, ((0, 0), (k - 1, 0), (0, 0)))
    out = b
    for i in range(k):
        out = out + xp[:, i:i + l] * w[i]
    return out


def _ssd_branch(z, xbc, dt_raw, conv_w, conv_b, dt_bias, a_log, d_skip, norm_w):
    b, l, _ = xbc.shape
    xbc = jax.nn.silu(_causal_dwconv(xbc, conv_w, conv_b))
    xs, bm, cm = jnp.split(xbc, [SSD_D_INNER, SSD_D_INNER + SSD_GROUPS * SSD_STATE], axis=-1)
    xs = xs.reshape(b, l, SSD_HEADS, SSD_HEAD_DIM)
    dt = jax.nn.softplus((dt_raw + dt_bias).astype(jnp.float32))
    a = -jnp.exp(a_log.astype(jnp.float32))
    pad = (-l) % SSD_CHUNK
    lp = l + pad
    nc = lp // SSD_CHUNK
    q = SSD_CHUNK

    def front(t):
        return jnp.pad(t, ((0, 0), (pad, 0)) + ((0, 0),) * (t.ndim - 2))

    xdt = front(xs * dt[..., None]).reshape(b, nc, q, SSD_GROUPS, SSD_HPG, SSD_HEAD_DIM)
    bmc = front(bm).reshape(b, nc, q, SSD_GROUPS, SSD_STATE)
    cmc = front(cm).reshape(b, nc, q, SSD_GROUPS, SSD_STATE)
    a_dt = jnp.moveaxis(front(dt * a).reshape(b, nc, q, SSD_GROUPS, SSD_HPG), 2, -1)
    a_cum = jnp.cumsum(a_dt, axis=-1)
    causal = jnp.tril(jnp.ones((q, q), dtype=bool))
    seg = jnp.where(causal, a_cum[..., :, None] - a_cum[..., None, :], -jnp.inf)
    decay_in = jnp.exp(seg)
    cb = jnp.einsum('bclgn,bcsgn->bcgls', cmc, bmc)
    y_diag = jnp.einsum('bcgls,bcgjls,bcsgjp->bclgjp', cb, decay_in, xdt)
    decay_to_end = jnp.exp(a_cum[..., -1:] - a_cum)
    states = jnp.einsum('bcsgn,bcgjs,bcsgjp->bcgjpn', bmc, decay_to_end, xdt)
    chunk_decay = jnp.exp(a_cum[..., -1])

    def step(carry, inp):
        st, dec = inp
        return carry * dec[..., None, None] + st, carry

    _, prev = lax.scan(step, jnp.zeros_like(states[:, 0]),
                       (jnp.moveaxis(states, 1, 0), jnp.moveaxis(chunk_decay, 1, 0)))
    prev = jnp.moveaxis(prev, 0, 1)
    y_off = jnp.einsum('bclgn,bcgjpn,bcgjl->bclgjp', cmc, prev, jnp.exp(a_cum))
    y = (y_diag + y_off).reshape(b, lp, SSD_HEADS, SSD_HEAD_DIM)[:, pad:]
    y = y + d_skip[:, None] * xs
    y = y.reshape(b, l, SSD_D_INNER) * jax.nn.silu(z)
    yg = y.reshape(b, l, SSD_GROUPS, SSD_D_INNER // SSD_GROUPS).astype(jnp.float32)
    yg = yg * lax.rsqrt(jnp.mean(yg * yg, axis=-1, keepdims=True) + EPS)
    return (yg.reshape(b, l, SSD_D_INNER) * norm_w).astype(z.dtype)


def _rglru_branch(xb, yb, conv_w, conv_b, wa, ba, wx, bx, lam):
    b, l, _ = xb.shape
    xr = _causal_dwconv(xb, conv_w, conv_b)
    xh = xr.reshape(b, l, LRU_HEADS, LRU_BLOCK)
    gate_r = jax.nn.sigmoid((jnp.einsum('blhi,hij->blhj', xh, wa).reshape(b, l, LRU_WIDTH) + ba).astype(jnp.float32))
    gate_i = jax.nn.sigmoid((jnp.einsum('blhi,hij->blhj', xh, wx).reshape(b, l, LRU_WIDTH) + bx).astype(jnp.float32))
    log_a = -LRU_C * gate_r * jax.nn.softplus(-lam.astype(jnp.float32))
    a = jnp.exp(log_a)
    mult = jnp.sqrt(-jnp.expm1(2.0 * log_a))
    u = mult * gate_i * xr.astype(jnp.float32)

    def combine(left, right):
        a_l, u_l = left
        a_r, u_r = right
        return a_l * a_r, a_r * u_l + u_r

    _, h = lax.associative_scan(combine, (a, u), axis=1)
    return (h * jax.nn.gelu(yb.astype(jnp.float32))).astype(xb.dtype)


def _moe(u, w_rg, w_re, w1, w3, w2):
    b, l, d = u.shape
    t = u.reshape(b * l, d)
    n_tok = t.shape[0]
    g_prob = jax.nn.softmax((t @ w_rg).astype(jnp.float32), axis=-1)
    g_p, g_idx = lax.top_k(g_prob, 1)
    e_logits = (t @ w_re).reshape(n_tok, MOE_GROUPS, MOE_EXP_PER_GROUP)
    e_sel = jnp.take_along_axis(e_logits, g_idx[:, :, None], axis=1)[:, 0]
    e_prob = jax.nn.softmax(e_sel.astype(jnp.float32), axis=-1)
    e_p, e_loc = lax.top_k(e_prob, MOE_TOPK)
    gates = g_p * e_p / jnp.sum(e_p, axis=-1, keepdims=True)
    expert_id = g_idx * MOE_EXP_PER_GROUP + e_loc

    n_asg = n_tok * MOE_TOPK
    n_blocks = -(-(n_asg + MOE_EXPERTS * (MOE_BLOCK - 1)) // MOE_BLOCK)
    p_len = n_blocks * MOE_BLOCK
    flat_e = expert_id.reshape(-1)
    flat_tok = jnp.repeat(jnp.arange(n_tok, dtype=jnp.int32), MOE_TOPK)
    flat_gate = gates.reshape(-1)
    order = jnp.argsort(flat_e)
    se = flat_e[order]
    counts = jnp.bincount(flat_e, length=MOE_EXPERTS)
    padded = (counts + MOE_BLOCK - 1) // MOE_BLOCK * MOE_BLOCK
    pad_end = jnp.cumsum(padded)
    pad_start = pad_end - padded
    start = jnp.cumsum(counts) - counts
    dest = pad_start[se] + jnp.arange(n_asg) - start[se]
    buf_tok = jnp.full((p_len,), n_tok, dtype=jnp.int32).at[dest].set(flat_tok[order])
    buf_gate = jnp.zeros((p_len,), u.dtype).at[dest].set(flat_gate[order].astype(u.dtype))
    blk_e = jnp.minimum(jnp.searchsorted(pad_end, jnp.arange(n_blocks) * MOE_BLOCK, side='right'),
                        MOE_EXPERTS - 1)
    t_pad = jnp.concatenate([t, jnp.zeros((1, d), t.dtype)], axis=0)

    def expert_block(args):
        tok, gate, e = args
        xb = t_pad[tok]
        hdn = jax.nn.silu(xb @ w1[e]) * (xb @ w3[e])
        return (hdn @ w2[e]) * gate[:, None]

    y_blocks = lax.map(expert_block, (buf_tok.reshape(n_blocks, MOE_BLOCK),
                                      buf_gate.reshape(n_blocks, MOE_BLOCK), blk_e))
    out = jnp.zeros((n_tok + 1, d), u.dtype).at[buf_tok].add(y_blocks.reshape(p_len, d).astype(u.dtype))
    return out[:n_tok].reshape(b, l, d)


def setup_inputs(seed: int = 0) -> dict:
    key = jax.random.key(seed)
    ks = jax.random.split(key, 32)
    f32 = jnp.float32

    def nrm(k, shape, scale):
        return jax.random.normal(k, shape, f32) * scale

    dt0 = jnp.exp(jax.random.uniform(ks[6], (DEPTH, SSD_HEADS), f32, np.log(1e-3), np.log(1e-1)))
    a_pow = jax.random.uniform(ks[14], (DEPTH, LRU_WIDTH), f32, 0.9, 0.999)
    a_base = a_pow ** (1.0 / LRU_C)
    return {
        "x": nrm(ks[0], (BATCH, SEQ, D_MODEL), 1.0),
        "meta_tokens": nrm(ks[1], (N_META, D_MODEL), 1.0),
        "norm_mix": 1.0 + nrm(ks[2], (DEPTH, D_MODEL), 0.02),
        "w_in": nrm(ks[3], (DEPTH, D_MODEL, IN_COLS), D_MODEL ** -0.5),
        "ssd_conv_w": nrm(ks[4], (DEPTH, CONV_K, SSD_CONV_DIM), CONV_K ** -0.5),
        "ssd_conv_b": nrm(ks[5], (DEPTH, SSD_CONV_DIM), 0.01),
        "ssd_dt_bias": dt0 + jnp.log(-jnp.expm1(-dt0)),
        "ssd_a_log": jnp.log(jax.random.uniform(ks[7], (DEPTH, SSD_HEADS), f32, 1.0, 16.0)),
        "ssd_d": 1.0 + nrm(ks[8], (DEPTH, SSD_HEADS), 0.02),
        "ssd_norm": 1.0 + nrm(ks[9], (DEPTH, SSD_D_INNER), 0.02),
        "w_ssd_out": nrm(ks[10], (DEPTH, SSD_D_INNER, D_MODEL), SSD_D_INNER ** -0.5),
        "lru_conv_w": nrm(ks[11], (DEPTH, CONV_K, LRU_WIDTH), CONV_K ** -0.5),
        "lru_conv_b": nrm(ks[12], (DEPTH, LRU_WIDTH), 0.01),
        "lru_wa": nrm(ks[13], (DEPTH, LRU_HEADS, LRU_BLOCK, LRU_BLOCK), LRU_BLOCK ** -0.5),
        "lru_ba": nrm(ks[15], (DEPTH, LRU_WIDTH), 0.01),
        "lru_wx": nrm(ks[16], (DEPTH, LRU_HEADS, LRU_BLOCK, LRU_BLOCK), LRU_BLOCK ** -0.5),
        "lru_bx": nrm(ks[17], (DEPTH, LRU_WIDTH), 0.01),
        "lru_lambda": jnp.log(a_base) - jnp.log1p(-a_base),
        "w_lru_out": nrm(ks[18], (DEPTH, LRU_WIDTH, D_MODEL), LRU_WIDTH ** -0.5),
        "gate_bias": nrm(ks[19], (DEPTH, N_BRANCH, D_MODEL), 0.01),
        "w_out": nrm(ks[20], (DEPTH, D_MODEL, D_MODEL), D_MODEL ** -0.5),
        "norm_ffn": 1.0 + nrm(ks[21], (DEPTH, D_MODEL), 0.02),
        "w_router_group": nrm(ks[22], (DEPTH, D_MODEL, MOE_GROUPS), D_MODEL ** -0.5),
        "w_router_expert": nrm(ks[23], (DEPTH, D_MODEL, MOE_EXPERTS), D_MODEL ** -0.5),
        "w_exp_gate": nrm(ks[24], (DEPTH, MOE_EXPERTS, D_MODEL, MOE_FF), D_MODEL ** -0.5),
        "w_exp_up": nrm(ks[25], (DEPTH, MOE_EXPERTS, D_MODEL, MOE_FF), D_MODEL ** -0.5),
        "w_exp_down": nrm(ks[26], (DEPTH, MOE_EXPERTS, MOE_FF, D_MODEL), MOE_FF ** -0.5),
        "norm_final": 1.0 + nrm(ks[27], (D_MODEL,), 0.02),
    }


def reference(x, meta_tokens, norm_mix, w_in, ssd_conv_w, ssd_conv_b, ssd_dt_bias, ssd_a_log,
              ssd_d, ssd_norm, w_ssd_out, lru_conv_w, lru_conv_b, lru_wa, lru_ba, lru_wx, lru_bx,
              lru_lambda, w_lru_out, gate_bias, w_out, norm_ffn, w_router_group, w_router_expert,
              w_exp_gate, w_exp_up, w_exp_down, norm_final):
    b = x.shape[0]
    meta = jnp.broadcast_to(meta_tokens.astype(x.dtype)[None], (b, N_META, D_MODEL))
    h = jnp.concatenate([meta, x], axis=1)
    l = h.shape[1]
    for layer in range(DEPTH):
        u = _rmsnorm(h, norm_mix[layer])
        proj = u @ w_in[layer]
        z, xbc, dt_raw, lx, ly, gl = jnp.split(proj, [COL_Z, COL_XBC, COL_DT, COL_LX, COL_LY], axis=-1)
        y_ssd = _ssd_branch(z, xbc, dt_raw, ssd_conv_w[layer], ssd_conv_b[layer], ssd_dt_bias[layer],
                            ssd_a_log[layer], ssd_d[layer], ssd_norm[layer]) @ w_ssd_out[layer]
        y_lru = _rglru_branch(lx, ly, lru_conv_w[layer], lru_conv_b[layer], lru_wa[layer], lru_ba[layer],
                              lru_wx[layer], lru_bx[layer], lru_lambda[layer]) @ w_lru_out[layer]
        gates = jax.nn.sigmoid((gl.reshape(b, l, N_BRANCH, D_MODEL) + gate_bias[layer]).astype(jnp.float32)).astype(h.dtype)
        mixed = gates[..., 0, :] * y_ssd + gates[..., 1, :] * y_lru
        h = h + mixed @ w_out[layer]
        h = h + _moe(_rmsnorm(h, norm_ffn[layer]), w_router_group[layer], w_router_expert[layer],
                     w_exp_gate[layer], w_exp_up[layer], w_exp_down[layer])
    return _rmsnorm(h, norm_final)[:, N_META:]
```

```python
import functools

import numpy as np
import jax
import jax.numpy as jnp
from jax import lax
from jax.experimental import pallas as pl
from jax.experimental.pallas import tpu as pltpu

N_META = 16
CONV_K = 4
EPS = 1e-6
SSD_HEAD_DIM = 64
SSD_HEAD_SHIFT = 6
SSD_GROUPS = 8
SSD_STATE = 128
SSD_CHUNK = 128
LRU_HEADS = 8
LRU_C = 8.0
N_BRANCH = 2
MOE_GROUPS = 8
MOE_EXP_PER_GROUP = 8
MOE_EXPERTS = MOE_GROUPS * MOE_EXP_PER_GROUP
MOE_TOPK = 2
MOE_BLOCK = 128

LANES = 128
SUBLANES = 8
VMEM_LIMIT = 56 * 1024 * 1024
MXU_DTYPE = jnp.bfloat16

EXPERT_ROWS = 1024
EXPERT_SUB = 256
EXPERT_FF_TILE = 256


def _pick(n, options):
    for o in options:
        if n % o == 0:
            return o
    raise ValueError(f"no tile in {options} divides {n}")


def _params(sem, vmem=VMEM_LIMIT):
    return pltpu.CompilerParams(dimension_semantics=sem, vmem_limit_bytes=vmem)


def _dot(a, b):
    return jnp.dot(a.astype(MXU_DTYPE), b.astype(MXU_DTYPE), preferred_element_type=jnp.float32)


def _dot_exact_rhs(v, sel):
    sel = sel.astype(jnp.bfloat16)
    hi = v.astype(jnp.bfloat16)
    r1 = v - hi.astype(jnp.float32)
    mid = r1.astype(jnp.bfloat16)
    lo = (r1 - mid.astype(jnp.float32)).astype(jnp.bfloat16)
    f32 = jnp.float32
    return (jnp.dot(hi, sel, preferred_element_type=f32) + jnp.dot(mid, sel, preferred_element_type=f32)
            + jnp.dot(lo, sel, preferred_element_type=f32))


def _dot_exact_lhs(sel, v):
    sel = sel.astype(jnp.bfloat16)
    hi = v.astype(jnp.bfloat16)
    r1 = v - hi.astype(jnp.float32)
    mid = r1.astype(jnp.bfloat16)
    lo = (r1 - mid.astype(jnp.float32)).astype(jnp.bfloat16)
    f32 = jnp.float32
    return (jnp.dot(sel, hi, preferred_element_type=f32) + jnp.dot(sel, mid, preferred_element_type=f32)
            + jnp.dot(sel, lo, preferred_element_type=f32))


def _softplus(x):
    return jnp.maximum(x, 0.0) + jnp.log1p(jnp.exp(-jnp.abs(x)))


def _sigmoid(x):
    return 1.0 / (1.0 + jnp.exp(-x))


def _silu(x):
    return x * _sigmoid(x)


def _causal_conv(x, w_ref, b_ref):
    acc = b_ref[...] + w_ref[CONV_K - 1:CONV_K, :] * x
    for k in range(1, CONV_K):
        acc = acc + w_ref[CONV_K - 1 - k:CONV_K - k, :] * pltpu.roll(x, k, 0)
    return acc


def _norm_inproj_kernel(x_ref, g_ref, w_ref, wdt_ref, o_ref, dt_ref, xn_ref):
    @pl.when(pl.program_id(1) == 0)
    def _():
        x = x_ref[...]
        y = x * lax.rsqrt(jnp.mean(x * x, axis=-1, keepdims=True) + EPS)
        xn_ref[...] = (y * g_ref[...]).astype(xn_ref.dtype)
        dt_ref[...] = jnp.dot(xn_ref[...], wdt_ref[...], preferred_element_type=jnp.float32)

    o_ref[...] = jnp.dot(xn_ref[...], w_ref[...], preferred_element_type=jnp.float32)


def _norm_inproj(hp, g, w, wdt):
    tp, d = hp.shape
    n = w.shape[1]
    tm = _pick(tp, (1024, 512, 256, 128))
    tn = _pick(n, (512, 256, 128))
    return pl.pallas_call(
        _norm_inproj_kernel,
        grid=(tp // tm, n // tn),
        in_specs=[
            pl.BlockSpec((tm, d), lambda i, j: (i, 0)),
            pl.BlockSpec((1, d), lambda i, j: (0, 0)),
            pl.BlockSpec((d, tn), lambda i, j: (0, j)),
            pl.BlockSpec((d, LANES), lambda i, j: (0, 0)),
        ],
        out_specs=[
            pl.BlockSpec((tm, tn), lambda i, j: (i, j)),
            pl.BlockSpec((tm, LANES), lambda i, j: (i, 0)),
        ],
        out_shape=[
            jax.ShapeDtypeStruct((tp, n), jnp.float32),
            jax.ShapeDtypeStruct((tp, LANES), jnp.float32),
        ],
        scratch_shapes=[pltpu.VMEM((tm, d), MXU_DTYPE)],
        compiler_params=_params(("parallel", "arbitrary")),
        name="norm_inproj",
    )(hp, g, w, wdt)


def _ssd_kernel(n_pad, z_ref, xs_ref, b_ref, c_ref, dt_ref,
                cwx_ref, cwb_ref, cwc_ref, cbx_ref, cbb_ref, cbc_ref,
                dtb_ref, a_ref, d_ref, nw_ref, o_ref,
                xs_s, b_s, c_s, dt_s, adt_s, state_s):
    g = pl.program_id(1)
    lp = xs_ref.shape[1]
    q = SSD_CHUNK
    gw = xs_ref.shape[2]
    hpg = gw // SSD_HEAD_DIM
    f32 = jnp.float32

    xs_s[...] = _silu(_causal_conv(xs_ref[0], cwx_ref, cbx_ref))
    b_s[...] = _silu(_causal_conv(b_ref[0], cwb_ref, cbb_ref))
    c_s[...] = _silu(_causal_conv(c_ref[0], cwc_ref, cbc_ref))

    dt = _softplus(dt_ref[0] + dtb_ref[...])
    rows = lax.broadcasted_iota(jnp.int32, (LANES, gw), 0)
    cols = lax.broadcasted_iota(jnp.int32, (LANES, gw), 1)
    expand = (rows == g * hpg + lax.shift_right_logical(cols, SSD_HEAD_SHIFT)).astype(f32)
    dt_g = _dot_exact_rhs(dt, expand)
    dt_s[...] = dt_g
    adt_s[...] = dt_g * a_ref[...]

    state_s[...] = jnp.zeros_like(state_s)
    li = lax.broadcasted_iota(jnp.int32, (q, q), 0)
    si = lax.broadcasted_iota(jnp.int32, (q, q), 1)
    causal = li >= si
    tri = causal.astype(f32)
    lane = lax.broadcasted_iota(jnp.int32, (q, gw), 1)
    row = lax.broadcasted_iota(jnp.int32, (q, gw), 0)

    def chunk(c, carry):
        r0 = pl.multiple_of(c * q, q)
        sl = pl.ds(r0, q)
        xs = xs_s[sl, :]
        bm = b_s[sl, :]
        cm = c_s[sl, :]
        xdt = jnp.where(row + r0 >= n_pad, xs * dt_s[sl, :], 0.0)
        acum = _dot_exact_lhs(tri, adt_s[sl, :])
        acum_t = acum.T
        a_last = acum[q - 1:q, :]
        state = state_s[...]

        cb = lax.dot_general(cm.astype(MXU_DTYPE), bm.astype(MXU_DTYPE),
                             (((1,), (1,)), ((), ())), preferred_element_type=f32)
        y = _dot(cm, state) * jnp.exp(acum)
        for j in range(hpg):
            c0 = j * SSD_HEAD_DIM
            seg = acum[:, c0:c0 + 1] - acum_t[c0:c0 + 1, :]
            decay = jnp.exp(jnp.where(causal, seg, -jnp.inf))
            in_head = (lane >= c0) & (lane < c0 + SSD_HEAD_DIM)
            y = y + _dot(cb * decay, jnp.where(in_head, xdt, 0.0))
        y = y + d_ref[...] * xs

        state_s[...] = state * jnp.exp(a_last) + _dot(bm.T, xdt * jnp.exp(a_last - acum))

        yz = y * _silu(z_ref[0, sl, :])
        yn = yz * lax.rsqrt(jnp.mean(yz * yz, axis=-1, keepdims=True) + EPS)
        o_ref[0, sl, :] = (yn * nw_ref[...]).astype(o_ref.dtype)
        return carry

    lax.fori_loop(0, lp // q, chunk, 0)


def _ssd(proj3, dt3, n_pad, cw, cb, dt_bias, a_log, d_skip, norm_w, d_inner):
    b, lp, _ = proj3.shape
    gw = d_inner // SSD_GROUPS
    hpg = gw // SSD_HEAD_DIM
    ns = SSD_STATE
    xs_blk0 = d_inner // gw
    b_blk0 = 2 * d_inner // ns
    c_blk0 = (2 * d_inner + SSD_GROUPS * ns) // ns
    heads = d_inner // SSD_HEAD_DIM
    f32 = jnp.float32

    cwx = cw[:, :d_inner]
    cwb = cw[:, d_inner:d_inner + SSD_GROUPS * ns]
    cwc = cw[:, d_inner + SSD_GROUPS * ns:]
    cbx = cb[None, :d_inner]
    cbb = cb[None, d_inner:d_inner + SSD_GROUPS * ns]
    cbc = cb[None, d_inner + SSD_GROUPS * ns:]
    dtb = jnp.zeros((1, LANES), f32).at[0, :heads].set(dt_bias)
    a_ch = jnp.repeat(-jnp.exp(a_log.astype(f32)), SSD_HEAD_DIM)[None, :]
    d_ch = jnp.repeat(d_skip, SSD_HEAD_DIM)[None, :]
    nw = norm_w[None, :]

    seq = lambda w, off: pl.BlockSpec((1, lp, w), lambda i, g: (i, 0, off + g))
    vec = lambda w: pl.BlockSpec((1, w), lambda i, g: (0, g))
    cwspec = lambda w: pl.BlockSpec((CONV_K, w), lambda i, g: (0, g))
    return pl.pallas_call(
        functools.partial(_ssd_kernel, n_pad),
        grid=(b, SSD_GROUPS),
        in_specs=[
            seq(gw, 0), seq(gw, xs_blk0), seq(ns, b_blk0), seq(ns, c_blk0),
            pl.BlockSpec((1, lp, LANES), lambda i, g: (i, 0, 0)),
            cwspec(gw), cwspec(ns), cwspec(ns), vec(gw), vec(ns), vec(ns),
            pl.BlockSpec((1, LANES), lambda i, g: (0, 0)),
            vec(gw), vec(gw), vec(gw),
        ],
        out_specs=pl.BlockSpec((1, lp, gw), lambda i, g: (i, 0, g)),
        out_shape=jax.ShapeDtypeStruct((b, lp, d_inner), MXU_DTYPE),
        scratch_shapes=[
            pltpu.VMEM((lp, gw), f32), pltpu.VMEM((lp, ns), f32), pltpu.VMEM((lp, ns), f32),
            pltpu.VMEM((lp, gw), f32), pltpu.VMEM((lp, gw), f32), pltpu.VMEM((ns, gw), f32),
        ],
        compiler_params=_params(("parallel", "parallel")),
        name="ssd",
    )(proj3, proj3, proj3, proj3, dt3, cwx, cwb, cwc, cbx, cbb, cbc, dtb, a_ch, d_ch, nw)


def _lru_kernel(n_pad, lx_ref, ly_ref, cw_ref, cb_ref, wa_ref, ba_ref, wx_ref, bx_ref, lam_ref, o_ref,
                a_s, u_s, h_s):
    lp = lx_ref.shape[1]
    w = lx_ref.shape[2]
    xr = _causal_conv(lx_ref[0], cw_ref, cb_ref)
    gate_r = _sigmoid(_dot(xr, wa_ref[0]) + ba_ref[...])
    gate_i = _sigmoid(_dot(xr, wx_ref[0]) + bx_ref[...])
    log_a = (-LRU_C) * gate_r * _softplus(-lam_ref[...])
    a = jnp.exp(log_a)
    a_s[...] = a
    mult = jnp.sqrt(jnp.tanh(-log_a) * (a * a + 1.0))
    rows = lax.broadcasted_iota(jnp.int32, (lp, w), 0)
    u_s[...] = jnp.where(rows >= n_pad, mult * gate_i * xr, 0.0)

    sub = lax.broadcasted_iota(jnp.int32, (SUBLANES, w), 0)

    def tile(t, h_prev):
        sl = pl.ds(pl.multiple_of(t * SUBLANES, SUBLANES), SUBLANES)
        a = a_s[sl, :]
        u = u_s[sl, :]
        for d in (1, 2, 4):
            keep = sub >= d
            u = jnp.where(keep, a * pltpu.roll(u, d, 0) + u, u)
            a = jnp.where(keep, a * pltpu.roll(a, d, 0), a)
        h = a * h_prev + u
        h_s[sl, :] = h
        return jnp.broadcast_to(h[SUBLANES - 1:SUBLANES, :], (SUBLANES, w))

    lax.fori_loop(0, lp // SUBLANES, tile, jnp.zeros((SUBLANES, w), jnp.float32))
    o_ref[0] = (h_s[...] * jax.nn.gelu(ly_ref[0])).astype(o_ref.dtype)


def _lru(proj3, n_pad, lx_col, ly_col, cw, cb, wa, ba, wx, bx, lam, width):
    b, lp, _ = proj3.shape
    w = width // LRU_HEADS
    seq = lambda off: pl.BlockSpec((1, lp, w), lambda i, h: (i, 0, off + h))
    vec = pl.BlockSpec((1, w), lambda i, h: (0, h))
    mat = pl.BlockSpec((1, w, w), lambda i, h: (h, 0, 0))
    return pl.pallas_call(
        functools.partial(_lru_kernel, n_pad),
        grid=(b, LRU_HEADS),
        in_specs=[seq(lx_col // w), seq(ly_col // w),
                  pl.BlockSpec((CONV_K, w), lambda i, h: (0, h)), vec, mat, vec, mat, vec, vec],
        out_specs=pl.BlockSpec((1, lp, w), lambda i, h: (i, 0, h)),
        out_shape=jax.ShapeDtypeStruct((b, lp, width), MXU_DTYPE),
        scratch_shapes=[pltpu.VMEM((lp, w), jnp.float32)] * 3,
        compiler_params=_params(("parallel", "parallel")),
        name="rglru",
    )(proj3, proj3, cw, cb[None, :], wa, ba[None, :], wx, bx[None, :], lam[None, :])


def _merge_kernel(ys_ref, yl_ref, ws_ref, wl_ref, g0_ref, g1_ref, gb_ref, o_ref):
    y_ssd = jnp.dot(ys_ref[...], ws_ref[...], preferred_element_type=jnp.float32)
    y_lru = jnp.dot(yl_ref[...], wl_ref[...], preferred_element_type=jnp.float32)
    gate0 = _sigmoid(g0_ref[...] + gb_ref[0:1, :])
    gate1 = _sigmoid(g1_ref[...] + gb_ref[1:2, :])
    o_ref[...] = (gate0 * y_ssd + gate1 * y_lru).astype(o_ref.dtype)


def _merge(ys, yl, ws, wl, proj, gate_col, gate_bias):
    tp, d = ys.shape
    tm = _pick(tp, (1024, 512, 256, 128))
    tn = _pick(d, (512, 256, 128))
    g0 = gate_col // tn
    g1 = (gate_col + d) // tn
    return pl.pallas_call(
        _merge_kernel,
        grid=(tp // tm, d // tn),
        in_specs=[
            pl.BlockSpec((tm, d), lambda i, j: (i, 0)),
            pl.BlockSpec((tm, d), lambda i, j: (i, 0)),
            pl.BlockSpec((d, tn), lambda i, j: (0, j)),
            pl.BlockSpec((d, tn), lambda i, j: (0, j)),
            pl.BlockSpec((tm, tn), lambda i, j: (i, g0 + j)),
            pl.BlockSpec((tm, tn), lambda i, j: (i, g1 + j)),
            pl.BlockSpec((N_BRANCH, tn), lambda i, j: (0, j)),
        ],
        out_specs=pl.BlockSpec((tm, tn), lambda i, j: (i, j)),
        out_shape=jax.ShapeDtypeStruct((tp, d), MXU_DTYPE),
        compiler_params=_params(("parallel", "arbitrary")),
        name="merge",
    )(ys, yl, ws, wl, proj, proj, gate_bias)


def _first_index_of_max(p, valid, lane):
    pm = jnp.where(valid, p, -jnp.inf)
    top = jnp.max(pm, axis=-1, keepdims=True)
    idx = jnp.min(jnp.where(valid & (pm == top), lane, LANES), axis=-1, keepdims=True)
    return top, idx


def _masked_softmax(x, valid):
    m = jnp.max(jnp.where(valid, x, -jnp.inf), axis=-1, keepdims=True)
    e = jnp.where(valid, jnp.exp(x - m), 0.0)
    return e / jnp.sum(e, axis=-1, keepdims=True)


def _outproj_router_kernel(mix_ref, h_ref, wo_ref, g_ref, wr_ref, h2_ref, u_ref, eid_ref, gate_ref):
    h2 = h_ref[...] + jnp.dot(mix_ref[...], wo_ref[...], preferred_element_type=jnp.float32)
    h2_ref[...] = h2
    u = h2 * lax.rsqrt(jnp.mean(h2 * h2, axis=-1, keepdims=True) + EPS) * g_ref[...]
    u_ref[...] = u
    logits = _dot(u, wr_ref[...])
    lane = lax.broadcasted_iota(jnp.int32, logits.shape, 1)

    g_prob = _masked_softmax(logits, lane < MOE_GROUPS)
    g_p, g_idx = _first_index_of_max(g_prob, lane < MOE_GROUPS, lane)

    e_lo = MOE_GROUPS + g_idx * MOE_EXP_PER_GROUP
    in_group = (lane >= e_lo) & (lane < e_lo + MOE_EXP_PER_GROUP)
    e_prob = _masked_softmax(logits, in_group)
    p1, i1 = _first_index_of_max(e_prob, in_group, lane)
    rest = in_group & (lane != i1)
    p2, i2 = _first_index_of_max(e_prob, rest, lane)
    denom = p1 + p2

    col = lax.broadcasted_iota(jnp.int32, eid_ref.shape, 1)
    eid_ref[...] = jnp.where(col == 0, i1, i2) - MOE_GROUPS
    gate_ref[...] = jnp.where(col == 0, g_p * p1 / denom, g_p * p2 / denom)


def _outproj_router(mixed, hp, wo, g_ffn, wr):
    tp, d = hp.shape
    tm = _pick(tp, (256, 128))
    row = lambda w: pl.BlockSpec((tm, w), lambda i: (i, 0))
    return pl.pallas_call(
        _outproj_router_kernel,
        grid=(tp // tm,),
        in_specs=[row(d), row(d),
                  pl.BlockSpec((d, d), lambda i: (0, 0)),
                  pl.BlockSpec((1, d), lambda i: (0, 0)),
                  pl.BlockSpec((d, LANES), lambda i: (0, 0))],
        out_specs=[row(d), row(d), row(MOE_TOPK), row(MOE_TOPK)],
        out_shape=[jax.ShapeDtypeStruct((tp, d), jnp.float32),
                   jax.ShapeDtypeStruct((tp, d), jnp.float32),
                   jax.ShapeDtypeStruct((tp, MOE_TOPK), jnp.int32),
                   jax.ShapeDtypeStruct((tp, MOE_TOPK), jnp.float32)],
        compiler_params=_params(("parallel",)),
        name="outproj_router",
    )(mixed, hp, wo, g_ffn, wr)


def _expert_kernel(n_batch, batch_slots, pad_slots,
                   n_items_ref, item_e_ref, item_start_ref, item_rows_ref, item_real_ref, slot_ref,
                   u_hbm, w1_ref, w3_ref, w2_ref, y_hbm,
                   xf_s, xb_s, acc_s, w1_s, w3_s, w2_s, sem):
    i = pl.program_id(0)
    f = pl.program_id(1)
    nf = pl.num_programs(1)
    active = i < n_items_ref[0]
    start = item_start_ref[i]
    n_rows = item_rows_ref[i]
    n_real = item_real_ref[i]

    def row_copy_in(r):
        tok = lax.shift_right_logical(slot_ref[start + r], 1)
        return pltpu.make_async_copy(u_hbm.at[pl.ds(tok, 1)], xf_s.at[pl.ds(r, 1)], sem.at[0])

    def row_copy_out(r):
        return pltpu.make_async_copy(acc_s.at[pl.ds(r, 1)], y_hbm.at[pl.ds(slot_ref[start + r], 1)], sem.at[1])

    @pl.when((i == 0) & (f == 0))
    def _():
        xf_s[...] = jnp.zeros_like(xf_s)
        fills = [pltpu.make_async_copy(xf_s.at[pl.ds(0, pad_slots)], y_hbm.at[pl.ds(b * batch_slots, pad_slots)],
                                       sem.at[1]) for b in range(n_batch)]
        for cp in fills:
            cp.start()
        for cp in fills:
            cp.wait()

    @pl.when(active & (f == 0))
    def _():
        def issue(r, c):
            row_copy_in(r).start()
            return c
        lax.fori_loop(0, n_real, issue, 0)

        def drain(r, c):
            row_copy_in(r).wait()
            return c
        lax.fori_loop(0, n_real, drain, 0)
        xb_s[...] = xf_s[...].astype(xb_s.dtype)

    @pl.when(active)
    def _():
        w1_s[...] = w1_ref[0].astype(w1_s.dtype)
        w3_s[...] = w3_ref[0].astype(w3_s.dtype)
        w2_s[...] = w2_ref[0].astype(w2_s.dtype)
        for sb in range(EXPERT_ROWS // EXPERT_SUB):
            @pl.when(sb * EXPERT_SUB < n_rows)
            def _():
                sl = pl.ds(sb * EXPERT_SUB, EXPERT_SUB)
                x = xb_s[sl, :]
                a = jnp.dot(x, w1_s[...], preferred_element_type=jnp.float32)
                b = jnp.dot(x, w3_s[...], preferred_element_type=jnp.float32)
                hdn = (_silu(a) * b).astype(w2_s.dtype)
                part = jnp.dot(hdn, w2_s[...], preferred_element_type=jnp.float32)

                @pl.when(f == 0)
                def _():
                    acc_s[sl, :] = part

                @pl.when(f > 0)
                def _():
                    acc_s[sl, :] += part

    @pl.when(active & (f == nf - 1))
    def _():
        def issue(r, c):
            row_copy_out(r).start()
            return c
        lax.fori_loop(0, n_real, issue, 0)

        def drain(r, c):
            row_copy_out(r).wait()
            return c
        lax.fori_loop(0, n_real, drain, 0)


def _experts(u, sched, w1, w3, w2, n_items_max, n_batch, lp, n_pad):
    tp, d = u.shape
    assert n_pad * MOE_TOPK <= EXPERT_ROWS
    ff = w1.shape[2]
    tf = EXPERT_FF_TILE
    nf = ff // tf
    n_items, item_e, item_start, item_rows, item_real, slot = sched

    def w_in_map(i, f, n_items, item_e, *_):
        return (item_e[i], 0, jnp.where(i < n_items[0], f, nf - 1))

    def w_out_map(i, f, n_items, item_e, *_):
        return (item_e[i], jnp.where(i < n_items[0], f, nf - 1), 0)

    grid_spec = pltpu.PrefetchScalarGridSpec(
        num_scalar_prefetch=6,
        grid=(n_items_max, nf),
        in_specs=[
            pl.BlockSpec(memory_space=pl.ANY),
            pl.BlockSpec((1, d, tf), w_in_map),
            pl.BlockSpec((1, d, tf), w_in_map),
            pl.BlockSpec((1, tf, d), w_out_map),
        ],
        out_specs=pl.BlockSpec(memory_space=pl.ANY),
        scratch_shapes=[
            pltpu.VMEM((EXPERT_ROWS, d), jnp.float32),
            pltpu.VMEM((EXPERT_ROWS, d), MXU_DTYPE),
            pltpu.VMEM((EXPERT_ROWS, d), jnp.float32),
            pltpu.VMEM((d, tf), MXU_DTYPE),
            pltpu.VMEM((d, tf), MXU_DTYPE),
            pltpu.VMEM((tf, d), MXU_DTYPE),
            pltpu.SemaphoreType.DMA((2,)),
        ],
    )
    return pl.pallas_call(
        functools.partial(_expert_kernel, n_batch, lp * MOE_TOPK, n_pad * MOE_TOPK),
        grid_spec=grid_spec,
        out_shape=jax.ShapeDtypeStruct((tp * MOE_TOPK, d), jnp.float32),
        compiler_params=_params(("arbitrary", "arbitrary")),
        name="experts",
    )(n_items, item_e, item_start, item_rows, item_real, slot, u, w1, w3, w2)


def _expert_schedule(eid, tok_rows, n_items_max):
    i32 = jnp.int32
    n_asg = tok_rows.shape[0] * MOE_TOPK
    p_len = -(-(n_asg + MOE_EXPERTS * (MOE_BLOCK - 1)) // MOE_BLOCK) * MOE_BLOCK
    flat_e = eid[tok_rows].reshape(-1)
    flat_slot = (jnp.asarray(tok_rows, i32)[:, None] * MOE_TOPK + jnp.arange(MOE_TOPK, dtype=i32)).reshape(-1)
    onehot = (flat_e[:, None] == jnp.arange(MOE_EXPERTS, dtype=i32)[None, :]).astype(i32)
    csum = jnp.cumsum(onehot, axis=0)
    counts = csum[-1]
    rank = jnp.take_along_axis(csum, flat_e[:, None], axis=1)[:, 0] - 1
    padded = (counts + MOE_BLOCK - 1) // MOE_BLOCK * MOE_BLOCK
    pad_end = jnp.cumsum(padded)
    pad_start = pad_end - padded
    dest = pad_start[flat_e] + rank
    slot = jnp.zeros((p_len,), i32).at[dest].set(flat_slot)

    chunks = (padded + EXPERT_ROWS - 1) // EXPERT_ROWS
    chunk_end = jnp.cumsum(chunks)
    item = jnp.arange(n_items_max, dtype=i32)
    item_e = jnp.minimum(jnp.searchsorted(chunk_end, item, side='right'), MOE_EXPERTS - 1).astype(i32)
    n_items = chunk_end[-1:].astype(i32)
    last_e = item_e[jnp.maximum(n_items[0] - 1, 0)]
    item_e = jnp.where(item < n_items[0], item_e, last_e)
    k = item - (chunk_end - chunks)[item_e]
    item_start = (pad_start[item_e] + k * EXPERT_ROWS).astype(i32)
    item_rows = jnp.clip(padded[item_e] - k * EXPERT_ROWS, 0, EXPERT_ROWS).astype(i32)
    item_real = jnp.clip(counts[item_e] - k * EXPERT_ROWS, 0, EXPERT_ROWS).astype(i32)
    live = item < n_items[0]
    item_start = jnp.where(live, item_start, 0)
    item_rows = jnp.where(live, item_rows, 0)
    item_real = jnp.where(live, item_real, 0)
    return n_items, item_e, item_start, item_rows, item_real, slot


def _final_kernel(h_ref, y_ref, gate_ref, g_ref, o_ref):
    d = h_ref.shape[1]
    moe = gate_ref[:, 0:1] * y_ref[:, :d] + gate_ref[:, 1:2] * y_ref[:, d:]
    h = h_ref[...] + moe
    o_ref[...] = h * lax.rsqrt(jnp.mean(h * h, axis=-1, keepdims=True) + EPS) * g_ref[...]


def _final(h2, y2, gates, g_final, batch, lp, seq):
    tp, d = h2.shape
    tm = SSD_CHUNK
    skip = (lp - seq) // tm
    per_b = seq // tm
    src = lambda w: pl.BlockSpec((tm, w), lambda b, c: (b * (lp // tm) + skip + c, 0))
    return pl.pallas_call(
        _final_kernel,
        grid=(batch, per_b),
        in_specs=[src(d), src(MOE_TOPK * d), src(MOE_TOPK), pl.BlockSpec((1, d), lambda b, c: (0, 0))],
        out_specs=pl.BlockSpec((tm, d), lambda b, c: (b * per_b + c, 0)),
        out_shape=jax.ShapeDtypeStruct((batch * seq, d), jnp.float32),
        compiler_params=_params(("parallel", "parallel")),
        name="final_norm",
    )(h2, y2.reshape(tp, MOE_TOPK * d), gates, g_final)


def kernel(x, meta_tokens, norm_mix, w_in, ssd_conv_w, ssd_conv_b, ssd_dt_bias, ssd_a_log, ssd_d, ssd_norm, w_ssd_out, lru_conv_w, lru_conv_b, lru_wa, lru_ba, lru_wx, lru_bx, lru_lambda, w_lru_out, gate_bias, w_out, norm_ffn, w_router_group, w_router_expert, w_exp_gate, w_exp_up, w_exp_down, norm_final):
    batch, seq, d = x.shape
    depth = norm_mix.shape[0]
    assert depth == 1 and seq % SSD_CHUNK == 0
    l = N_META + seq
    lp = -(-l // SSD_CHUNK) * SSD_CHUNK
    n_pad = lp - l
    d_inner = d
    heads = d_inner // SSD_HEAD_DIM
    conv_dim = d_inner + 2 * SSD_GROUPS * SSD_STATE
    col_xbc = d_inner
    col_dt = col_xbc + conv_dim
    col_lx = col_dt + heads
    col_ly = col_lx + d
    col_gl = col_ly + d
    f32 = jnp.float32
    lyr = 0

    meta = jnp.broadcast_to(meta_tokens.astype(f32)[None], (batch, N_META, d))
    hp = jnp.concatenate([jnp.zeros((batch, n_pad, d), f32), meta, x], axis=1).reshape(batch * lp, d)

    w = w_in[lyr]
    w_main = jnp.concatenate([w[:, :col_dt], w[:, col_lx:]], axis=1).astype(MXU_DTYPE)
    w_dt = jnp.zeros((d, LANES), f32).at[:, :heads].set(w[:, col_dt:col_lx]).astype(MXU_DTYPE)
    proj, dt_raw = _norm_inproj(hp, norm_mix[lyr][None, :], w_main, w_dt)
    n_main = w_main.shape[1]
    proj3 = proj.reshape(batch, lp, n_main)
    lx_col = col_dt
    ly_col = lx_col + d
    gl_col = ly_col + d

    ys = _ssd(proj3, dt_raw.reshape(batch, lp, LANES), n_pad, ssd_conv_w[lyr], ssd_conv_b[lyr], ssd_dt_bias[lyr],
              ssd_a_log[lyr], ssd_d[lyr], ssd_norm[lyr], d_inner)
    yl = _lru(proj3, n_pad, lx_col, ly_col, lru_conv_w[lyr], lru_conv_b[lyr], lru_wa[lyr].astype(MXU_DTYPE),
              lru_ba[lyr], lru_wx[lyr].astype(MXU_DTYPE), lru_bx[lyr], lru_lambda[lyr], d)

    mixed = _merge(ys.reshape(batch * lp, d), yl.reshape(batch * lp, d), w_ssd_out[lyr].astype(MXU_DTYPE),
                   w_lru_out[lyr].astype(MXU_DTYPE), proj, gl_col, gate_bias[lyr])

    w_router = jnp.zeros((d, LANES), f32)
    w_router = w_router.at[:, :MOE_GROUPS].set(w_router_group[lyr])
    w_router = w_router.at[:, MOE_GROUPS:MOE_GROUPS + MOE_EXPERTS].set(w_router_expert[lyr])
    h2, u2, eid, gates = _outproj_router(mixed, hp, w_out[lyr].astype(MXU_DTYPE), norm_ffn[lyr][None, :],
                                         w_router.astype(MXU_DTYPE))

    tok_rows = np.nonzero((np.arange(batch * lp) % lp) >= n_pad)[0].astype(np.int32)
    n_asg = tok_rows.shape[0] * MOE_TOPK
    p_max = n_asg + MOE_EXPERTS * (MOE_BLOCK - 1)
    n_items_max = MOE_EXPERTS + p_max // EXPERT_ROWS
    sched = _expert_schedule(eid, tok_rows, n_items_max)
    ff = w_exp_gate.shape[-1]
    y2 = _experts(u2, sched, w_exp_gate.reshape(MOE_EXPERTS, d, ff), w_exp_up.reshape(MOE_EXPERTS, d, ff),
                  w_exp_down.reshape(MOE_EXPERTS, ff, d), n_items_max, batch, lp, n_pad)

    out = _final(h2, y2, gates, norm_final[None, :], batch, lp, seq)
    return out.reshape(batch, seq, d)
```

```python
import functools

import numpy as np
import jax
import jax.numpy as jnp
from jax import lax
from jax.experimental import pallas as pl
from jax.experimental.pallas import tpu as pltpu

N_META = 16
CONV_K = 4
EPS = 1e-6
SSD_HEAD_DIM = 64
SSD_HEAD_SHIFT = 6
SSD_GROUPS = 8
SSD_STATE = 128
SSD_CHUNK = 128
LRU_HEADS = 8
LRU_C = 8.0
N_BRANCH = 2
MOE_GROUPS = 8
MOE_EXP_PER_GROUP = 8
MOE_EXPERTS = MOE_GROUPS * MOE_EXP_PER_GROUP
MOE_TOPK = 2
MOE_BLOCK = 128

LANES = 128
SUBLANES = 8
VMEM_LIMIT = 56 * 1024 * 1024
MXU_DTYPE = jnp.bfloat16

EXPERT_ROWS = 1024
EXPERT_FF_TILE = 512
EXPERT_OUT_TILE = 1024
GATHER_UNROLL = 8


def _pick(n, options):
    for o in options:
        if n % o == 0:
            return o
    raise ValueError(f"no tile in {options} divides {n}")


def _params(sem, vmem=VMEM_LIMIT):
    return pltpu.CompilerParams(dimension_semantics=sem, vmem_limit_bytes=vmem)


def _dot(a, b):
    return jnp.dot(a.astype(MXU_DTYPE), b.astype(MXU_DTYPE), preferred_element_type=jnp.float32)


def _dot_exact_rhs(v, sel):
    sel = sel.astype(jnp.bfloat16)
    hi = v.astype(jnp.bfloat16)
    r1 = v - hi.astype(jnp.float32)
    mid = r1.astype(jnp.bfloat16)
    lo = (r1 - mid.astype(jnp.float32)).astype(jnp.bfloat16)
    f32 = jnp.float32
    return (jnp.dot(hi, sel, preferred_element_type=f32) + jnp.dot(mid, sel, preferred_element_type=f32)
            + jnp.dot(lo, sel, preferred_element_type=f32))


def _dot_exact_lhs(sel, v):
    sel = sel.astype(jnp.bfloat16)
    hi = v.astype(jnp.bfloat16)
    r1 = v - hi.astype(jnp.float32)
    mid = r1.astype(jnp.bfloat16)
    lo = (r1 - mid.astype(jnp.float32)).astype(jnp.bfloat16)
    f32 = jnp.float32
    return (jnp.dot(sel, hi, preferred_element_type=f32) + jnp.dot(sel, mid, preferred_element_type=f32)
            + jnp.dot(sel, lo, preferred_element_type=f32))


def _softplus(x):
    return jnp.maximum(x, 0.0) + jnp.log1p(jnp.exp(-jnp.abs(x)))


def _sigmoid(x):
    return 1.0 / (1.0 + jnp.exp(-x))


def _silu(x):
    return x * _sigmoid(x)


def _causal_conv(x, w_ref, b_ref):
    acc = b_ref[...] + w_ref[CONV_K - 1:CONV_K, :] * x
    for k in range(1, CONV_K):
        acc = acc + w_ref[CONV_K - 1 - k:CONV_K - k, :] * pltpu.roll(x, k, 0)
    return acc


def _norm_inproj_kernel(x_ref, g_ref, w_ref, wdt_ref, o_ref, dt_ref, xn_ref):
    @pl.when(pl.program_id(1) == 0)
    def _():
        x = x_ref[...]
        y = x * lax.rsqrt(jnp.mean(x * x, axis=-1, keepdims=True) + EPS)
        xn_ref[...] = (y * g_ref[...]).astype(xn_ref.dtype)
        dt_ref[...] = jnp.dot(xn_ref[...], wdt_ref[...], preferred_element_type=jnp.float32)

    o_ref[...] = jnp.dot(xn_ref[...], w_ref[...], preferred_element_type=jnp.float32)


def _norm_inproj(hp, g, w, wdt):
    tp, d = hp.shape
    n = w.shape[1]
    tm = _pick(tp, (1024, 512, 256, 128))
    tn = _pick(n, (512, 256, 128))
    return pl.pallas_call(
        _norm_inproj_kernel,
        grid=(tp // tm, n // tn),
        in_specs=[
            pl.BlockSpec((tm, d), lambda i, j: (i, 0)),
            pl.BlockSpec((1, d), lambda i, j: (0, 0)),
            pl.BlockSpec((d, tn), lambda i, j: (0, j)),
            pl.BlockSpec((d, LANES), lambda i, j: (0, 0)),
        ],
        out_specs=[
            pl.BlockSpec((tm, tn), lambda i, j: (i, j)),
            pl.BlockSpec((tm, LANES), lambda i, j: (i, 0)),
        ],
        out_shape=[
            jax.ShapeDtypeStruct((tp, n), jnp.float32),
            jax.ShapeDtypeStruct((tp, LANES), jnp.float32),
        ],
        scratch_shapes=[pltpu.VMEM((tm, d), MXU_DTYPE)],
        compiler_params=_params(("parallel", "arbitrary")),
        name="norm_inproj",
    )(hp, g, w, wdt)


def _ssd_kernel(n_pad, z_ref, xs_ref, b_ref, c_ref, dt_ref,
                cwx_ref, cwb_ref, cwc_ref, cbx_ref, cbb_ref, cbc_ref,
                dtb_ref, a_ref, d_ref, nw_ref, o_ref,
                xs_s, b_s, c_s, dt_s, adt_s, state_s):
    g = pl.program_id(1)
    lp = xs_ref.shape[1]
    q = SSD_CHUNK
    gw = xs_ref.shape[2]
    hpg = gw // SSD_HEAD_DIM
    f32 = jnp.float32

    xs_s[...] = _silu(_causal_conv(xs_ref[0], cwx_ref, cbx_ref))
    b_s[...] = _silu(_causal_conv(b_ref[0], cwb_ref, cbb_ref))
    c_s[...] = _silu(_causal_conv(c_ref[0], cwc_ref, cbc_ref))

    dt = _softplus(dt_ref[0] + dtb_ref[...])
    rows = lax.broadcasted_iota(jnp.int32, (LANES, gw), 0)
    cols = lax.broadcasted_iota(jnp.int32, (LANES, gw), 1)
    expand = (rows == g * hpg + lax.shift_right_logical(cols, SSD_HEAD_SHIFT)).astype(f32)
    dt_g = _dot_exact_rhs(dt, expand)
    dt_s[...] = dt_g
    adt_s[...] = dt_g * a_ref[...]

    state_s[...] = jnp.zeros_like(state_s)
    li = lax.broadcasted_iota(jnp.int32, (q, q), 0)
    si = lax.broadcasted_iota(jnp.int32, (q, q), 1)
    causal = li >= si
    tri = causal.astype(f32)
    lane = lax.broadcasted_iota(jnp.int32, (q, gw), 1)
    row = lax.broadcasted_iota(jnp.int32, (q, gw), 0)

    def chunk(c, carry):
        r0 = pl.multiple_of(c * q, q)
        sl = pl.ds(r0, q)
        xs = xs_s[sl, :]
        bm = b_s[sl, :]
        cm = c_s[sl, :]
        xdt = jnp.where(row + r0 >= n_pad, xs * dt_s[sl, :], 0.0)
        acum = _dot_exact_lhs(tri, adt_s[sl, :])
        acum_t = acum.T
        a_last = acum[q - 1:q, :]
        state = state_s[...]

        cb = lax.dot_general(cm.astype(MXU_DTYPE), bm.astype(MXU_DTYPE),
                             (((1,), (1,)), ((), ())), preferred_element_type=f32)
        y = _dot(cm, state) * jnp.exp(acum)
        for j in range(hpg):
            c0 = j * SSD_HEAD_DIM
            seg = acum[:, c0:c0 + 1] - acum_t[c0:c0 + 1, :]
            decay = jnp.exp(jnp.where(causal, seg, -jnp.inf))
            in_head = (lane >= c0) & (lane < c0 + SSD_HEAD_DIM)
            y = y + _dot(cb * decay, jnp.where(in_head, xdt, 0.0))
        y = y + d_ref[...] * xs

        state_s[...] = state * jnp.exp(a_last) + _dot(bm.T, xdt * jnp.exp(a_last - acum))

        yz = y * _silu(z_ref[0, sl, :])
        yn = yz * lax.rsqrt(jnp.mean(yz * yz, axis=-1, keepdims=True) + EPS)
        o_ref[0, sl, :] = (yn * nw_ref[...]).astype(o_ref.dtype)
        return carry

    lax.fori_loop(0, lp // q, chunk, 0)


def _ssd(proj3, dt3, n_pad, cw, cb, dt_bias, a_log, d_skip, norm_w, d_inner):
    b, lp, _ = proj3.shape
    gw = d_inner // SSD_GROUPS
    hpg = gw // SSD_HEAD_DIM
    ns = SSD_STATE
    xs_blk0 = d_inner // gw
    b_blk0 = 2 * d_inner // ns
    c_blk0 = (2 * d_inner + SSD_GROUPS * ns) // ns
    heads = d_inner // SSD_HEAD_DIM
    f32 = jnp.float32

    cwx = cw[:, :d_inner]
    cwb = cw[:, d_inner:d_inner + SSD_GROUPS * ns]
    cwc = cw[:, d_inner + SSD_GROUPS * ns:]
    cbx = cb[None, :d_inner]
    cbb = cb[None, d_inner:d_inner + SSD_GROUPS * ns]
    cbc = cb[None, d_inner + SSD_GROUPS * ns:]
    dtb = jnp.zeros((1, LANES), f32).at[0, :heads].set(dt_bias)
    a_ch = jnp.repeat(-jnp.exp(a_log.astype(f32)), SSD_HEAD_DIM)[None, :]
    d_ch = jnp.repeat(d_skip, SSD_HEAD_DIM)[None, :]
    nw = norm_w[None, :]

    seq = lambda w, off: pl.BlockSpec((1, lp, w), lambda i, g: (i, 0, off + g))
    vec = lambda w: pl.BlockSpec((1, w), lambda i, g: (0, g))
    cwspec = lambda w: pl.BlockSpec((CONV_K, w), lambda i, g: (0, g))
    return pl.pallas_call(
        functools.partial(_ssd_kernel, n_pad),
        grid=(b, SSD_GROUPS),
        in_specs=[
            seq(gw, 0), seq(gw, xs_blk0), seq(ns, b_blk0), seq(ns, c_blk0),
            pl.BlockSpec((1, lp, LANES), lambda i, g: (i, 0, 0)),
            cwspec(gw), cwspec(ns), cwspec(ns), vec(gw), vec(ns), vec(ns),
            pl.BlockSpec((1, LANES), lambda i, g: (0, 0)),
            vec(gw), vec(gw), vec(gw),
        ],
        out_specs=pl.BlockSpec((1, lp, gw), lambda i, g: (i, 0, g)),
        out_shape=jax.ShapeDtypeStruct((b, lp, d_inner), MXU_DTYPE),
        scratch_shapes=[
            pltpu.VMEM((lp, gw), f32), pltpu.VMEM((lp, ns), f32), pltpu.VMEM((lp, ns), f32),
            pltpu.VMEM((lp, gw), f32), pltpu.VMEM((lp, gw), f32), pltpu.VMEM((ns, gw), f32),
        ],
        compiler_params=_params(("parallel", "parallel")),
        name="ssd",
    )(proj3, proj3, proj3, proj3, dt3, cwx, cwb, cwc, cbx, cbb, cbc, dtb, a_ch, d_ch, nw)


def _lru_kernel(n_pad, lx_ref, ly_ref, cw_ref, cb_ref, wa_ref, ba_ref, wx_ref, bx_ref, lam_ref, o_ref,
                a_s, u_s, h_s):
    lp = lx_ref.shape[1]
    w = lx_ref.shape[2]
    xr = _causal_conv(lx_ref[0], cw_ref, cb_ref)
    gate_r = _sigmoid(_dot(xr, wa_ref[0]) + ba_ref[...])
    gate_i = _sigmoid(_dot(xr, wx_ref[0]) + bx_ref[...])
    log_a = (-LRU_C) * gate_r * _softplus(-lam_ref[...])
    a = jnp.exp(log_a)
    a_s[...] = a
    mult = jnp.sqrt(jnp.tanh(-log_a) * (a * a + 1.0))
    rows = lax.broadcasted_iota(jnp.int32, (lp, w), 0)
    u_s[...] = jnp.where(rows >= n_pad, mult * gate_i * xr, 0.0)

    sub = lax.broadcasted_iota(jnp.int32, (SUBLANES, w), 0)

    def tile(t, h_prev):
        sl = pl.ds(pl.multiple_of(t * SUBLANES, SUBLANES), SUBLANES)
        a = a_s[sl, :]
        u = u_s[sl, :]
        for d in (1, 2, 4):
            keep = sub >= d
            u = jnp.where(keep, a * pltpu.roll(u, d, 0) + u, u)
            a = jnp.where(keep, a * pltpu.roll(a, d, 0), a)
        h = a * h_prev + u
        h_s[sl, :] = h
        return jnp.broadcast_to(h[SUBLANES - 1:SUBLANES, :], (SUBLANES, w))

    lax.fori_loop(0, lp // SUBLANES, tile, jnp.zeros((SUBLANES, w), jnp.float32))
    o_ref[0] = (h_s[...] * jax.nn.gelu(ly_ref[0])).astype(o_ref.dtype)


def _lru(proj3, n_pad, lx_col, ly_col, cw, cb, wa, ba, wx, bx, lam, width):
    b, lp, _ = proj3.shape
    w = width // LRU_HEADS
    seq = lambda off: pl.BlockSpec((1, lp, w), lambda i, h: (i, 0, off + h))
    vec = pl.BlockSpec((1, w), lambda i, h: (0, h))
    mat = pl.BlockSpec((1, w, w), lambda i, h: (h, 0, 0))
    return pl.pallas_call(
        functools.partial(_lru_kernel, n_pad),
        grid=(b, LRU_HEADS),
        in_specs=[seq(lx_col // w), seq(ly_col // w),
                  pl.BlockSpec((CONV_K, w), lambda i, h: (0, h)), vec, mat, vec, mat, vec, vec],
        out_specs=pl.BlockSpec((1, lp, w), lambda i, h: (i, 0, h)),
        out_shape=jax.ShapeDtypeStruct((b, lp, width), MXU_DTYPE),
        scratch_shapes=[pltpu.VMEM((lp, w), jnp.float32)] * 3,
        compiler_params=_params(("parallel", "parallel")),
        name="rglru",
    )(proj3, proj3, cw, cb[None, :], wa, ba[None, :], wx, bx[None, :], lam[None, :])


def _merge_kernel(ys_ref, yl_ref, ws_ref, wl_ref, g0_ref, g1_ref, gb_ref, o_ref):
    y_ssd = jnp.dot(ys_ref[...], ws_ref[...], preferred_element_type=jnp.float32)
    y_lru = jnp.dot(yl_ref[...], wl_ref[...], preferred_element_type=jnp.float32)
    gate0 = _sigmoid(g0_ref[...] + gb_ref[0:1, :])
    gate1 = _sigmoid(g1_ref[...] + gb_ref[1:2, :])
    o_ref[...] = (gate0 * y_ssd + gate1 * y_lru).astype(o_ref.dtype)


def _merge(ys, yl, ws, wl, proj, gate_col, gate_bias):
    tp, d = ys.shape
    tm = _pick(tp, (1024, 512, 256, 128))
    tn = _pick(d, (512, 256, 128))
    g0 = gate_col // tn
    g1 = (gate_col + d) // tn
    return pl.pallas_call(
        _merge_kernel,
        grid=(tp // tm, d // tn),
        in_specs=[
            pl.BlockSpec((tm, d), lambda i, j: (i, 0)),
            pl.BlockSpec((tm, d), lambda i, j: (i, 0)),
            pl.BlockSpec((d, tn), lambda i, j: (0, j)),
            pl.BlockSpec((d, tn), lambda i, j: (0, j)),
            pl.BlockSpec((tm, tn), lambda i, j: (i, g0 + j)),
            pl.BlockSpec((tm, tn), lambda i, j: (i, g1 + j)),
            pl.BlockSpec((N_BRANCH, tn), lambda i, j: (0, j)),
        ],
        out_specs=pl.BlockSpec((tm, tn), lambda i, j: (i, j)),
        out_shape=jax.ShapeDtypeStruct((tp, d), MXU_DTYPE),
        compiler_params=_params(("parallel", "arbitrary")),
        name="merge",
    )(ys, yl, ws, wl, proj, proj, gate_bias)


def _first_index_of_max(p, valid, lane):
    pm = jnp.where(valid, p, -jnp.inf)
    top = jnp.max(pm, axis=-1, keepdims=True)
    idx = jnp.min(jnp.where(valid & (pm == top), lane, LANES), axis=-1, keepdims=True)
    return top, idx


def _masked_softmax(x, valid):
    m = jnp.max(jnp.where(valid, x, -jnp.inf), axis=-1, keepdims=True)
    e = jnp.where(valid, jnp.exp(x - m), 0.0)
    return e / jnp.sum(e, axis=-1, keepdims=True)


def _outproj_router_kernel(mix_ref, h_ref, wo_ref, g_ref, wr_ref, h2_ref, u_ref, eid_ref, gate_ref):
    h2 = h_ref[...] + jnp.dot(mix_ref[...], wo_ref[...], preferred_element_type=jnp.float32)
    h2_ref[...] = h2
    u = h2 * lax.rsqrt(jnp.mean(h2 * h2, axis=-1, keepdims=True) + EPS) * g_ref[...]
    u_ref[...] = u
    logits = _dot(u, wr_ref[...])
    lane = lax.broadcasted_iota(jnp.int32, logits.shape, 1)

    g_prob = _masked_softmax(logits, lane < MOE_GROUPS)
    g_p, g_idx = _first_index_of_max(g_prob, lane < MOE_GROUPS, lane)

    e_lo = MOE_GROUPS + g_idx * MOE_EXP_PER_GROUP
    in_group = (lane >= e_lo) & (lane < e_lo + MOE_EXP_PER_GROUP)
    e_prob = _masked_softmax(logits, in_group)
    p1, i1 = _first_index_of_max(e_prob, in_group, lane)
    rest = in_group & (lane != i1)
    p2, i2 = _first_index_of_max(e_prob, rest, lane)
    denom = p1 + p2

    col = lax.broadcasted_iota(jnp.int32, eid_ref.shape, 1)
    eid_ref[...] = jnp.where(col == 0, i1, i2) - MOE_GROUPS
    gate_ref[...] = jnp.where(col == 0, g_p * p1 / denom, g_p * p2 / denom)


def _outproj_router(mixed, hp, wo, g_ffn, wr):
    tp, d = hp.shape
    tm = _pick(tp, (256, 128))
    row = lambda w: pl.BlockSpec((tm, w), lambda i: (i, 0))
    return pl.pallas_call(
        _outproj_router_kernel,
        grid=(tp // tm,),
        in_specs=[row(d), row(d),
                  pl.BlockSpec((d, d), lambda i: (0, 0)),
                  pl.BlockSpec((1, d), lambda i: (0, 0)),
                  pl.BlockSpec((d, LANES), lambda i: (0, 0))],
        out_specs=[row(d), row(d), row(MOE_TOPK), row(MOE_TOPK)],
        out_shape=[jax.ShapeDtypeStruct((tp, d), jnp.float32),
                   jax.ShapeDtypeStruct((tp, d), jnp.float32),
                   jax.ShapeDtypeStruct((tp, MOE_TOPK), jnp.int32),
                   jax.ShapeDtypeStruct((tp, MOE_TOPK), jnp.float32)],
        compiler_params=_params(("parallel",)),
        name="outproj_router",
    )(mixed, hp, wo, g_ffn, wr)


def _wait_rows(n, make_copy):
    for k in range(EXPERT_ROWS.bit_length() - 1, -1, -1):
        @pl.when((n & (1 << k)) != 0)
        def _():
            make_copy(1 << k).wait()


def _expert_kernel(n_items_ref, n_used_ref, item_e_ref, item_start_ref, item_rows_ref, item_real_ref, tok_ref,
                   u_hbm, w1_ref, w3_ref, w2_ref, y_hbm,
                   xg_s, xb_s, acc_s, sem):
    i = pl.program_id(0)
    f = pl.program_id(1)
    nf = pl.num_programs(1)
    n_items = n_items_ref[0]
    active = i < n_items
    n_rows = item_rows_ref[i]
    d = acc_s.shape[1]
    blk = MOE_BLOCK

    def issue_gather(item):
        start = item_start_ref[item]
        n = item_real_ref[item]
        n_groups = n // GATHER_UNROLL

        def one(r):
            pltpu.make_async_copy(u_hbm.at[pl.ds(tok_ref[start + r], 1)], xg_s.at[pl.ds(r, 1)], sem.at[0]).start()

        def group(g, c):
            for j in range(GATHER_UNROLL):
                one(g * GATHER_UNROLL + j)
            return c
        lax.fori_loop(0, n_groups, group, 0)

        def single(r, c):
            one(r)
            return c
        lax.fori_loop(n_groups * GATHER_UNROLL, n, single, 0)

    def wait_gather(item):
        _wait_rows(item_real_ref[item],
                   lambda m: pltpu.make_async_copy(u_hbm.at[pl.ds(0, m)], xg_s.at[pl.ds(0, m)], sem.at[0]))

    def out_copy(item, j):
        r0 = pl.multiple_of(j * blk, blk)
        dst0 = pl.multiple_of(item_start_ref[item] + r0, blk)
        return pltpu.make_async_copy(acc_s.at[pl.ds(r0, blk)], y_hbm.at[pl.ds(dst0, blk)], sem.at[1])

    def issue_out(item):
        def body(j, c):
            out_copy(item, j).start()
            return c
        lax.fori_loop(0, item_rows_ref[item] // blk, body, 0)

    def wait_out(item):
        def body(j, c):
            out_copy(item, j).wait()
            return c
        lax.fori_loop(0, item_rows_ref[item] // blk, body, 0)

    @pl.when((i == 0) & (f == 0))
    def _():
        xg_s[...] = jnp.zeros_like(xg_s)
        acc_s[pl.ds(0, blk), :] = jnp.zeros((blk, d), acc_s.dtype)
        n_used = n_used_ref[0]
        n_blocks = y_hbm.shape[0] // blk

        def tail_copy(j):
            return pltpu.make_async_copy(acc_s.at[pl.ds(0, blk)], y_hbm.at[pl.ds(pl.multiple_of(j * blk, blk), blk)],
                                         sem.at[1])

        def fill(j, c):
            tail_copy(j).start()
            return c
        lax.fori_loop(n_used, n_blocks, fill, 0)

        def drain(j, c):
            tail_copy(j).wait()
            return c
        lax.fori_loop(n_used, n_blocks, drain, 0)

        @pl.when(active)
        def _():
            issue_gather(0)

    @pl.when(active & (f == 0))
    def _():
        wait_gather(i)
        for m in range(blk, EXPERT_ROWS + 1, blk):
            @pl.when(n_rows == m)
            def _():
                xb_s[pl.ds(0, m), :] = xg_s[pl.ds(0, m), :].astype(xb_s.dtype)

        @pl.when(i + 1 < n_items)
        def _():
            issue_gather(i + 1)

    @pl.when((f == 0) & (i >= 1) & (i - 1 < n_items) & jnp.logical_not(active))
    def _():
        wait_out(i - 1)

    for m in range(blk, EXPERT_ROWS + 1, blk):
        @pl.when(active & (n_rows == m))
        def _():
            rows = pl.ds(0, m)
            x = xb_s[rows, :]
            a = jnp.dot(x, w1_ref[0].astype(MXU_DTYPE), preferred_element_type=jnp.float32)
            b = jnp.dot(x, w3_ref[0].astype(MXU_DTYPE), preferred_element_type=jnp.float32)
            hdn = (_silu(a) * b).astype(MXU_DTYPE)

            @pl.when((f == 0) & (i >= 1))
            def _():
                wait_out(i - 1)

            for c0 in range(0, d, EXPERT_OUT_TILE):
                cols = pl.ds(c0, EXPERT_OUT_TILE)
                part = jnp.dot(hdn, w2_ref[0, :, cols].astype(MXU_DTYPE), preferred_element_type=jnp.float32)

                @pl.when(f == 0)
                def _():
                    acc_s[rows, cols] = part

                @pl.when(f > 0)
                def _():
                    acc_s[rows, cols] += part

    @pl.when(f == nf - 1)
    def _():
        @pl.when(active)
        def _():
            issue_out(i)

        @pl.when(active & (i == pl.num_programs(0) - 1))
        def _():
            wait_out(i)


def _experts(u, sched, w1, w3, w2, n_items_max, p_len):
    tp, d = u.shape
    ff = w1.shape[2]
    tf = EXPERT_FF_TILE
    nf = ff // tf
    n_items, n_used, item_e, item_start, item_rows, item_real, tok_sorted = sched

    def w_in_map(i, f, n_items, n_used, item_e, *_):
        return (item_e[i], 0, jnp.where(i < n_items[0], f, nf - 1))

    def w_out_map(i, f, n_items, n_used, item_e, *_):
        return (item_e[i], jnp.where(i < n_items[0], f, nf - 1), 0)

    grid_spec = pltpu.PrefetchScalarGridSpec(
        num_scalar_prefetch=7,
        grid=(n_items_max, nf),
        in_specs=[
            pl.BlockSpec(memory_space=pl.ANY),
            pl.BlockSpec((1, d, tf), w_in_map),
            pl.BlockSpec((1, d, tf), w_in_map),
            pl.BlockSpec((1, tf, d), w_out_map),
        ],
        out_specs=pl.BlockSpec(memory_space=pl.ANY),
        scratch_shapes=[
            pltpu.VMEM((EXPERT_ROWS, d), jnp.float32),
            pltpu.VMEM((EXPERT_ROWS, d), MXU_DTYPE),
            pltpu.VMEM((EXPERT_ROWS, d), jnp.float32),
            pltpu.SemaphoreType.DMA((2,)),
        ],
    )
    return pl.pallas_call(
        _expert_kernel,
        grid_spec=grid_spec,
        out_shape=jax.ShapeDtypeStruct((p_len, d), jnp.float32),
        compiler_params=_params(("arbitrary", "arbitrary")),
        name="experts",
    )(n_items, n_used, item_e, item_start, item_rows, item_real, tok_sorted, u, w1, w3, w2)


def _expert_schedule(eid, tok_rows, n_items_max, p_len):
    i32 = jnp.int32
    flat_e = eid[tok_rows].reshape(-1)
    flat_tok = jnp.repeat(jnp.asarray(tok_rows, i32), MOE_TOPK)
    onehot = (flat_e[:, None] == jnp.arange(MOE_EXPERTS, dtype=i32)[None, :]).astype(i32)
    csum = jnp.cumsum(onehot, axis=0)
    counts = csum[-1]
    rank = jnp.take_along_axis(csum, flat_e[:, None], axis=1)[:, 0] - 1
    padded = (counts + MOE_BLOCK - 1) // MOE_BLOCK * MOE_BLOCK
    pad_end = jnp.cumsum(padded)
    pad_start = pad_end - padded
    dest = pad_start[flat_e] + rank
    tok_sorted = jnp.zeros((p_len,), i32).at[dest].set(flat_tok)
    n_used = (pad_end[-1:] // MOE_BLOCK).astype(i32)

    chunks = (padded + EXPERT_ROWS - 1) // EXPERT_ROWS
    chunk_end = jnp.cumsum(chunks)
    item = jnp.arange(n_items_max, dtype=i32)
    item_e = jnp.minimum(jnp.sum((chunk_end[None, :] <= item[:, None]).astype(i32), axis=1), MOE_EXPERTS - 1)
    n_items = chunk_end[-1:].astype(i32)
    last_e = item_e[jnp.maximum(n_items[0] - 1, 0)]
    item_e = jnp.where(item < n_items[0], item_e, last_e)
    k = item - (chunk_end - chunks)[item_e]
    item_start = (pad_start[item_e] + k * EXPERT_ROWS).astype(i32)
    item_rows = jnp.clip(padded[item_e] - k * EXPERT_ROWS, 0, EXPERT_ROWS).astype(i32)
    item_real = jnp.clip(counts[item_e] - k * EXPERT_ROWS, 0, EXPERT_ROWS).astype(i32)
    live = item < n_items[0]
    item_start = jnp.where(live, item_start, 0)
    item_rows = jnp.where(live, item_rows, 0)
    item_real = jnp.where(live, item_real, 0)
    return (n_items, n_used, item_e, item_start, item_rows, item_real, tok_sorted), dest.reshape(-1, MOE_TOPK)


def _final_kernel(tm, dest_ref, h_ref, gate_ref, g_ref, ys_hbm, o_ref, ybuf, sem):
    t = pl.program_id(0)
    n_tiles = pl.num_programs(0)

    def issue(tile, buf):
        base = tile * (tm * MOE_TOPK)
        for j in range(tm):
            for k in range(MOE_TOPK):
                pos = dest_ref[base + j * MOE_TOPK + k]
                pltpu.make_async_copy(ys_hbm.at[pl.ds(pos, 1)], ybuf.at[buf, k, pl.ds(j, 1)], sem.at[buf]).start()

    @pl.when(t == 0)
    def _():
        issue(0, 0)

    @pl.when(t + 1 < n_tiles)
    def _():
        issue(t + 1, (t + 1) % 2)

    buf = t % 2
    for k in range(MOE_TOPK):
        pltpu.make_async_copy(ys_hbm.at[pl.ds(0, tm)], ybuf.at[buf, k], sem.at[buf]).wait()
    moe = gate_ref[:, 0:1] * ybuf[buf, 0] + gate_ref[:, 1:2] * ybuf[buf, 1]
    h = h_ref[...] + moe
    o_ref[...] = h * lax.rsqrt(jnp.mean(h * h, axis=-1, keepdims=True) + EPS) * g_ref[...]


def _final(h2, ys, dest_x, gates, g_final, batch, lp, seq):
    tp, d = h2.shape
    tm = SSD_CHUNK
    skip = (lp - seq) // tm
    per_b = seq // tm
    src = lambda w: pl.BlockSpec((tm, w), lambda t, dest: ((t // per_b) * (lp // tm) + skip + t % per_b, 0))
    grid_spec = pltpu.PrefetchScalarGridSpec(
        num_scalar_prefetch=1,
        grid=(batch * per_b,),
        in_specs=[src(d), src(MOE_TOPK), pl.BlockSpec((1, d), lambda t, dest: (0, 0)),
                  pl.BlockSpec(memory_space=pl.ANY)],
        out_specs=pl.BlockSpec((tm, d), lambda t, dest: (t, 0)),
        scratch_shapes=[pltpu.VMEM((2, MOE_TOPK, tm, d), jnp.float32), pltpu.SemaphoreType.DMA((2,))],
    )
    return pl.pallas_call(
        functools.partial(_final_kernel, tm),
        grid_spec=grid_spec,
        out_shape=jax.ShapeDtypeStruct((batch * seq, d), jnp.float32),
        compiler_params=_params(("arbitrary",)),
        name="final_norm",
    )(dest_x, h2, gates, g_final, ys)


def kernel(x, meta_tokens, norm_mix, w_in, ssd_conv_w, ssd_conv_b, ssd_dt_bias, ssd_a_log, ssd_d, ssd_norm, w_ssd_out, lru_conv_w, lru_conv_b, lru_wa, lru_ba, lru_wx, lru_bx, lru_lambda, w_lru_out, gate_bias, w_out, norm_ffn, w_router_group, w_router_expert, w_exp_gate, w_exp_up, w_exp_down, norm_final):
    batch, seq, d = x.shape
    depth = norm_mix.shape[0]
    assert depth == 1 and seq % SSD_CHUNK == 0
    l = N_META + seq
    lp = -(-l // SSD_CHUNK) * SSD_CHUNK
    n_pad = lp - l
    d_inner = d
    heads = d_inner // SSD_HEAD_DIM
    conv_dim = d_inner + 2 * SSD_GROUPS * SSD_STATE
    col_xbc = d_inner
    col_dt = col_xbc + conv_dim
    col_lx = col_dt + heads
    col_ly = col_lx + d
    col_gl = col_ly + d
    f32 = jnp.float32
    lyr = 0

    meta = jnp.broadcast_to(meta_tokens.astype(f32)[None], (batch, N_META, d))
    hp = jnp.concatenate([jnp.zeros((batch, n_pad, d), f32), meta, x], axis=1).reshape(batch * lp, d)

    w = w_in[lyr]
    w_main = jnp.concatenate([w[:, :col_dt], w[:, col_lx:]], axis=1).astype(MXU_DTYPE)
    w_dt = jnp.zeros((d, LANES), f32).at[:, :heads].set(w[:, col_dt:col_lx]).astype(MXU_DTYPE)
    proj, dt_raw = _norm_inproj(hp, norm_mix[lyr][None, :], w_main, w_dt)
    n_main = w_main.shape[1]
    proj3 = proj.reshape(batch, lp, n_main)
    lx_col = col_dt
    ly_col = lx_col + d
    gl_col = ly_col + d

    ys = _ssd(proj3, dt_raw.reshape(batch, lp, LANES), n_pad, ssd_conv_w[lyr], ssd_conv_b[lyr], ssd_dt_bias[lyr],
              ssd_a_log[lyr], ssd_d[lyr], ssd_norm[lyr], d_inner)
    yl = _lru(proj3, n_pad, lx_col, ly_col, lru_conv_w[lyr], lru_conv_b[lyr], lru_wa[lyr].astype(MXU_DTYPE),
              lru_ba[lyr], lru_wx[lyr].astype(MXU_DTYPE), lru_bx[lyr], lru_lambda[lyr], d)

    mixed = _merge(ys.reshape(batch * lp, d), yl.reshape(batch * lp, d), w_ssd_out[lyr].astype(MXU_DTYPE),
                   w_lru_out[lyr].astype(MXU_DTYPE), proj, gl_col, gate_bias[lyr])

    w_router = jnp.zeros((d, LANES), f32)
    w_router = w_router.at[:, :MOE_GROUPS].set(w_router_group[lyr])
    w_router = w_router.at[:, MOE_GROUPS:MOE_GROUPS + MOE_EXPERTS].set(w_router_expert[lyr])
    h2, u2, eid, gates = _outproj_router(mixed, hp, w_out[lyr].astype(MXU_DTYPE), norm_ffn[lyr][None, :],
                                         w_router.astype(MXU_DTYPE))

    tok_rows = np.nonzero((np.arange(batch * lp) % lp) >= n_pad)[0].astype(np.int32)
    n_asg = tok_rows.shape[0] * MOE_TOPK
    p_max = n_asg + MOE_EXPERTS * (MOE_BLOCK - 1)
    n_items_max = MOE_EXPERTS + p_max // EXPERT_ROWS
    p_len = -(-p_max // MOE_BLOCK) * MOE_BLOCK
    sched, dest = _expert_schedule(eid, tok_rows, n_items_max, p_len)
    ff = w_exp_gate.shape[-1]
    ys_sorted = _experts(u2, sched, w_exp_gate.reshape(MOE_EXPERTS, d, ff), w_exp_up.reshape(MOE_EXPERTS, d, ff),
                         w_exp_down.reshape(MOE_EXPERTS, ff, d), n_items_max, p_len)

    dest_x = dest.reshape(batch, l, MOE_TOPK)[:, N_META:].reshape(-1)
    out = _final(h2, ys_sorted, dest_x, gates, norm_final[None, :], batch, lp, seq)
    return out.reshape(batch, seq, d)
```

```python
import functools

import numpy as np
import jax
import jax.numpy as jnp
from jax import lax
from jax.experimental import pallas as pl
from jax.experimental.pallas import tpu as pltpu

N_META = 16
CONV_K = 4
EPS = 1e-6
SSD_HEAD_DIM = 64
SSD_HEAD_SHIFT = 6
SSD_GROUPS = 8
SSD_STATE = 128
SSD_CHUNK = 128
LRU_HEADS = 8
LRU_C = 8.0
N_BRANCH = 2
MOE_GROUPS = 8
MOE_EXP_PER_GROUP = 8
MOE_EXPERTS = MOE_GROUPS * MOE_EXP_PER_GROUP
MOE_TOPK = 2
MOE_BLOCK = 128

LANES = 128
SUBLANES = 8
VMEM_LIMIT = 56 * 1024 * 1024
MXU_DTYPE = jnp.bfloat16

EXPERT_ROWS = 1024
EXPERT_FF_TILE = 512
EXPERT_OUT_TILE = 1024
GATHER_UNROLL = 8


def _pick(n, options):
    for o in options:
        if n % o == 0:
            return o
    raise ValueError(f"no tile in {options} divides {n}")


def _params(sem, vmem=VMEM_LIMIT):
    return pltpu.CompilerParams(dimension_semantics=sem, vmem_limit_bytes=vmem)


def _dot(a, b):
    return jnp.dot(a.astype(MXU_DTYPE), b.astype(MXU_DTYPE), preferred_element_type=jnp.float32)


def _dot_exact_rhs(v, sel):
    sel = sel.astype(jnp.bfloat16)
    hi = v.astype(jnp.bfloat16)
    r1 = v - hi.astype(jnp.float32)
    mid = r1.astype(jnp.bfloat16)
    lo = (r1 - mid.astype(jnp.float32)).astype(jnp.bfloat16)
    f32 = jnp.float32
    return (jnp.dot(hi, sel, preferred_element_type=f32) + jnp.dot(mid, sel, preferred_element_type=f32)
            + jnp.dot(lo, sel, preferred_element_type=f32))


def _dot_exact_lhs(sel, v):
    sel = sel.astype(jnp.bfloat16)
    hi = v.astype(jnp.bfloat16)
    r1 = v - hi.astype(jnp.float32)
    mid = r1.astype(jnp.bfloat16)
    lo = (r1 - mid.astype(jnp.float32)).astype(jnp.bfloat16)
    f32 = jnp.float32
    return (jnp.dot(sel, hi, preferred_element_type=f32) + jnp.dot(sel, mid, preferred_element_type=f32)
            + jnp.dot(sel, lo, preferred_element_type=f32))


def _softplus(x):
    return jnp.maximum(x, 0.0) + jnp.log1p(jnp.exp(-jnp.abs(x)))


def _sigmoid(x):
    return 1.0 / (1.0 + jnp.exp(-x))


def _silu(x):
    return x * _sigmoid(x)


def _causal_conv_chunk(x_ref, w_ref, b_ref, c):
    q = SSD_CHUNK
    r0 = pl.multiple_of(c * q, q)
    cur = x_ref[0, pl.ds(r0, q), :]
    prev = x_ref[0, pl.ds(pl.multiple_of(jnp.maximum(r0 - SUBLANES, 0), SUBLANES), SUBLANES), :]
    prev = jnp.where(c > 0, prev, 0.0)
    sub = lax.broadcasted_iota(jnp.int32, prev.shape, 0)
    acc = b_ref[...] + w_ref[CONV_K - 1:CONV_K, :] * cur
    for k in range(1, CONV_K):
        rolled = pltpu.roll(cur, k, 0)
        head = jnp.where(sub < k, pltpu.roll(prev, k, 0), rolled[:SUBLANES])
        shifted = jnp.concatenate([head, rolled[SUBLANES:]], axis=0)
        acc = acc + w_ref[CONV_K - 1 - k:CONV_K - k, :] * shifted
    return acc


def _norm_inproj_kernel(x_ref, g_ref, w_ref, wdt_ref, dtb_ref, alog_ref, o_ref, dt_ref, acum_ref, xn_ref):
    @pl.when(pl.program_id(1) == 0)
    def _():
        x = x_ref[...]
        y = x * lax.rsqrt(jnp.mean(x * x, axis=-1, keepdims=True) + EPS)
        xn_ref[...] = (y * g_ref[...]).astype(xn_ref.dtype)
        raw = jnp.dot(xn_ref[...], wdt_ref[...], preferred_element_type=jnp.float32)
        dt = _softplus(raw + dtb_ref[...])
        dt_ref[...] = dt
        adt = dt * (-jnp.exp(alog_ref[...]))
        q = SSD_CHUNK
        tri = (lax.broadcasted_iota(jnp.int32, (q, q), 0) >= lax.broadcasted_iota(jnp.int32, (q, q), 1))
        for c in range(x.shape[0] // q):
            acum_ref[c * q:(c + 1) * q, :] = _dot_exact_lhs(tri.astype(jnp.float32), adt[c * q:(c + 1) * q, :])

    o_ref[...] = jnp.dot(xn_ref[...], w_ref[...], preferred_element_type=jnp.float32)


def _norm_inproj(hp, g, w, wdt, dtb, alog):
    tp, d = hp.shape
    n = w.shape[1]
    tm = _pick(tp, (1024, 512, 256, 128))
    tn = _pick(n, (512, 256, 128))
    vec = pl.BlockSpec((1, LANES), lambda i, j: (0, 0))
    return pl.pallas_call(
        _norm_inproj_kernel,
        grid=(tp // tm, n // tn),
        in_specs=[
            pl.BlockSpec((tm, d), lambda i, j: (i, 0)),
            pl.BlockSpec((1, d), lambda i, j: (0, 0)),
            pl.BlockSpec((d, tn), lambda i, j: (0, j)),
            pl.BlockSpec((d, LANES), lambda i, j: (0, 0)),
            vec, vec,
        ],
        out_specs=[
            pl.BlockSpec((tm, tn), lambda i, j: (i, j)),
            pl.BlockSpec((tm, LANES), lambda i, j: (i, 0)),
            pl.BlockSpec((tm, LANES), lambda i, j: (i, 0)),
        ],
        out_shape=[
            jax.ShapeDtypeStruct((tp, n), jnp.float32),
            jax.ShapeDtypeStruct((tp, LANES), jnp.float32),
            jax.ShapeDtypeStruct((tp, LANES), jnp.float32),
        ],
        scratch_shapes=[pltpu.VMEM((tm, d), MXU_DTYPE)],
        compiler_params=_params(("parallel", "arbitrary")),
        name="norm_inproj",
    )(hp, g, w, wdt, dtb, alog)


def _ssd_kernel(n_pad, z_ref, xs_ref, b_ref, c_ref, dt_ref, acum_ref,
                cwx_ref, cwb_ref, cwc_ref, cbx_ref, cbb_ref, cbc_ref,
                d_ref, nw_ref, o_ref,
                xs_s, b_s, c_s, state_s):
    g = pl.program_id(1)
    lp = xs_ref.shape[1]
    q = SSD_CHUNK
    n_chunks = lp // q
    gw = xs_ref.shape[2]
    hpg = gw // SSD_HEAD_DIM
    f32 = jnp.float32

    def conv_into(c, slot):
        xs_s[slot] = _silu(_causal_conv_chunk(xs_ref, cwx_ref, cbx_ref, c))
        b_s[slot] = _silu(_causal_conv_chunk(b_ref, cwb_ref, cbb_ref, c))
        c_s[slot] = _silu(_causal_conv_chunk(c_ref, cwc_ref, cbc_ref, c))

    rows = lax.broadcasted_iota(jnp.int32, (LANES, gw), 0)
    cols = lax.broadcasted_iota(jnp.int32, (LANES, gw), 1)
    expand = (rows == g * hpg + lax.shift_right_logical(cols, SSD_HEAD_SHIFT)).astype(f32)

    state_s[...] = jnp.zeros_like(state_s)
    li = lax.broadcasted_iota(jnp.int32, (q, q), 0)
    si = lax.broadcasted_iota(jnp.int32, (q, q), 1)
    causal = li >= si
    lane = lax.broadcasted_iota(jnp.int32, (q, gw), 1)
    row = lax.broadcasted_iota(jnp.int32, (q, gw), 0)

    conv_into(0, 0)

    def chunk(c, carry):
        r0 = pl.multiple_of(c * q, q)
        sl = pl.ds(r0, q)
        slot = c % 2
        xs = xs_s[slot]
        bm = b_s[slot]
        cm = c_s[slot]
        conv_into(jnp.minimum(c + 1, n_chunks - 1), 1 - slot)
        dt_g = _dot_exact_rhs(dt_ref[0, sl, :], expand)
        acum = _dot_exact_rhs(acum_ref[0, sl, :], expand)
        xdt = jnp.where(row + r0 >= n_pad, xs * dt_g, 0.0)
        acum_t = acum.T
        a_last = acum[q - 1:q, :]
        state = state_s[...]

        cb = lax.dot_general(cm.astype(MXU_DTYPE), bm.astype(MXU_DTYPE),
                             (((1,), (1,)), ((), ())), preferred_element_type=f32)
        y = _dot(cm, state) * jnp.exp(acum)
        for j in range(hpg):
            c0 = j * SSD_HEAD_DIM
            seg = acum[:, c0:c0 + 1] - acum_t[c0:c0 + 1, :]
            decay = jnp.exp(jnp.where(causal, seg, -jnp.inf))
            in_head = (lane >= c0) & (lane < c0 + SSD_HEAD_DIM)
            y = y + _dot(cb * decay, jnp.where(in_head, xdt, 0.0))
        y = y + d_ref[...] * xs

        state_s[...] = state * jnp.exp(a_last) + _dot(bm.T, xdt * jnp.exp(a_last - acum))

        yz = y * _silu(z_ref[0, sl, :])
        yn = yz * lax.rsqrt(jnp.mean(yz * yz, axis=-1, keepdims=True) + EPS)
        o_ref[0, sl, :] = (yn * nw_ref[...]).astype(o_ref.dtype)
        return carry

    lax.fori_loop(0, lp // q, chunk, 0)


def _ssd(proj3, dt3, acum3, n_pad, cw, cb, d_skip, norm_w, d_inner):
    b, lp, _ = proj3.shape
    gw = d_inner // SSD_GROUPS
    ns = SSD_STATE
    q = SSD_CHUNK
    xs_blk0 = d_inner // gw
    b_blk0 = 2 * d_inner // ns
    c_blk0 = (2 * d_inner + SSD_GROUPS * ns) // ns
    f32 = jnp.float32

    cwx = cw[:, :d_inner]
    cwb = cw[:, d_inner:d_inner + SSD_GROUPS * ns]
    cwc = cw[:, d_inner + SSD_GROUPS * ns:]
    cbx = cb[None, :d_inner]
    cbb = cb[None, d_inner:d_inner + SSD_GROUPS * ns]
    cbc = cb[None, d_inner + SSD_GROUPS * ns:]
    d_ch = jnp.repeat(d_skip, SSD_HEAD_DIM)[None, :]
    nw = norm_w[None, :]

    seq = lambda w, off: pl.BlockSpec((1, lp, w), lambda i, g: (i, 0, off + g))
    head = pl.BlockSpec((1, lp, LANES), lambda i, g: (i, 0, 0))
    vec = lambda w: pl.BlockSpec((1, w), lambda i, g: (0, g))
    cwspec = lambda w: pl.BlockSpec((CONV_K, w), lambda i, g: (0, g))
    return pl.pallas_call(
        functools.partial(_ssd_kernel, n_pad),
        grid=(b, SSD_GROUPS),
        in_specs=[
            seq(gw, 0), seq(gw, xs_blk0), seq(ns, b_blk0), seq(ns, c_blk0), head, head,
            cwspec(gw), cwspec(ns), cwspec(ns), vec(gw), vec(ns), vec(ns),
            vec(gw), vec(gw),
        ],
        out_specs=pl.BlockSpec((1, lp, gw), lambda i, g: (i, 0, g)),
        out_shape=jax.ShapeDtypeStruct((b, lp, d_inner), MXU_DTYPE),
        scratch_shapes=[
            pltpu.VMEM((2, q, gw), f32), pltpu.VMEM((2, q, ns), f32), pltpu.VMEM((2, q, ns), f32),
            pltpu.VMEM((ns, gw), f32),
        ],
        compiler_params=_params(("parallel", "parallel")),
        name="ssd",
    )(proj3, proj3, proj3, proj3, dt3, acum3, cwx, cwb, cwc, cbx, cbb, cbc, d_ch, nw)


def _lru_kernel(n_pad, lx_ref, ly_ref, cw_ref, cb_ref, wa_ref, ba_ref, wx_ref, bx_ref, lam_ref, o_ref):
    lp = lx_ref.shape[1]
    w = lx_ref.shape[2]
    q = SSD_CHUNK
    neg_c_softplus = (-LRU_C) * _softplus(-lam_ref[...])
    row = lax.broadcasted_iota(jnp.int32, (q, w), 0)
    sub = lax.broadcasted_iota(jnp.int32, (SUBLANES, w), 0)

    def chunk(c, h_prev):
        r0 = pl.multiple_of(c * q, q)
        sl = pl.ds(r0, q)
        xr = _causal_conv_chunk(lx_ref, cw_ref, cb_ref, c)
        gate_r = _sigmoid(_dot(xr, wa_ref[0]) + ba_ref[...])
        gate_i = _sigmoid(_dot(xr, wx_ref[0]) + bx_ref[...])
        log_a = gate_r * neg_c_softplus
        a = jnp.exp(log_a)
        mult = jnp.sqrt(jnp.tanh(-log_a) * (a * a + 1.0))
        u = jnp.where(row + r0 >= n_pad, mult * gate_i * xr, 0.0)
        gate_y = jax.nn.gelu(ly_ref[0, sl, :])

        out = []
        for t in range(q // SUBLANES):
            rows8 = slice(t * SUBLANES, (t + 1) * SUBLANES)
            at, ut = a[rows8], u[rows8]
            for d in (1, 2, 4):
                keep = sub >= d
                ut = jnp.where(keep, at * pltpu.roll(ut, d, 0) + ut, ut)
                at = jnp.where(keep, at * pltpu.roll(at, d, 0), at)
            h = at * h_prev + ut
            h_prev = jnp.broadcast_to(h[SUBLANES - 1:SUBLANES, :], (SUBLANES, w))
            out.append(h * gate_y[rows8])
        o_ref[0, sl, :] = jnp.concatenate(out, axis=0).astype(o_ref.dtype)
        return h_prev

    lax.fori_loop(0, lp // q, chunk, jnp.zeros((SUBLANES, w), jnp.float32))


def _lru(proj3, n_pad, lx_col, ly_col, cw, cb, wa, ba, wx, bx, lam, width):
    b, lp, _ = proj3.shape
    w = width // LRU_HEADS
    seq = lambda off: pl.BlockSpec((1, lp, w), lambda i, h: (i, 0, off + h))
    vec = pl.BlockSpec((1, w), lambda i, h: (0, h))
    mat = pl.BlockSpec((1, w, w), lambda i, h: (h, 0, 0))
    return pl.pallas_call(
        functools.partial(_lru_kernel, n_pad),
        grid=(b, LRU_HEADS),
        in_specs=[seq(lx_col // w), seq(ly_col // w),
                  pl.BlockSpec((CONV_K, w), lambda i, h: (0, h)), vec, mat, vec, mat, vec, vec],
        out_specs=pl.BlockSpec((1, lp, w), lambda i, h: (i, 0, h)),
        out_shape=jax.ShapeDtypeStruct((b, lp, width), MXU_DTYPE),
        compiler_params=_params(("parallel", "parallel")),
        name="rglru",
    )(proj3, proj3, cw, cb[None, :], wa, ba[None, :], wx, bx[None, :], lam[None, :])


def _merge_kernel(ys_ref, yl_ref, ws_ref, wl_ref, g0_ref, g1_ref, gb_ref, o_ref):
    y_ssd = jnp.dot(ys_ref[...], ws_ref[...], preferred_element_type=jnp.float32)
    y_lru = jnp.dot(yl_ref[...], wl_ref[...], preferred_element_type=jnp.float32)
    gate0 = _sigmoid(g0_ref[...] + gb_ref[0:1, :])
    gate1 = _sigmoid(g1_ref[...] + gb_ref[1:2, :])
    o_ref[...] = (gate0 * y_ssd + gate1 * y_lru).astype(o_ref.dtype)


def _merge(ys, yl, ws, wl, proj, gate_col, gate_bias):
    tp, d = ys.shape
    tm = _pick(tp, (1024, 512, 256, 128))
    tn = _pick(d, (512, 256, 128))
    g0 = gate_col // tn
    g1 = (gate_col + d) // tn
    return pl.pallas_call(
        _merge_kernel,
        grid=(tp // tm, d // tn),
        in_specs=[
            pl.BlockSpec((tm, d), lambda i, j: (i, 0)),
            pl.BlockSpec((tm, d), lambda i, j: (i, 0)),
            pl.BlockSpec((d, tn), lambda i, j: (0, j)),
            pl.BlockSpec((d, tn), lambda i, j: (0, j)),
            pl.BlockSpec((tm, tn), lambda i, j: (i, g0 + j)),
            pl.BlockSpec((tm, tn), lambda i, j: (i, g1 + j)),
            pl.BlockSpec((N_BRANCH, tn), lambda i, j: (0, j)),
        ],
        out_specs=pl.BlockSpec((tm, tn), lambda i, j: (i, j)),
        out_shape=jax.ShapeDtypeStruct((tp, d), MXU_DTYPE),
        compiler_params=_params(("parallel", "arbitrary")),
        name="merge",
    )(ys, yl, ws, wl, proj, proj, gate_bias)


def _first_index_of_max(p, valid, lane):
    pm = jnp.where(valid, p, -jnp.inf)
    top = jnp.max(pm, axis=-1, keepdims=True)
    idx = jnp.min(jnp.where(valid & (pm == top), lane, LANES), axis=-1, keepdims=True)
    return top, idx


def _masked_softmax(x, valid):
    m = jnp.max(jnp.where(valid, x, -jnp.inf), axis=-1, keepdims=True)
    e = jnp.where(valid, jnp.exp(x - m), 0.0)
    return e / jnp.sum(e, axis=-1, keepdims=True)


def _outproj_router_kernel(mix_ref, h_ref, wo_ref, g_ref, wr_ref, h2_ref, u_ref, eid_ref, gate_ref):
    h2 = h_ref[...] + jnp.dot(mix_ref[...], wo_ref[...], preferred_element_type=jnp.float32)
    h2_ref[...] = h2
    u = h2 * lax.rsqrt(jnp.mean(h2 * h2, axis=-1, keepdims=True) + EPS) * g_ref[...]
    u_ref[...] = u
    logits = _dot(u, wr_ref[...])
    lane = lax.broadcasted_iota(jnp.int32, logits.shape, 1)

    g_prob = _masked_softmax(logits, lane < MOE_GROUPS)
    g_p, g_idx = _first_index_of_max(g_prob, lane < MOE_GROUPS, lane)

    e_lo = MOE_GROUPS + g_idx * MOE_EXP_PER_GROUP
    in_group = (lane >= e_lo) & (lane < e_lo + MOE_EXP_PER_GROUP)
    e_prob = _masked_softmax(logits, in_group)
    p1, i1 = _first_index_of_max(e_prob, in_group, lane)
    rest = in_group & (lane != i1)
    p2, i2 = _first_index_of_max(e_prob, rest, lane)
    denom = p1 + p2

    col = lax.broadcasted_iota(jnp.int32, eid_ref.shape, 1)
    eid_ref[...] = jnp.where(col == 0, i1, i2) - MOE_GROUPS
    gate_ref[...] = jnp.where(col == 0, g_p * p1 / denom, g_p * p2 / denom)


def _outproj_router(mixed, hp, wo, g_ffn, wr):
    tp, d = hp.shape
    tm = _pick(tp, (256, 128))
    row = lambda w: pl.BlockSpec((tm, w), lambda i: (i, 0))
    return pl.pallas_call(
        _outproj_router_kernel,
        grid=(tp // tm,),
        in_specs=[row(d), row(d),
                  pl.BlockSpec((d, d), lambda i: (0, 0)),
                  pl.BlockSpec((1, d), lambda i: (0, 0)),
                  pl.BlockSpec((d, LANES), lambda i: (0, 0))],
        out_specs=[row(d), row(d), row(MOE_TOPK), row(MOE_TOPK)],
        out_shape=[jax.ShapeDtypeStruct((tp, d), jnp.float32),
                   jax.ShapeDtypeStruct((tp, d), jnp.float32),
                   jax.ShapeDtypeStruct((tp, MOE_TOPK), jnp.int32),
                   jax.ShapeDtypeStruct((tp, MOE_TOPK), jnp.float32)],
        compiler_params=_params(("parallel",)),
        name="outproj_router",
    )(mixed, hp, wo, g_ffn, wr)


def _wait_rows(n, make_copy):
    for k in range(EXPERT_ROWS.bit_length() - 1, -1, -1):
        @pl.when((n & (1 << k)) != 0)
        def _():
            make_copy(1 << k).wait()


def _expert_kernel(n_items_ref, n_used_ref, item_e_ref, item_start_ref, item_rows_ref, item_real_ref, tok_ref,
                   u_hbm, w1_ref, w3_ref, w2_ref, y_hbm,
                   xg_s, xb_s, acc_s, sem):
    i = pl.program_id(0)
    f = pl.program_id(1)
    nf = pl.num_programs(1)
    n_items = n_items_ref[0]
    active = i < n_items
    n_rows = item_rows_ref[i]
    d = acc_s.shape[1]
    blk = MOE_BLOCK

    def issue_gather(item):
        start = item_start_ref[item]
        n = item_real_ref[item]
        n_groups = n // GATHER_UNROLL

        def one(r):
            pltpu.make_async_copy(u_hbm.at[pl.ds(tok_ref[start + r], 1)], xg_s.at[pl.ds(r, 1)], sem.at[0]).start()

        def group(g, c):
            for j in range(GATHER_UNROLL):
                one(g * GATHER_UNROLL + j)
            return c
        lax.fori_loop(0, n_groups, group, 0)

        def single(r, c):
            one(r)
            return c
        lax.fori_loop(n_groups * GATHER_UNROLL, n, single, 0)

    def wait_gather(item):
        _wait_rows(item_real_ref[item],
                   lambda m: pltpu.make_async_copy(u_hbm.at[pl.ds(0, m)], xg_s.at[pl.ds(0, m)], sem.at[0]))

    def out_copy(item, j):
        r0 = pl.multiple_of(j * blk, blk)
        dst0 = pl.multiple_of(item_start_ref[item] + r0, blk)
        return pltpu.make_async_copy(acc_s.at[pl.ds(r0, blk)], y_hbm.at[pl.ds(dst0, blk)], sem.at[1])

    def issue_out(item):
        def body(j, c):
            out_copy(item, j).start()
            return c
        lax.fori_loop(0, item_rows_ref[item] // blk, body, 0)

    def wait_out(item):
        def body(j, c):
            out_copy(item, j).wait()
            return c
        lax.fori_loop(0, item_rows_ref[item] // blk, body, 0)

    @pl.when((i == 0) & (f == 0))
    def _():
        xg_s[...] = jnp.zeros_like(xg_s)
        acc_s[pl.ds(0, blk), :] = jnp.zeros((blk, d), acc_s.dtype)
        n_used = n_used_ref[0]
        n_blocks = y_hbm.shape[0] // blk

        def tail_copy(j):
            return pltpu.make_async_copy(acc_s.at[pl.ds(0, blk)], y_hbm.at[pl.ds(pl.multiple_of(j * blk, blk), blk)],
                                         sem.at[1])

        def fill(j, c):
            tail_copy(j).start()
            return c
        lax.fori_loop(n_used, n_blocks, fill, 0)

        def drain(j, c):
            tail_copy(j).wait()
            return c
        lax.fori_loop(n_used, n_blocks, drain, 0)

        @pl.when(active)
        def _():
            issue_gather(0)

    @pl.when(active & (f == 0))
    def _():
        wait_gather(i)
        for m in range(blk, EXPERT_ROWS + 1, blk):
            @pl.when(n_rows == m)
            def _():
                xb_s[pl.ds(0, m), :] = xg_s[pl.ds(0, m), :].astype(xb_s.dtype)

        @pl.when(i + 1 < n_items)
        def _():
            issue_gather(i + 1)

    @pl.when((f == 0) & (i >= 1) & (i - 1 < n_items) & jnp.logical_not(active))
    def _():
        wait_out(i - 1)

    for m in range(blk, EXPERT_ROWS + 1, blk):
        @pl.when(active & (n_rows == m))
        def _():
            rows = pl.ds(0, m)
            x = xb_s[rows, :]
            a = jnp.dot(x, w1_ref[0].astype(MXU_DTYPE), preferred_element_type=jnp.float32)
            b = jnp.dot(x, w3_ref[0].astype(MXU_DTYPE), preferred_element_type=jnp.float32)
            hdn = (_silu(a) * b).astype(MXU_DTYPE)

            @pl.when((f == 0) & (i >= 1))
            def _():
                wait_out(i - 1)

            for c0 in range(0, d, EXPERT_OUT_TILE):
                cols = pl.ds(c0, EXPERT_OUT_TILE)
                part = jnp.dot(hdn, w2_ref[0, :, cols].astype(MXU_DTYPE), preferred_element_type=jnp.float32)

                @pl.when(f == 0)
                def _():
                    acc_s[rows, cols] = part

                @pl.when(f > 0)
                def _():
                    acc_s[rows, cols] += part

    @pl.when(f == nf - 1)
    def _():
        @pl.when(active)
        def _():
            issue_out(i)

        @pl.when(active & (i == pl.num_programs(0) - 1))
        def _():
            wait_out(i)


def _experts(u, sched, w1, w3, w2, n_items_max, p_len):
    tp, d = u.shape
    ff = w1.shape[2]
    tf = EXPERT_FF_TILE
    nf = ff // tf
    n_items, n_used, item_e, item_start, item_rows, item_real, tok_sorted = sched

    def w_in_map(i, f, n_items, n_used, item_e, *_):
        return (item_e[i], 0, jnp.where(i < n_items[0], f, nf - 1))

    def w_out_map(i, f, n_items, n_used, item_e, *_):
        return (item_e[i], jnp.where(i < n_items[0], f, nf - 1), 0)

    grid_spec = pltpu.PrefetchScalarGridSpec(
        num_scalar_prefetch=7,
        grid=(n_items_max, nf),
        in_specs=[
            pl.BlockSpec(memory_space=pl.ANY),
            pl.BlockSpec((1, d, tf), w_in_map),
            pl.BlockSpec((1, d, tf), w_in_map),
            pl.BlockSpec((1, tf, d), w_out_map),
        ],
        out_specs=pl.BlockSpec(memory_space=pl.ANY),
        scratch_shapes=[
            pltpu.VMEM((EXPERT_ROWS, d), jnp.float32),
            pltpu.VMEM((EXPERT_ROWS, d), MXU_DTYPE),
            pltpu.VMEM((EXPERT_ROWS, d), jnp.float32),
            pltpu.SemaphoreType.DMA((2,)),
        ],
    )
    return pl.pallas_call(
        _expert_kernel,
        grid_spec=grid_spec,
        out_shape=jax.ShapeDtypeStruct((p_len, d), jnp.float32),
        compiler_params=_params(("arbitrary", "arbitrary")),
        name="experts",
    )(n_items, n_used, item_e, item_start, item_rows, item_real, tok_sorted, u, w1, w3, w2)


def _expert_schedule(eid, tok_rows, n_items_max, p_len):
    i32 = jnp.int32
    flat_e = eid[tok_rows].reshape(-1)
    flat_tok = jnp.repeat(jnp.asarray(tok_rows, i32), MOE_TOPK)
    onehot = (flat_e[:, None] == jnp.arange(MOE_EXPERTS, dtype=i32)[None, :]).astype(i32)
    csum = jnp.cumsum(onehot, axis=0)
    counts = csum[-1]
    rank = jnp.take_along_axis(csum, flat_e[:, None], axis=1)[:, 0] - 1
    padded = (counts + MOE_BLOCK - 1) // MOE_BLOCK * MOE_BLOCK
    pad_end = jnp.cumsum(padded)
    pad_start = pad_end - padded
    dest = pad_start[flat_e] + rank
    tok_sorted = jnp.zeros((p_len,), i32).at[dest].set(flat_tok)
    n_used = (pad_end[-1:] // MOE_BLOCK).astype(i32)

    chunks = (padded + EXPERT_ROWS - 1) // EXPERT_ROWS
    chunk_end = jnp.cumsum(chunks)
    item = jnp.arange(n_items_max, dtype=i32)
    item_e = jnp.minimum(jnp.sum((chunk_end[None, :] <= item[:, None]).astype(i32), axis=1), MOE_EXPERTS - 1)
    n_items = chunk_end[-1:].astype(i32)
    last_e = item_e[jnp.maximum(n_items[0] - 1, 0)]
    item_e = jnp.where(item < n_items[0], item_e, last_e)
    k = item - (chunk_end - chunks)[item_e]
    item_start = (pad_start[item_e] + k * EXPERT_ROWS).astype(i32)
    item_rows = jnp.clip(padded[item_e] - k * EXPERT_ROWS, 0, EXPERT_ROWS).astype(i32)
    item_real = jnp.clip(counts[item_e] - k * EXPERT_ROWS, 0, EXPERT_ROWS).astype(i32)
    live = item < n_items[0]
    item_start = jnp.where(live, item_start, 0)
    item_rows = jnp.where(live, item_rows, 0)
    item_real = jnp.where(live, item_real, 0)
    return (n_items, n_used, item_e, item_start, item_rows, item_real, tok_sorted), dest.reshape(-1, MOE_TOPK)


def _final_kernel(tm, dest_ref, h_ref, gate_ref, g_ref, ys_hbm, o_ref, ybuf, sem):
    t = pl.program_id(0)
    n_tiles = pl.num_programs(0)

    def issue(tile, buf):
        base = tile * (tm * MOE_TOPK)
        for j in range(tm):
            for k in range(MOE_TOPK):
                pos = dest_ref[base + j * MOE_TOPK + k]
                pltpu.make_async_copy(ys_hbm.at[pl.ds(pos, 1)], ybuf.at[buf, k, pl.ds(j, 1)], sem.at[buf]).start()

    @pl.when(t == 0)
    def _():
        issue(0, 0)

    @pl.when(t + 1 < n_tiles)
    def _():
        issue(t + 1, (t + 1) % 2)

    buf = t % 2
    for k in range(MOE_TOPK):
        pltpu.make_async_copy(ys_hbm.at[pl.ds(0, tm)], ybuf.at[buf, k], sem.at[buf]).wait()
    moe = gate_ref[:, 0:1] * ybuf[buf, 0] + gate_ref[:, 1:2] * ybuf[buf, 1]
    h = h_ref[...] + moe
    o_ref[...] = h * lax.rsqrt(jnp.mean(h * h, axis=-1, keepdims=True) + EPS) * g_ref[...]


def _final(h2, ys, dest_x, gates, g_final, batch, lp, seq):
    tp, d = h2.shape
    tm = SSD_CHUNK
    skip = (lp - seq) // tm
    per_b = seq // tm
    src = lambda w: pl.BlockSpec((tm, w), lambda t, dest: ((t // per_b) * (lp // tm) + skip + t % per_b, 0))
    grid_spec = pltpu.PrefetchScalarGridSpec(
        num_scalar_prefetch=1,
        grid=(batch * per_b,),
        in_specs=[src(d), src(MOE_TOPK), pl.BlockSpec((1, d), lambda t, dest: (0, 0)),
                  pl.BlockSpec(memory_space=pl.ANY)],
        out_specs=pl.BlockSpec((tm, d), lambda t, dest: (t, 0)),
        scratch_shapes=[pltpu.VMEM((2, MOE_TOPK, tm, d), jnp.float32), pltpu.SemaphoreType.DMA((2,))],
    )
    return pl.pallas_call(
        functools.partial(_final_kernel, tm),
        grid_spec=grid_spec,
        out_shape=jax.ShapeDtypeStruct((batch * seq, d), jnp.float32),
        compiler_params=_params(("arbitrary",)),
        name="final_norm",
    )(dest_x, h2, gates, g_final, ys)


def kernel(x, meta_tokens, norm_mix, w_in, ssd_conv_w, ssd_conv_b, ssd_dt_bias, ssd_a_log, ssd_d, ssd_norm, w_ssd_out, lru_conv_w, lru_conv_b, lru_wa, lru_ba, lru_wx, lru_bx, lru_lambda, w_lru_out, gate_bias, w_out, norm_ffn, w_router_group, w_router_expert, w_exp_gate, w_exp_up, w_exp_down, norm_final):
    batch, seq, d = x.shape
    depth = norm_mix.shape[0]
    assert depth == 1 and seq % SSD_CHUNK == 0
    l = N_META + seq
    lp = -(-l // SSD_CHUNK) * SSD_CHUNK
    n_pad = lp - l
    d_inner = d
    heads = d_inner // SSD_HEAD_DIM
    conv_dim = d_inner + 2 * SSD_GROUPS * SSD_STATE
    col_xbc = d_inner
    col_dt = col_xbc + conv_dim
    col_lx = col_dt + heads
    col_ly = col_lx + d
    col_gl = col_ly + d
    f32 = jnp.float32
    lyr = 0

    meta = jnp.broadcast_to(meta_tokens.astype(f32)[None], (batch, N_META, d))
    hp = jnp.concatenate([jnp.zeros((batch, n_pad, d), f32), meta, x], axis=1).reshape(batch * lp, d)

    w = w_in[lyr]
    w_main = jnp.concatenate([w[:, :col_dt], w[:, col_lx:]], axis=1).astype(MXU_DTYPE)
    w_dt = jnp.zeros((d, LANES), f32).at[:, :heads].set(w[:, col_dt:col_lx]).astype(MXU_DTYPE)
    dtb = jnp.zeros((1, LANES), f32).at[0, :heads].set(ssd_dt_bias[lyr])
    alog = jnp.zeros((1, LANES), f32).at[0, :heads].set(ssd_a_log[lyr])
    proj, dt, acum = _norm_inproj(hp, norm_mix[lyr][None, :], w_main, w_dt, dtb, alog)
    n_main = w_main.shape[1]
    proj3 = proj.reshape(batch, lp, n_main)
    lx_col = col_dt
    ly_col = lx_col + d
    gl_col = ly_col + d

    ys = _ssd(proj3, dt.reshape(batch, lp, LANES), acum.reshape(batch, lp, LANES), n_pad, ssd_conv_w[lyr],
              ssd_conv_b[lyr], ssd_d[lyr], ssd_norm[lyr], d_inner)
    yl = _lru(proj3, n_pad, lx_col, ly_col, lru_conv_w[lyr], lru_conv_b[lyr], lru_wa[lyr].astype(MXU_DTYPE),
              lru_ba[lyr], lru_wx[lyr].astype(MXU_DTYPE), lru_bx[lyr], lru_lambda[lyr], d)

    mixed = _merge(ys.reshape(batch * lp, d), yl.reshape(batch * lp, d), w_ssd_out[lyr].astype(MXU_DTYPE),
                   w_lru_out[lyr].astype(MXU_DTYPE), proj, gl_col, gate_bias[lyr])

    w_router = jnp.zeros((d, LANES), f32)
    w_router = w_router.at[:, :MOE_GROUPS].set(w_router_group[lyr])
    w_router = w_router.at[:, MOE_GROUPS:MOE_GROUPS + MOE_EXPERTS].set(w_router_expert[lyr])
    h2, u2, eid, gates = _outproj_router(mixed, hp, w_out[lyr].astype(MXU_DTYPE), norm_ffn[lyr][None, :],
                                         w_router.astype(MXU_DTYPE))

    tok_rows = np.nonzero((np.arange(batch * lp) % lp) >= n_pad)[0].astype(np.int32)
    n_asg = tok_rows.shape[0] * MOE_TOPK
    p_max = n_asg + MOE_EXPERTS * (MOE_BLOCK - 1)
    n_items_max = MOE_EXPERTS + p_max // EXPERT_ROWS
    p_len = -(-p_max // MOE_BLOCK) * MOE_BLOCK
    sched, dest = _expert_schedule(eid, tok_rows, n_items_max, p_len)
    ff = w_exp_gate.shape[-1]
    ys_sorted = _experts(u2, sched, w_exp_gate.reshape(MOE_EXPERTS, d, ff), w_exp_up.reshape(MOE_EXPERTS, d, ff),
                         w_exp_down.reshape(MOE_EXPERTS, ff, d), n_items_max, p_len)

    dest_x = dest.reshape(batch, l, MOE_TOPK)[:, N_META:].reshape(-1)
    out = _final(h2, ys_sorted, dest_x, gates, norm_final[None, :], batch, lp, seq)
    return out.reshape(batch, seq, d)
```

```python
import functools

import numpy as np
import jax
import jax.numpy as jnp
from jax import lax
from jax.experimental import pallas as pl
from jax.experimental.pallas import tpu as pltpu

N_META = 16
CONV_K = 4
EPS = 1e-6
SSD_HEAD_DIM = 64
SSD_HEAD_SHIFT = 6
SSD_GROUPS = 8
SSD_STATE = 128
SSD_CHUNK = 128
LRU_HEADS = 8
LRU_C = 8.0
N_BRANCH = 2
MOE_GROUPS = 8
MOE_EXP_PER_GROUP = 8
MOE_EXPERTS = MOE_GROUPS * MOE_EXP_PER_GROUP
MOE_TOPK = 2
MOE_BLOCK = 64

LANES = 128
SUBLANES = 8
VMEM_LIMIT = 56 * 1024 * 1024
MXU_DTYPE = jnp.bfloat16

EXPERT_ROWS = 1024
EXPERT_FF_TILE = 512
EXPERT_OUT_TILE = 1024
GATHER_UNROLL = 8


def _pick(n, options):
    for o in options:
        if n % o == 0:
            return o
    raise ValueError(f"no tile in {options} divides {n}")


def _params(sem, vmem=VMEM_LIMIT):
    return pltpu.CompilerParams(dimension_semantics=sem, vmem_limit_bytes=vmem)


def _dot(a, b):
    return jnp.dot(a.astype(MXU_DTYPE), b.astype(MXU_DTYPE), preferred_element_type=jnp.float32)


def _dot_exact_rhs(v, sel):
    sel = sel.astype(jnp.bfloat16)
    hi = v.astype(jnp.bfloat16)
    r1 = v - hi.astype(jnp.float32)
    mid = r1.astype(jnp.bfloat16)
    lo = (r1 - mid.astype(jnp.float32)).astype(jnp.bfloat16)
    f32 = jnp.float32
    return (jnp.dot(hi, sel, preferred_element_type=f32) + jnp.dot(mid, sel, preferred_element_type=f32)
            + jnp.dot(lo, sel, preferred_element_type=f32))


def _dot_exact_lhs(sel, v):
    sel = sel.astype(jnp.bfloat16)
    hi = v.astype(jnp.bfloat16)
    r1 = v - hi.astype(jnp.float32)
    mid = r1.astype(jnp.bfloat16)
    lo = (r1 - mid.astype(jnp.float32)).astype(jnp.bfloat16)
    f32 = jnp.float32
    return (jnp.dot(sel, hi, preferred_element_type=f32) + jnp.dot(sel, mid, preferred_element_type=f32)
            + jnp.dot(sel, lo, preferred_element_type=f32))


def _softplus(x):
    return jnp.maximum(x, 0.0) + jnp.log1p(jnp.exp(-jnp.abs(x)))


def _sigmoid(x):
    return 1.0 / (1.0 + jnp.exp(-x))


def _silu(x):
    return x * _sigmoid(x)


def _causal_conv_chunk(x_ref, w_ref, b_ref, c):
    q = SSD_CHUNK
    r0 = pl.multiple_of(c * q, q)
    cur = x_ref[0, pl.ds(r0, q), :]
    prev = x_ref[0, pl.ds(pl.multiple_of(jnp.maximum(r0 - SUBLANES, 0), SUBLANES), SUBLANES), :]
    prev = jnp.where(c > 0, prev, 0.0)
    sub = lax.broadcasted_iota(jnp.int32, prev.shape, 0)
    acc = b_ref[...] + w_ref[CONV_K - 1:CONV_K, :] * cur
    for k in range(1, CONV_K):
        rolled = pltpu.roll(cur, k, 0)
        head = jnp.where(sub < k, pltpu.roll(prev, k, 0), rolled[:SUBLANES])
        shifted = jnp.concatenate([head, rolled[SUBLANES:]], axis=0)
        acc = acc + w_ref[CONV_K - 1 - k:CONV_K - k, :] * shifted
    return acc


def _norm_inproj_kernel(x_ref, g_ref, w_ref, wdt_ref, dtb_ref, alog_ref, o_ref, dt_ref, acum_ref, xn_ref):
    @pl.when(pl.program_id(1) == 0)
    def _():
        x = x_ref[...]
        y = x * lax.rsqrt(jnp.mean(x * x, axis=-1, keepdims=True) + EPS)
        xn_ref[...] = (y * g_ref[...]).astype(xn_ref.dtype)
        raw = jnp.dot(xn_ref[...], wdt_ref[...], preferred_element_type=jnp.float32)
        dt = _softplus(raw + dtb_ref[...])
        dt_ref[...] = dt
        adt = dt * (-jnp.exp(alog_ref[...]))
        q = SSD_CHUNK
        tri = (lax.broadcasted_iota(jnp.int32, (q, q), 0) >= lax.broadcasted_iota(jnp.int32, (q, q), 1))
        for c in range(x.shape[0] // q):
            acum_ref[c * q:(c + 1) * q, :] = _dot_exact_lhs(tri.astype(jnp.float32), adt[c * q:(c + 1) * q, :])

    o_ref[...] = jnp.dot(xn_ref[...], w_ref[...], preferred_element_type=jnp.float32)


def _norm_inproj(hp, g, w, wdt, dtb, alog):
    tp, d = hp.shape
    n = w.shape[1]
    tm = _pick(tp, (1024, 512, 256, 128))
    tn = _pick(n, (1024, 512, 256, 128))
    vec = pl.BlockSpec((1, LANES), lambda i, j: (0, 0))
    return pl.pallas_call(
        _norm_inproj_kernel,
        grid=(tp // tm, n // tn),
        in_specs=[
            pl.BlockSpec((tm, d), lambda i, j: (i, 0)),
            pl.BlockSpec((1, d), lambda i, j: (0, 0)),
            pl.BlockSpec((d, tn), lambda i, j: (0, j)),
            pl.BlockSpec((d, LANES), lambda i, j: (0, 0)),
            vec, vec,
        ],
        out_specs=[
            pl.BlockSpec((tm, tn), lambda i, j: (i, j)),
            pl.BlockSpec((tm, LANES), lambda i, j: (i, 0)),
            pl.BlockSpec((tm, LANES), lambda i, j: (i, 0)),
        ],
        out_shape=[
            jax.ShapeDtypeStruct((tp, n), jnp.float32),
            jax.ShapeDtypeStruct((tp, LANES), jnp.float32),
            jax.ShapeDtypeStruct((tp, LANES), jnp.float32),
        ],
        scratch_shapes=[pltpu.VMEM((tm, d), MXU_DTYPE)],
        compiler_params=_params(("parallel", "arbitrary")),
        name="norm_inproj",
    )(hp, g, w, wdt, dtb, alog)


def _ssd_kernel(n_pad, z_ref, xs_ref, b_ref, c_ref, dt_ref, acum_ref,
                cwx_ref, cwb_ref, cwc_ref, cbx_ref, cbb_ref, cbc_ref,
                d_ref, nw_ref, o_ref,
                xs_s, b_s, c_s, state_s):
    g = pl.program_id(1)
    lp = xs_ref.shape[1]
    q = SSD_CHUNK
    n_chunks = lp // q
    gw = xs_ref.shape[2]
    hpg = gw // SSD_HEAD_DIM
    f32 = jnp.float32

    def conv_into(c, slot):
        xs_s[slot] = _silu(_causal_conv_chunk(xs_ref, cwx_ref, cbx_ref, c))
        b_s[slot] = _silu(_causal_conv_chunk(b_ref, cwb_ref, cbb_ref, c))
        c_s[slot] = _silu(_causal_conv_chunk(c_ref, cwc_ref, cbc_ref, c))

    rows = lax.broadcasted_iota(jnp.int32, (LANES, gw), 0)
    cols = lax.broadcasted_iota(jnp.int32, (LANES, gw), 1)
    expand = (rows == g * hpg + lax.shift_right_logical(cols, SSD_HEAD_SHIFT)).astype(f32)

    state_s[...] = jnp.zeros_like(state_s)
    li = lax.broadcasted_iota(jnp.int32, (q, q), 0)
    si = lax.broadcasted_iota(jnp.int32, (q, q), 1)
    causal = li >= si
    lane = lax.broadcasted_iota(jnp.int32, (q, gw), 1)
    row = lax.broadcasted_iota(jnp.int32, (q, gw), 0)

    conv_into(0, 0)

    def chunk(c, carry):
        r0 = pl.multiple_of(c * q, q)
        sl = pl.ds(r0, q)
        slot = c % 2
        xs = xs_s[slot]
        bm = b_s[slot]
        cm = c_s[slot]
        conv_into(jnp.minimum(c + 1, n_chunks - 1), 1 - slot)
        dt_g = _dot_exact_rhs(dt_ref[0, sl, :], expand)
        acum = _dot_exact_rhs(acum_ref[0, sl, :], expand)
        xdt = jnp.where(row + r0 >= n_pad, xs * dt_g, 0.0)
        acum_t = acum.T
        a_last = acum[q - 1:q, :]
        state = state_s[...]

        cb = lax.dot_general(cm.astype(MXU_DTYPE), bm.astype(MXU_DTYPE),
                             (((1,), (1,)), ((), ())), preferred_element_type=f32)
        y = _dot(cm, state) * jnp.exp(acum)
        for j in range(hpg):
            c0 = j * SSD_HEAD_DIM
            seg = acum[:, c0:c0 + 1] - acum_t[c0:c0 + 1, :]
            decay = jnp.exp(jnp.where(causal, seg, -jnp.inf))
            in_head = (lane >= c0) & (lane < c0 + SSD_HEAD_DIM)
            y = y + _dot(cb * decay, jnp.where(in_head, xdt, 0.0))
        y = y + d_ref[...] * xs

        state_s[...] = state * jnp.exp(a_last) + _dot(bm.T, xdt * jnp.exp(a_last - acum))

        yz = y * _silu(z_ref[0, sl, :])
        yn = yz * lax.rsqrt(jnp.mean(yz * yz, axis=-1, keepdims=True) + EPS)
        o_ref[0, sl, :] = (yn * nw_ref[...]).astype(o_ref.dtype)
        return carry

    lax.fori_loop(0, lp // q, chunk, 0)


def _ssd(proj3, dt3, acum3, n_pad, cw, cb, d_skip, norm_w, d_inner):
    b, lp, _ = proj3.shape
    gw = d_inner // SSD_GROUPS
    ns = SSD_STATE
    q = SSD_CHUNK
    xs_blk0 = d_inner // gw
    b_blk0 = 2 * d_inner // ns
    c_blk0 = (2 * d_inner + SSD_GROUPS * ns) // ns
    f32 = jnp.float32

    cwx = cw[:, :d_inner]
    cwb = cw[:, d_inner:d_inner + SSD_GROUPS * ns]
    cwc = cw[:, d_inner + SSD_GROUPS * ns:]
    cbx = cb[None, :d_inner]
    cbb = cb[None, d_inner:d_inner + SSD_GROUPS * ns]
    cbc = cb[None, d_inner + SSD_GROUPS * ns:]
    d_ch = jnp.repeat(d_skip, SSD_HEAD_DIM)[None, :]
    nw = norm_w[None, :]

    seq = lambda w, off: pl.BlockSpec((1, lp, w), lambda i, g: (i, 0, off + g))
    head = pl.BlockSpec((1, lp, LANES), lambda i, g: (i, 0, 0))
    vec = lambda w: pl.BlockSpec((1, w), lambda i, g: (0, g))
    cwspec = lambda w: pl.BlockSpec((CONV_K, w), lambda i, g: (0, g))
    return pl.pallas_call(
        functools.partial(_ssd_kernel, n_pad),
        grid=(b, SSD_GROUPS),
        in_specs=[
            seq(gw, 0), seq(gw, xs_blk0), seq(ns, b_blk0), seq(ns, c_blk0), head, head,
            cwspec(gw), cwspec(ns), cwspec(ns), vec(gw), vec(ns), vec(ns),
            vec(gw), vec(gw),
        ],
        out_specs=pl.BlockSpec((1, lp, gw), lambda i, g: (i, 0, g)),
        out_shape=jax.ShapeDtypeStruct((b, lp, d_inner), MXU_DTYPE),
        scratch_shapes=[
            pltpu.VMEM((2, q, gw), f32), pltpu.VMEM((2, q, ns), f32), pltpu.VMEM((2, q, ns), f32),
            pltpu.VMEM((ns, gw), f32),
        ],
        compiler_params=_params(("parallel", "parallel")),
        name="ssd",
    )(proj3, proj3, proj3, proj3, dt3, acum3, cwx, cwb, cwc, cbx, cbb, cbc, d_ch, nw)


def _lru_kernel(n_pad, lx_ref, ly_ref, cw_ref, cb_ref, wa_ref, ba_ref, wx_ref, bx_ref, lam_ref, o_ref):
    lp = lx_ref.shape[1]
    w = lx_ref.shape[2]
    q = SSD_CHUNK
    neg_c_softplus = (-LRU_C) * _softplus(-lam_ref[...])
    row = lax.broadcasted_iota(jnp.int32, (q, w), 0)
    sub = lax.broadcasted_iota(jnp.int32, (SUBLANES, w), 0)

    def chunk(c, h_prev):
        r0 = pl.multiple_of(c * q, q)
        sl = pl.ds(r0, q)
        xr = _causal_conv_chunk(lx_ref, cw_ref, cb_ref, c)
        gate_r = _sigmoid(_dot(xr, wa_ref[0]) + ba_ref[...])
        gate_i = _sigmoid(_dot(xr, wx_ref[0]) + bx_ref[...])
        log_a = gate_r * neg_c_softplus
        a = jnp.exp(log_a)
        mult = jnp.sqrt(jnp.tanh(-log_a) * (a * a + 1.0))
        u = jnp.where(row + r0 >= n_pad, mult * gate_i * xr, 0.0)
        gate_y = jax.nn.gelu(ly_ref[0, sl, :])

        out = []
        for t in range(q // SUBLANES):
            rows8 = slice(t * SUBLANES, (t + 1) * SUBLANES)
            at, ut = a[rows8], u[rows8]
            for d in (1, 2, 4):
                keep = sub >= d
                ut = jnp.where(keep, at * pltpu.roll(ut, d, 0) + ut, ut)
                at = jnp.where(keep, at * pltpu.roll(at, d, 0), at)
            h = at * h_prev + ut
            h_prev = jnp.broadcast_to(h[SUBLANES - 1:SUBLANES, :], (SUBLANES, w))
            out.append(h * gate_y[rows8])
        o_ref[0, sl, :] = jnp.concatenate(out, axis=0).astype(o_ref.dtype)
        return h_prev

    lax.fori_loop(0, lp // q, chunk, jnp.zeros((SUBLANES, w), jnp.float32))


def _lru(proj3, n_pad, lx_col, ly_col, cw, cb, wa, ba, wx, bx, lam, width):
    b, lp, _ = proj3.shape
    w = width // LRU_HEADS
    seq = lambda off: pl.BlockSpec((1, lp, w), lambda i, h: (i, 0, off + h))
    vec = pl.BlockSpec((1, w), lambda i, h: (0, h))
    mat = pl.BlockSpec((1, w, w), lambda i, h: (h, 0, 0))
    return pl.pallas_call(
        functools.partial(_lru_kernel, n_pad),
        grid=(b, LRU_HEADS),
        in_specs=[seq(lx_col // w), seq(ly_col // w),
                  pl.BlockSpec((CONV_K, w), lambda i, h: (0, h)), vec, mat, vec, mat, vec, vec],
        out_specs=pl.BlockSpec((1, lp, w), lambda i, h: (i, 0, h)),
        out_shape=jax.ShapeDtypeStruct((b, lp, width), MXU_DTYPE),
        compiler_params=_params(("parallel", "parallel")),
        name="rglru",
    )(proj3, proj3, cw, cb[None, :], wa, ba[None, :], wx, bx[None, :], lam[None, :])


def _merge_kernel(ys_ref, yl_ref, ws_ref, wl_ref, g0_ref, g1_ref, gb_ref, o_ref):
    y_ssd = jnp.dot(ys_ref[...], ws_ref[...], preferred_element_type=jnp.float32)
    y_lru = jnp.dot(yl_ref[...], wl_ref[...], preferred_element_type=jnp.float32)
    gate0 = _sigmoid(g0_ref[...] + gb_ref[0:1, :])
    gate1 = _sigmoid(g1_ref[...] + gb_ref[1:2, :])
    o_ref[...] = (gate0 * y_ssd + gate1 * y_lru).astype(o_ref.dtype)


def _merge(ys, yl, ws, wl, proj, gate_col, gate_bias):
    tp, d = ys.shape
    tm = _pick(tp, (1024, 512, 256, 128))
    tn = _pick(d, (512, 256, 128))
    g0 = gate_col // tn
    g1 = (gate_col + d) // tn
    return pl.pallas_call(
        _merge_kernel,
        grid=(tp // tm, d // tn),
        in_specs=[
            pl.BlockSpec((tm, d), lambda i, j: (i, 0)),
            pl.BlockSpec((tm, d), lambda i, j: (i, 0)),
            pl.BlockSpec((d, tn), lambda i, j: (0, j)),
            pl.BlockSpec((d, tn), lambda i, j: (0, j)),
            pl.BlockSpec((tm, tn), lambda i, j: (i, g0 + j)),
            pl.BlockSpec((tm, tn), lambda i, j: (i, g1 + j)),
            pl.BlockSpec((N_BRANCH, tn), lambda i, j: (0, j)),
        ],
        out_specs=pl.BlockSpec((tm, tn), lambda i, j: (i, j)),
        out_shape=jax.ShapeDtypeStruct((tp, d), MXU_DTYPE),
        compiler_params=_params(("parallel", "arbitrary")),
        name="merge",
    )(ys, yl, ws, wl, proj, proj, gate_bias)


def _first_index_of_max(p, valid, lane):
    pm = jnp.where(valid, p, -jnp.inf)
    top = jnp.max(pm, axis=-1, keepdims=True)
    idx = jnp.min(jnp.where(valid & (pm == top), lane, LANES), axis=-1, keepdims=True)
    return top, idx


def _masked_softmax(x, valid):
    m = jnp.max(jnp.where(valid, x, -jnp.inf), axis=-1, keepdims=True)
    e = jnp.where(valid, jnp.exp(x - m), 0.0)
    return e / jnp.sum(e, axis=-1, keepdims=True)


def _outproj_router_kernel(mix_ref, h_ref, wo_ref, g_ref, wr_ref, h2_ref, u_ref, eid_ref, gate_ref):
    h2 = h_ref[...] + jnp.dot(mix_ref[...], wo_ref[...], preferred_element_type=jnp.float32)
    h2_ref[...] = h2
    u = h2 * lax.rsqrt(jnp.mean(h2 * h2, axis=-1, keepdims=True) + EPS) * g_ref[...]
    u_ref[...] = u
    logits = _dot(u, wr_ref[...])
    lane = lax.broadcasted_iota(jnp.int32, logits.shape, 1)

    g_prob = _masked_softmax(logits, lane < MOE_GROUPS)
    g_p, g_idx = _first_index_of_max(g_prob, lane < MOE_GROUPS, lane)

    e_lo = MOE_GROUPS + g_idx * MOE_EXP_PER_GROUP
    in_group = (lane >= e_lo) & (lane < e_lo + MOE_EXP_PER_GROUP)
    e_prob = _masked_softmax(logits, in_group)
    p1, i1 = _first_index_of_max(e_prob, in_group, lane)
    rest = in_group & (lane != i1)
    p2, i2 = _first_index_of_max(e_prob, rest, lane)
    denom = p1 + p2

    col = lax.broadcasted_iota(jnp.int32, eid_ref.shape, 1)
    eid_ref[...] = jnp.where(col == 0, i1, i2) - MOE_GROUPS
    gate_ref[...] = jnp.where(col == 0, g_p * p1 / denom, g_p * p2 / denom)


def _outproj_router(mixed, hp, wo, g_ffn, wr):
    tp, d = hp.shape
    tm = _pick(tp, (512, 256, 128))
    row = lambda w: pl.BlockSpec((tm, w), lambda i: (i, 0))
    return pl.pallas_call(
        _outproj_router_kernel,
        grid=(tp // tm,),
        in_specs=[row(d), row(d),
                  pl.BlockSpec((d, d), lambda i: (0, 0)),
                  pl.BlockSpec((1, d), lambda i: (0, 0)),
                  pl.BlockSpec((d, LANES), lambda i: (0, 0))],
        out_specs=[row(d), row(d), row(MOE_TOPK), row(MOE_TOPK)],
        out_shape=[jax.ShapeDtypeStruct((tp, d), jnp.float32),
                   jax.ShapeDtypeStruct((tp, d), jnp.float32),
                   jax.ShapeDtypeStruct((tp, MOE_TOPK), jnp.int32),
                   jax.ShapeDtypeStruct((tp, MOE_TOPK), jnp.float32)],
        compiler_params=_params(("parallel",)),
        name="outproj_router",
    )(mixed, hp, wo, g_ffn, wr)


def _wait_rows(n, make_copy):
    for k in range(EXPERT_ROWS.bit_length() - 1, -1, -1):
        @pl.when((n & (1 << k)) != 0)
        def _():
            make_copy(1 << k).wait()


def _expert_kernel(n_items_ref, n_used_ref, item_e_ref, item_start_ref, item_rows_ref, item_real_ref, tok_ref,
                   u_hbm, w1_ref, w3_ref, w2_ref, y_hbm,
                   xg_s, xb_s, acc_s, sem):
    i = pl.program_id(0)
    f = pl.program_id(1)
    nf = pl.num_programs(1)
    n_items = n_items_ref[0]
    active = i < n_items
    n_rows = item_rows_ref[i]
    d = acc_s.shape[1]
    blk = MOE_BLOCK

    def issue_gather(item):
        start = item_start_ref[item]
        n = item_real_ref[item]
        n_groups = n // GATHER_UNROLL

        def one(r):
            pltpu.make_async_copy(u_hbm.at[pl.ds(tok_ref[start + r], 1)], xg_s.at[pl.ds(r, 1)], sem.at[0]).start()

        def group(g, c):
            for j in range(GATHER_UNROLL):
                one(g * GATHER_UNROLL + j)
            return c
        lax.fori_loop(0, n_groups, group, 0)

        def single(r, c):
            one(r)
            return c
        lax.fori_loop(n_groups * GATHER_UNROLL, n, single, 0)

    def wait_gather(item):
        _wait_rows(item_real_ref[item],
                   lambda m: pltpu.make_async_copy(u_hbm.at[pl.ds(0, m)], xg_s.at[pl.ds(0, m)], sem.at[0]))

    def out_copy(item, j):
        r0 = pl.multiple_of(j * blk, blk)
        dst0 = pl.multiple_of(item_start_ref[item] + r0, blk)
        return pltpu.make_async_copy(acc_s.at[pl.ds(r0, blk)], y_hbm.at[pl.ds(dst0, blk)], sem.at[1])

    def issue_out(item):
        def body(j, c):
            out_copy(item, j).start()
            return c
        lax.fori_loop(0, item_rows_ref[item] // blk, body, 0)

    def wait_out(item):
        def body(j, c):
            out_copy(item, j).wait()
            return c
        lax.fori_loop(0, item_rows_ref[item] // blk, body, 0)

    @pl.when((i == 0) & (f == 0))
    def _():
        xg_s[...] = jnp.zeros_like(xg_s)
        acc_s[pl.ds(0, blk), :] = jnp.zeros((blk, d), acc_s.dtype)
        n_used = n_used_ref[0]
        n_blocks = y_hbm.shape[0] // blk

        def tail_copy(j):
            return pltpu.make_async_copy(acc_s.at[pl.ds(0, blk)], y_hbm.at[pl.ds(pl.multiple_of(j * blk, blk), blk)],
                                         sem.at[1])

        def fill(j, c):
            tail_copy(j).start()
            return c
        lax.fori_loop(n_used, n_blocks, fill, 0)

        def drain(j, c):
            tail_copy(j).wait()
            return c
        lax.fori_loop(n_used, n_blocks, drain, 0)

        @pl.when(active)
        def _():
            issue_gather(0)

    @pl.when(active & (f == 0))
    def _():
        wait_gather(i)
        for m in range(blk, EXPERT_ROWS + 1, blk):
            @pl.when(n_rows == m)
            def _():
                xb_s[pl.ds(0, m), :] = xg_s[pl.ds(0, m), :].astype(xb_s.dtype)

        @pl.when(i + 1 < n_items)
        def _():
            issue_gather(i + 1)

    @pl.when((f == 0) & (i >= 1) & (i - 1 < n_items) & jnp.logical_not(active))
    def _():
        wait_out(i - 1)

    for m in range(blk, EXPERT_ROWS + 1, blk):
        @pl.when(active & (n_rows == m))
        def _():
            rows = pl.ds(0, m)
            x = xb_s[rows, :]
            a = jnp.dot(x, w1_ref[0].astype(MXU_DTYPE), preferred_element_type=jnp.float32)
            b = jnp.dot(x, w3_ref[0].astype(MXU_DTYPE), preferred_element_type=jnp.float32)
            hdn = (_silu(a) * b).astype(MXU_DTYPE)

            @pl.when((f == 0) & (i >= 1))
            def _():
                wait_out(i - 1)

            for c0 in range(0, d, EXPERT_OUT_TILE):
                cols = pl.ds(c0, EXPERT_OUT_TILE)
                part = jnp.dot(hdn, w2_ref[0, :, cols].astype(MXU_DTYPE), preferred_element_type=jnp.float32)

                @pl.when(f == 0)
                def _():
                    acc_s[rows, cols] = part

                @pl.when(f > 0)
                def _():
                    acc_s[rows, cols] += part

    @pl.when(f == nf - 1)
    def _():
        @pl.when(active)
        def _():
            issue_out(i)

        @pl.when(active & (i == pl.num_programs(0) - 1))
        def _():
            wait_out(i)


def _experts(u, sched, w1, w3, w2, n_items_max, p_len):
    tp, d = u.shape
    ff = w1.shape[2]
    tf = EXPERT_FF_TILE
    nf = ff // tf
    n_items, n_used, item_e, item_start, item_rows, item_real, tok_sorted = sched

    def w_in_map(i, f, n_items, n_used, item_e, *_):
        return (item_e[i], 0, jnp.where(i < n_items[0], f, nf - 1))

    def w_out_map(i, f, n_items, n_used, item_e, *_):
        return (item_e[i], jnp.where(i < n_items[0], f, nf - 1), 0)

    grid_spec = pltpu.PrefetchScalarGridSpec(
        num_scalar_prefetch=7,
        grid=(n_items_max, nf),
        in_specs=[
            pl.BlockSpec(memory_space=pl.ANY),
            pl.BlockSpec((1, d, tf), w_in_map),
            pl.BlockSpec((1, d, tf), w_in_map),
            pl.BlockSpec((1, tf, d), w_out_map),
        ],
        out_specs=pl.BlockSpec(memory_space=pl.ANY),
        scratch_shapes=[
            pltpu.VMEM((EXPERT_ROWS, d), jnp.float32),
            pltpu.VMEM((EXPERT_ROWS, d), MXU_DTYPE),
            pltpu.VMEM((EXPERT_ROWS, d), jnp.float32),
            pltpu.SemaphoreType.DMA((2,)),
        ],
    )
    return pl.pallas_call(
        _expert_kernel,
        grid_spec=grid_spec,
        out_shape=jax.ShapeDtypeStruct((p_len, d), jnp.float32),
        compiler_params=_params(("arbitrary", "arbitrary")),
        name="experts",
    )(n_items, n_used, item_e, item_start, item_rows, item_real, tok_sorted, u, w1, w3, w2)


def _expert_schedule(eid, tok_rows, n_items_max, p_len):
    i32 = jnp.int32
    flat_e = eid[tok_rows].reshape(-1)
    flat_tok = jnp.repeat(jnp.asarray(tok_rows, i32), MOE_TOPK)
    onehot = (flat_e[:, None] == jnp.arange(MOE_EXPERTS, dtype=i32)[None, :]).astype(i32)
    csum = jnp.cumsum(onehot, axis=0)
    counts = csum[-1]
    rank = jnp.take_along_axis(csum, flat_e[:, None], axis=1)[:, 0] - 1
    padded = (counts + MOE_BLOCK - 1) // MOE_BLOCK * MOE_BLOCK
    pad_end = jnp.cumsum(padded)
    pad_start = pad_end - padded
    dest = pad_start[flat_e] + rank
    tok_sorted = jnp.zeros((p_len,), i32).at[dest].set(flat_tok)
    n_used = (pad_end[-1:] // MOE_BLOCK).astype(i32)

    chunks = (padded + EXPERT_ROWS - 1) // EXPERT_ROWS
    chunk_end = jnp.cumsum(chunks)
    item = jnp.arange(n_items_max, dtype=i32)
    item_e = jnp.minimum(jnp.sum((chunk_end[None, :] <= item[:, None]).astype(i32), axis=1), MOE_EXPERTS - 1)
    n_items = chunk_end[-1:].astype(i32)
    last_e = item_e[jnp.maximum(n_items[0] - 1, 0)]
    item_e = jnp.where(item < n_items[0], item_e, last_e)
    k = item - (chunk_end - chunks)[item_e]
    item_start = (pad_start[item_e] + k * EXPERT_ROWS).astype(i32)
    item_rows = jnp.clip(padded[item_e] - k * EXPERT_ROWS, 0, EXPERT_ROWS).astype(i32)
    item_real = jnp.clip(counts[item_e] - k * EXPERT_ROWS, 0, EXPERT_ROWS).astype(i32)
    live = item < n_items[0]
    item_start = jnp.where(live, item_start, 0)
    item_rows = jnp.where(live, item_rows, 0)
    item_real = jnp.where(live, item_real, 0)
    return (n_items, n_used, item_e, item_start, item_rows, item_real, tok_sorted), dest.reshape(-1, MOE_TOPK)


def _final_kernel(tm, dest_ref, h_ref, gate_ref, g_ref, ys_hbm, o_ref, ybuf, sem):
    t = pl.program_id(0)
    n_tiles = pl.num_programs(0)

    def issue(tile, buf):
        base = tile * (tm * MOE_TOPK)
        for j in range(tm):
            for k in range(MOE_TOPK):
                pos = dest_ref[base + j * MOE_TOPK + k]
                pltpu.make_async_copy(ys_hbm.at[pl.ds(pos, 1)], ybuf.at[buf, k, pl.ds(j, 1)], sem.at[buf]).start()

    @pl.when(t == 0)
    def _():
        issue(0, 0)

    @pl.when(t + 1 < n_tiles)
    def _():
        issue(t + 1, (t + 1) % 2)

    buf = t % 2
    for k in range(MOE_TOPK):
        pltpu.make_async_copy(ys_hbm.at[pl.ds(0, tm)], ybuf.at[buf, k], sem.at[buf]).wait()
    moe = gate_ref[:, 0:1] * ybuf[buf, 0] + gate_ref[:, 1:2] * ybuf[buf, 1]
    h = h_ref[...] + moe
    o_ref[...] = h * lax.rsqrt(jnp.mean(h * h, axis=-1, keepdims=True) + EPS) * g_ref[...]


def _final(h2, ys, dest_x, gates, g_final, batch, lp, seq):
    tp, d = h2.shape
    tm = SSD_CHUNK
    skip = (lp - seq) // tm
    per_b = seq // tm
    src = lambda w: pl.BlockSpec((tm, w), lambda t, dest: ((t // per_b) * (lp // tm) + skip + t % per_b, 0))
    grid_spec = pltpu.PrefetchScalarGridSpec(
        num_scalar_prefetch=1,
        grid=(batch * per_b,),
        in_specs=[src(d), src(MOE_TOPK), pl.BlockSpec((1, d), lambda t, dest: (0, 0)),
                  pl.BlockSpec(memory_space=pl.ANY)],
        out_specs=pl.BlockSpec((tm, d), lambda t, dest: (t, 0)),
        scratch_shapes=[pltpu.VMEM((2, MOE_TOPK, tm, d), jnp.float32), pltpu.SemaphoreType.DMA((2,))],
    )
    return pl.pallas_call(
        functools.partial(_final_kernel, tm),
        grid_spec=grid_spec,
        out_shape=jax.ShapeDtypeStruct((batch * seq, d), jnp.float32),
        compiler_params=_params(("arbitrary",)),
        name="final_norm",
    )(dest_x, h2, gates, g_final, ys)


def kernel(x, meta_tokens, norm_mix, w_in, ssd_conv_w, ssd_conv_b, ssd_dt_bias, ssd_a_log, ssd_d, ssd_norm, w_ssd_out, lru_conv_w, lru_conv_b, lru_wa, lru_ba, lru_wx, lru_bx, lru_lambda, w_lru_out, gate_bias, w_out, norm_ffn, w_router_group, w_router_expert, w_exp_gate, w_exp_up, w_exp_down, norm_final):
    batch, seq, d = x.shape
    depth = norm_mix.shape[0]
    assert depth == 1 and seq % SSD_CHUNK == 0
    l = N_META + seq
    lp = -(-l // SSD_CHUNK) * SSD_CHUNK
    n_pad = lp - l
    d_inner = d
    heads = d_inner // SSD_HEAD_DIM
    conv_dim = d_inner + 2 * SSD_GROUPS * SSD_STATE
    col_xbc = d_inner
    col_dt = col_xbc + conv_dim
    col_lx = col_dt + heads
    col_ly = col_lx + d
    col_gl = col_ly + d
    f32 = jnp.float32
    lyr = 0

    meta = jnp.broadcast_to(meta_tokens.astype(f32)[None], (batch, N_META, d))
    hp = jnp.concatenate([jnp.zeros((batch, n_pad, d), f32), meta, x], axis=1).reshape(batch * lp, d)

    w = w_in[lyr]
    w_main = jnp.concatenate([w[:, :col_dt], w[:, col_lx:]], axis=1).astype(MXU_DTYPE)
    w_dt = jnp.zeros((d, LANES), f32).at[:, :heads].set(w[:, col_dt:col_lx]).astype(MXU_DTYPE)
    dtb = jnp.zeros((1, LANES), f32).at[0, :heads].set(ssd_dt_bias[lyr])
    alog = jnp.zeros((1, LANES), f32).at[0, :heads].set(ssd_a_log[lyr])
    proj, dt, acum = _norm_inproj(hp, norm_mix[lyr][None, :], w_main, w_dt, dtb, alog)
    n_main = w_main.shape[1]
    proj3 = proj.reshape(batch, lp, n_main)
    lx_col = col_dt
    ly_col = lx_col + d
    gl_col = ly_col + d

    ys = _ssd(proj3, dt.reshape(batch, lp, LANES), acum.reshape(batch, lp, LANES), n_pad, ssd_conv_w[lyr],
              ssd_conv_b[lyr], ssd_d[lyr], ssd_norm[lyr], d_inner)
    yl = _lru(proj3, n_pad, lx_col, ly_col, lru_conv_w[lyr], lru_conv_b[lyr], lru_wa[lyr].astype(MXU_DTYPE),
              lru_ba[lyr], lru_wx[lyr].astype(MXU_DTYPE), lru_bx[lyr], lru_lambda[lyr], d)

    mixed = _merge(ys.reshape(batch * lp, d), yl.reshape(batch * lp, d), w_ssd_out[lyr].astype(MXU_DTYPE),
                   w_lru_out[lyr].astype(MXU_DTYPE), proj, gl_col, gate_bias[lyr])

    w_router = jnp.zeros((d, LANES), f32)
    w_router = w_router.at[:, :MOE_GROUPS].set(w_router_group[lyr])
    w_router = w_router.at[:, MOE_GROUPS:MOE_GROUPS + MOE_EXPERTS].set(w_router_expert[lyr])
    h2, u2, eid, gates = _outproj_router(mixed, hp, w_out[lyr].astype(MXU_DTYPE), norm_ffn[lyr][None, :],
                                         w_router.astype(MXU_DTYPE))

    tok_rows = np.nonzero((np.arange(batch * lp) % lp) >= n_pad)[0].astype(np.int32)
    n_asg = tok_rows.shape[0] * MOE_TOPK
    p_max = n_asg + MOE_EXPERTS * (MOE_BLOCK - 1)
    n_items_max = MOE_EXPERTS + p_max // EXPERT_ROWS
    p_len = -(-p_max // MOE_BLOCK) * MOE_BLOCK
    sched, dest = _expert_schedule(eid, tok_rows, n_items_max, p_len)
    ff = w_exp_gate.shape[-1]
    ys_sorted = _experts(u2, sched, w_exp_gate.reshape(MOE_EXPERTS, d, ff), w_exp_up.reshape(MOE_EXPERTS, d, ff),
                         w_exp_down.reshape(MOE_EXPERTS, ff, d), n_items_max, p_len)

    dest_x = dest.reshape(batch, l, MOE_TOPK)[:, N_META:].reshape(-1)
    out = _final(h2, ys_sorted, dest_x, gates, norm_final[None, :], batch, lp, seq)
    return out.reshape(batch, seq, d)
```

```python
import functools
import math

import numpy as np
import jax
import jax.numpy as jnp
from jax import lax
from jax.experimental import pallas as pl
from jax.experimental.pallas import tpu as pltpu

N_META = 16
CONV_K = 4
EPS = 1e-6
SSD_HEAD_DIM = 64
SSD_HEAD_SHIFT = 6
SSD_GROUPS = 8
SSD_STATE = 128
SSD_CHUNK = 128
LRU_HEADS = 8
LRU_C = 8.0
N_BRANCH = 2
MOE_GROUPS = 8
MOE_EXP_PER_GROUP = 8
MOE_EXPERTS = MOE_GROUPS * MOE_EXP_PER_GROUP
MOE_TOPK = 2
MOE_BLOCK = 128

LANES = 128
SUBLANES = 8
VMEM_LIMIT = 56 * 1024 * 1024
MXU_DTYPE = jnp.bfloat16

EXPERT_ROWS = 1024
EXPERT_ROW_VARIANTS = (256, 512, 640, 768, 1024)
EXPERT_FF_TILE = 512
EXPERT_OUT_TILE = 1024
GATHER_UNROLL = 8


def _pick(n, options):
    for o in options:
        if n % o == 0:
            return o
    raise ValueError(f"no tile in {options} divides {n}")


def _params(sem, vmem=VMEM_LIMIT):
    return pltpu.CompilerParams(dimension_semantics=sem, vmem_limit_bytes=vmem)


def _dot(a, b):
    return jnp.dot(a.astype(MXU_DTYPE), b.astype(MXU_DTYPE), preferred_element_type=jnp.float32)


def _dot_exact_rhs(v, sel):
    sel = sel.astype(jnp.bfloat16)
    hi = v.astype(jnp.bfloat16)
    r1 = v - hi.astype(jnp.float32)
    mid = r1.astype(jnp.bfloat16)
    lo = (r1 - mid.astype(jnp.float32)).astype(jnp.bfloat16)
    f32 = jnp.float32
    return (jnp.dot(hi, sel, preferred_element_type=f32) + jnp.dot(mid, sel, preferred_element_type=f32)
            + jnp.dot(lo, sel, preferred_element_type=f32))


def _dot_exact_lhs(sel, v):
    sel = sel.astype(jnp.bfloat16)
    hi = v.astype(jnp.bfloat16)
    r1 = v - hi.astype(jnp.float32)
    mid = r1.astype(jnp.bfloat16)
    lo = (r1 - mid.astype(jnp.float32)).astype(jnp.bfloat16)
    f32 = jnp.float32
    return (jnp.dot(sel, hi, preferred_element_type=f32) + jnp.dot(sel, mid, preferred_element_type=f32)
            + jnp.dot(sel, lo, preferred_element_type=f32))


def _softplus(x):
    return jnp.maximum(x, 0.0) + jnp.log1p(jnp.exp(-jnp.abs(x)))


def _sigmoid(x):
    return 1.0 / (1.0 + jnp.exp(-x))


def _silu(x):
    return x * _sigmoid(x)


def _causal_conv_chunk(x_ref, w_ref, b_ref, c):
    q = SSD_CHUNK
    r0 = pl.multiple_of(c * q, q)
    cur = x_ref[0, pl.ds(r0, q), :]
    prev = x_ref[0, pl.ds(pl.multiple_of(jnp.maximum(r0 - SUBLANES, 0), SUBLANES), SUBLANES), :]
    prev = jnp.where(c > 0, prev, 0.0)
    sub = lax.broadcasted_iota(jnp.int32, prev.shape, 0)
    acc = b_ref[...] + w_ref[CONV_K - 1:CONV_K, :] * cur
    for k in range(1, CONV_K):
        rolled = pltpu.roll(cur, k, 0)
        head = jnp.where(sub < k, pltpu.roll(prev, k, 0), rolled[:SUBLANES])
        shifted = jnp.concatenate([head, rolled[SUBLANES:]], axis=0)
        acc = acc + w_ref[CONV_K - 1 - k:CONV_K - k, :] * shifted
    return acc


def _norm_inproj_kernel(n_a, x_ref, g_ref, wa_ref, wb_ref, wdt_ref, dtb_ref, alog_ref, o_ref, dt_ref, acum_ref, xn_ref):
    j = pl.program_id(1)

    @pl.when(j == 0)
    def _():
        x = x_ref[...]
        y = x * lax.rsqrt(jnp.mean(x * x, axis=-1, keepdims=True) + EPS)
        xn_ref[...] = (y * g_ref[...]).astype(xn_ref.dtype)
        raw = jnp.dot(xn_ref[...], wdt_ref[...], preferred_element_type=jnp.float32)
        dt = _softplus(raw + dtb_ref[...])
        dt_ref[...] = dt
        adt = dt * (-jnp.exp(alog_ref[...]))
        q = SSD_CHUNK
        tri = (lax.broadcasted_iota(jnp.int32, (q, q), 0) >= lax.broadcasted_iota(jnp.int32, (q, q), 1))
        for c in range(x.shape[0] // q):
            acum_ref[c * q:(c + 1) * q, :] = _dot_exact_lhs(tri.astype(jnp.float32), adt[c * q:(c + 1) * q, :])

    @pl.when(j < n_a)
    def _():
        o_ref[...] = jnp.dot(xn_ref[...], wa_ref[...], preferred_element_type=jnp.float32)

    @pl.when(j >= n_a)
    def _():
        o_ref[...] = jnp.dot(xn_ref[...], wb_ref[...], preferred_element_type=jnp.float32)


def _norm_inproj(hp, g, w_a, w_b, wdt, dtb, alog):
    tp, d = hp.shape
    n = w_a.shape[1] + w_b.shape[1]
    tm = _pick(tp, (1024, 512, 256, 128))
    tn = math.gcd(_pick(w_a.shape[1], (1024, 512, 256, 128)), _pick(w_b.shape[1], (1024, 512, 256, 128)))
    n_a = w_a.shape[1] // tn
    vec = pl.BlockSpec((1, LANES), lambda i, j: (0, 0))
    return pl.pallas_call(
        functools.partial(_norm_inproj_kernel, n_a),
        grid=(tp // tm, n // tn),
        in_specs=[
            pl.BlockSpec((tm, d), lambda i, j: (i, 0)),
            pl.BlockSpec((1, d), lambda i, j: (0, 0)),
            pl.BlockSpec((d, tn), lambda i, j: (0, jnp.minimum(j, n_a - 1))),
            pl.BlockSpec((d, tn), lambda i, j: (0, jnp.maximum(j - n_a, 0))),
            pl.BlockSpec((d, LANES), lambda i, j: (0, 0)),
            vec, vec,
        ],
        out_specs=[
            pl.BlockSpec((tm, tn), lambda i, j: (i, j)),
            pl.BlockSpec((tm, LANES), lambda i, j: (i, 0)),
            pl.BlockSpec((tm, LANES), lambda i, j: (i, 0)),
        ],
        out_shape=[
            jax.ShapeDtypeStruct((tp, n), jnp.float32),
            jax.ShapeDtypeStruct((tp, LANES), jnp.float32),
            jax.ShapeDtypeStruct((tp, LANES), jnp.float32),
        ],
        scratch_shapes=[pltpu.VMEM((tm, d), MXU_DTYPE)],
        compiler_params=_params(("parallel", "arbitrary")),
        name="norm_inproj",
    )(hp, g, w_a, w_b, wdt, dtb, alog)


def _ssd_kernel(n_pad, z_ref, xs_ref, b_ref, c_ref, dt_ref, acum_ref,
                cwx_ref, cwb_ref, cwc_ref, cbx_ref, cbb_ref, cbc_ref,
                d_ref, nw_ref, o_ref,
                xs_s, b_s, c_s, state_s):
    g = pl.program_id(1)
    lp = xs_ref.shape[1]
    q = SSD_CHUNK
    n_chunks = lp // q
    gw = xs_ref.shape[2]
    hpg = gw // SSD_HEAD_DIM
    f32 = jnp.float32

    def conv_into(c, slot):
        xs_s[slot] = _silu(_causal_conv_chunk(xs_ref, cwx_ref, cbx_ref, c))
        b_s[slot] = _silu(_causal_conv_chunk(b_ref, cwb_ref, cbb_ref, c))
        c_s[slot] = _silu(_causal_conv_chunk(c_ref, cwc_ref, cbc_ref, c))

    rows = lax.broadcasted_iota(jnp.int32, (LANES, gw), 0)
    cols = lax.broadcasted_iota(jnp.int32, (LANES, gw), 1)
    expand = (rows == g * hpg + lax.shift_right_logical(cols, SSD_HEAD_SHIFT)).astype(f32)

    state_s[...] = jnp.zeros_like(state_s)
    li = lax.broadcasted_iota(jnp.int32, (q, q), 0)
    si = lax.broadcasted_iota(jnp.int32, (q, q), 1)
    causal = li >= si
    lane = lax.broadcasted_iota(jnp.int32, (q, gw), 1)
    row = lax.broadcasted_iota(jnp.int32, (q, gw), 0)

    conv_into(0, 0)

    def chunk(c, carry):
        r0 = pl.multiple_of(c * q, q)
        sl = pl.ds(r0, q)
        slot = c % 2
        xs = xs_s[slot]
        bm = b_s[slot]
        cm = c_s[slot]
        conv_into(jnp.minimum(c + 1, n_chunks - 1), 1 - slot)
        dt_g = _dot_exact_rhs(dt_ref[0, sl, :], expand)
        acum = _dot_exact_rhs(acum_ref[0, sl, :], expand)
        xdt = jnp.where(row + r0 >= n_pad, xs * dt_g, 0.0)
        acum_t = acum.T
        a_last = acum[q - 1:q, :]
        state = state_s[...]

        cb = lax.dot_general(cm.astype(MXU_DTYPE), bm.astype(MXU_DTYPE),
                             (((1,), (1,)), ((), ())), preferred_element_type=f32)
        y = _dot(cm, state) * jnp.exp(acum)
        for j in range(hpg):
            c0 = j * SSD_HEAD_DIM
            seg = acum[:, c0:c0 + 1] - acum_t[c0:c0 + 1, :]
            decay = jnp.exp(jnp.where(causal, seg, -jnp.inf))
            in_head = (lane >= c0) & (lane < c0 + SSD_HEAD_DIM)
            y = y + _dot(cb * decay, jnp.where(in_head, xdt, 0.0))
        y = y + d_ref[...] * xs

        state_s[...] = state * jnp.exp(a_last) + _dot(bm.T, xdt * jnp.exp(a_last - acum))

        yz = y * _silu(z_ref[0, sl, :])
        yn = yz * lax.rsqrt(jnp.mean(yz * yz, axis=-1, keepdims=True) + EPS)
        o_ref[0, sl, :] = (yn * nw_ref[...]).astype(o_ref.dtype)
        return carry

    lax.fori_loop(0, lp // q, chunk, 0)


def _ssd(proj3, dt3, acum3, n_pad, cw, cb, d_skip, norm_w, d_inner):
    b, lp, _ = proj3.shape
    gw = d_inner // SSD_GROUPS
    ns = SSD_STATE
    q = SSD_CHUNK
    xs_blk0 = d_inner // gw
    b_blk0 = 2 * d_inner // ns
    c_blk0 = (2 * d_inner + SSD_GROUPS * ns) // ns
    f32 = jnp.float32

    cwx = cw[:, :d_inner]
    cwb = cw[:, d_inner:d_inner + SSD_GROUPS * ns]
    cwc = cw[:, d_inner + SSD_GROUPS * ns:]
    cbx = cb[None, :d_inner]
    cbb = cb[None, d_inner:d_inner + SSD_GROUPS * ns]
    cbc = cb[None, d_inner + SSD_GROUPS * ns:]
    d_ch = jnp.repeat(d_skip, SSD_HEAD_DIM)[None, :]
    nw = norm_w[None, :]

    seq = lambda w, off: pl.BlockSpec((1, lp, w), lambda i, g: (i, 0, off + g))
    head = pl.BlockSpec((1, lp, LANES), lambda i, g: (i, 0, 0))
    vec = lambda w: pl.BlockSpec((1, w), lambda i, g: (0, g))
    cwspec = lambda w: pl.BlockSpec((CONV_K, w), lambda i, g: (0, g))
    return pl.pallas_call(
        functools.partial(_ssd_kernel, n_pad),
        grid=(b, SSD_GROUPS),
        in_specs=[
            seq(gw, 0), seq(gw, xs_blk0), seq(ns, b_blk0), seq(ns, c_blk0), head, head,
            cwspec(gw), cwspec(ns), cwspec(ns), vec(gw), vec(ns), vec(ns),
            vec(gw), vec(gw),
        ],
        out_specs=pl.BlockSpec((1, lp, gw), lambda i, g: (i, 0, g)),
        out_shape=jax.ShapeDtypeStruct((b, lp, d_inner), MXU_DTYPE),
        scratch_shapes=[
            pltpu.VMEM((2, q, gw), f32), pltpu.VMEM((2, q, ns), f32), pltpu.VMEM((2, q, ns), f32),
            pltpu.VMEM((ns, gw), f32),
        ],
        compiler_params=_params(("parallel", "parallel")),
        name="ssd",
    )(proj3, proj3, proj3, proj3, dt3, acum3, cwx, cwb, cwc, cbx, cbb, cbc, d_ch, nw)


def _lru_kernel(n_pad, lx_ref, ly_ref, cw_ref, cb_ref, wa_ref, ba_ref, wx_ref, bx_ref, lam_ref, o_ref):
    lp = lx_ref.shape[1]
    w = lx_ref.shape[2]
    q = SSD_CHUNK
    neg_c_softplus = (-LRU_C) * _softplus(-lam_ref[...])
    row = lax.broadcasted_iota(jnp.int32, (q, w), 0)
    sub = lax.broadcasted_iota(jnp.int32, (SUBLANES, w), 0)

    def chunk(c, h_prev):
        r0 = pl.multiple_of(c * q, q)
        sl = pl.ds(r0, q)
        xr = _causal_conv_chunk(lx_ref, cw_ref, cb_ref, c)
        gate_r = _sigmoid(_dot(xr, wa_ref[0]) + ba_ref[...])
        gate_i = _sigmoid(_dot(xr, wx_ref[0]) + bx_ref[...])
        log_a = gate_r * neg_c_softplus
        a = jnp.exp(log_a)
        mult = jnp.sqrt(jnp.tanh(-log_a) * (a * a + 1.0))
        u = jnp.where(row + r0 >= n_pad, mult * gate_i * xr, 0.0)
        gate_y = jax.nn.gelu(ly_ref[0, sl, :])

        out = []
        for t in range(q // SUBLANES):
            rows8 = slice(t * SUBLANES, (t + 1) * SUBLANES)
            at, ut = a[rows8], u[rows8]
            for d in (1, 2, 4):
                keep = sub >= d
                ut = jnp.where(keep, at * pltpu.roll(ut, d, 0) + ut, ut)
                at = jnp.where(keep, at * pltpu.roll(at, d, 0), at)
            h = at * h_prev + ut
            h_prev = jnp.broadcast_to(h[SUBLANES - 1:SUBLANES, :], (SUBLANES, w))
            out.append(h * gate_y[rows8])
        o_ref[0, sl, :] = jnp.concatenate(out, axis=0).astype(o_ref.dtype)
        return h_prev

    lax.fori_loop(0, lp // q, chunk, jnp.zeros((SUBLANES, w), jnp.float32))


def _lru(proj3, n_pad, lx_col, ly_col, cw, cb, wa, ba, wx, bx, lam, width):
    b, lp, _ = proj3.shape
    w = width // LRU_HEADS
    seq = lambda off: pl.BlockSpec((1, lp, w), lambda i, h: (i, 0, off + h))
    vec = pl.BlockSpec((1, w), lambda i, h: (0, h))
    mat = pl.BlockSpec((1, w, w), lambda i, h: (h, 0, 0))
    return pl.pallas_call(
        functools.partial(_lru_kernel, n_pad),
        grid=(b, LRU_HEADS),
        in_specs=[seq(lx_col // w), seq(ly_col // w),
                  pl.BlockSpec((CONV_K, w), lambda i, h: (0, h)), vec, mat, vec, mat, vec, vec],
        out_specs=pl.BlockSpec((1, lp, w), lambda i, h: (i, 0, h)),
        out_shape=jax.ShapeDtypeStruct((b, lp, width), MXU_DTYPE),
        compiler_params=_params(("parallel", "parallel")),
        name="rglru",
    )(proj3, proj3, cw, cb[None, :], wa, ba[None, :], wx, bx[None, :], lam[None, :])


def _merge_kernel(ys_ref, yl_ref, ws_ref, wl_ref, g0_ref, g1_ref, gb_ref, o_ref):
    y_ssd = jnp.dot(ys_ref[...], ws_ref[...], preferred_element_type=jnp.float32)
    y_lru = jnp.dot(yl_ref[...], wl_ref[...], preferred_element_type=jnp.float32)
    gate0 = _sigmoid(g0_ref[...] + gb_ref[0:1, :])
    gate1 = _sigmoid(g1_ref[...] + gb_ref[1:2, :])
    o_ref[...] = (gate0 * y_ssd + gate1 * y_lru).astype(o_ref.dtype)


def _merge(ys, yl, ws, wl, proj, gate_col, gate_bias):
    tp, d = ys.shape
    tm = _pick(tp, (1024, 512, 256, 128))
    tn = _pick(d, (512, 256, 128))
    g0 = gate_col // tn
    g1 = (gate_col + d) // tn
    return pl.pallas_call(
        _merge_kernel,
        grid=(tp // tm, d // tn),
        in_specs=[
            pl.BlockSpec((tm, d), lambda i, j: (i, 0)),
            pl.BlockSpec((tm, d), lambda i, j: (i, 0)),
            pl.BlockSpec((d, tn), lambda i, j: (0, j)),
            pl.BlockSpec((d, tn), lambda i, j: (0, j)),
            pl.BlockSpec((tm, tn), lambda i, j: (i, g0 + j)),
            pl.BlockSpec((tm, tn), lambda i, j: (i, g1 + j)),
            pl.BlockSpec((N_BRANCH, tn), lambda i, j: (0, j)),
        ],
        out_specs=pl.BlockSpec((tm, tn), lambda i, j: (i, j)),
        out_shape=jax.ShapeDtypeStruct((tp, d), MXU_DTYPE),
        compiler_params=_params(("parallel", "arbitrary")),
        name="merge",
    )(ys, yl, ws, wl, proj, proj, gate_bias)


def _first_index_of_max(p, valid, lane):
    pm = jnp.where(valid, p, -jnp.inf)
    top = jnp.max(pm, axis=-1, keepdims=True)
    idx = jnp.min(jnp.where(valid & (pm == top), lane, LANES), axis=-1, keepdims=True)
    return top, idx


def _masked_softmax(x, valid):
    m = jnp.max(jnp.where(valid, x, -jnp.inf), axis=-1, keepdims=True)
    e = jnp.where(valid, jnp.exp(x - m), 0.0)
    return e / jnp.sum(e, axis=-1, keepdims=True)


def _outproj_router_kernel(mix_ref, h_ref, wo_ref, g_ref, wr_ref, real_ref,
                           h2_ref, u_ref, eid_ref, gate_ref, rank_ref, counts_ref, counts_s):
    @pl.when(pl.program_id(0) == 0)
    def _():
        counts_s[...] = jnp.zeros_like(counts_s)

    h2 = h_ref[...] + jnp.dot(mix_ref[...], wo_ref[...], preferred_element_type=jnp.float32)
    h2_ref[...] = h2
    u = h2 * lax.rsqrt(jnp.mean(h2 * h2, axis=-1, keepdims=True) + EPS) * g_ref[...]
    u_ref[...] = u
    logits = _dot(u, wr_ref[...])
    lane = lax.broadcasted_iota(jnp.int32, logits.shape, 1)

    g_prob = _masked_softmax(logits, lane < MOE_GROUPS)
    g_p, g_idx = _first_index_of_max(g_prob, lane < MOE_GROUPS, lane)

    e_lo = MOE_GROUPS + g_idx * MOE_EXP_PER_GROUP
    in_group = (lane >= e_lo) & (lane < e_lo + MOE_EXP_PER_GROUP)
    e_prob = _masked_softmax(logits, in_group)
    p1, i1 = _first_index_of_max(e_prob, in_group, lane)
    rest = in_group & (lane != i1)
    p2, i2 = _first_index_of_max(e_prob, rest, lane)
    denom = p1 + p2

    col = lax.broadcasted_iota(jnp.int32, eid_ref.shape, 1)
    e1 = i1 - MOE_GROUPS
    e2 = i2 - MOE_GROUPS
    eid_ref[...] = jnp.where(col == 0, e1, e2)
    gate_ref[...] = jnp.where(col == 0, g_p * p1 / denom, g_p * p2 / denom)

    m = logits.shape[0]
    hit1 = (lane == e1) & (real_ref[...] > 0.0)
    hit2 = (lane == e2) & (real_ref[...] > 0.0)
    onehot = jnp.where(hit1 | hit2, 1.0, 0.0)
    earlier = (lax.broadcasted_iota(jnp.int32, (m, m), 0) > lax.broadcasted_iota(jnp.int32, (m, m), 1))
    before = counts_s[...] + jnp.dot(earlier.astype(jnp.bfloat16), onehot.astype(jnp.bfloat16),
                                     preferred_element_type=jnp.float32)
    rank1 = jnp.sum(jnp.where(hit1, before, 0.0), axis=-1, keepdims=True)
    rank2 = jnp.sum(jnp.where(hit2, before, 0.0), axis=-1, keepdims=True)
    rank_ref[...] = jnp.where(col == 0, rank1, rank2).astype(jnp.int32)
    counts_s[...] += jnp.sum(onehot, axis=0, keepdims=True)
    counts_ref[...] = counts_s[...].astype(jnp.int32)


def _outproj_router(mixed, hp, wo, g_ffn, wr, real_rows):
    tp, d = hp.shape
    tm = _pick(tp, (512, 256, 128))
    row = lambda w: pl.BlockSpec((tm, w), lambda i: (i, 0))
    return pl.pallas_call(
        _outproj_router_kernel,
        grid=(tp // tm,),
        in_specs=[row(d), row(d),
                  pl.BlockSpec((d, d), lambda i: (0, 0)),
                  pl.BlockSpec((1, d), lambda i: (0, 0)),
                  pl.BlockSpec((d, LANES), lambda i: (0, 0)),
                  row(1)],
        out_specs=[row(d), row(d), row(MOE_TOPK), row(MOE_TOPK), row(MOE_TOPK),
                   pl.BlockSpec((1, LANES), lambda i: (0, 0))],
        out_shape=[jax.ShapeDtypeStruct((tp, d), jnp.float32),
                   jax.ShapeDtypeStruct((tp, d), jnp.float32),
                   jax.ShapeDtypeStruct((tp, MOE_TOPK), jnp.int32),
                   jax.ShapeDtypeStruct((tp, MOE_TOPK), jnp.float32),
                   jax.ShapeDtypeStruct((tp, MOE_TOPK), jnp.int32),
                   jax.ShapeDtypeStruct((1, LANES), jnp.int32)],
        scratch_shapes=[pltpu.VMEM((1, LANES), jnp.float32)],
        compiler_params=_params(("arbitrary",)),
        name="outproj_router",
    )(mixed, hp, wo, g_ffn, wr, real_rows)


def _wait_rows(n, make_copy):
    for k in range(EXPERT_ROWS.bit_length() - 1, -1, -1):
        @pl.when((n & (1 << k)) != 0)
        def _():
            make_copy(1 << k).wait()


def _expert_kernel(n_items_ref, n_used_ref, item_e_ref, item_start_ref, item_rows_ref, item_real_ref, tok_ref,
                   u_hbm, w1_ref, w3_ref, w2_ref, y_hbm,
                   xg_s, xb_s, acc_s, sem):
    i = pl.program_id(0)
    f = pl.program_id(1)
    nf = pl.num_programs(1)
    n_items = n_items_ref[0]
    active = i < n_items
    n_rows = item_rows_ref[i]
    d = acc_s.shape[1]
    blk = MOE_BLOCK

    def issue_gather(item):
        start = item_start_ref[item]
        n = item_real_ref[item]
        n_groups = n // GATHER_UNROLL

        def one(r):
            pltpu.make_async_copy(u_hbm.at[pl.ds(tok_ref[start + r], 1)], xg_s.at[pl.ds(r, 1)], sem.at[0]).start()

        def group(g, c):
            for j in range(GATHER_UNROLL):
                one(g * GATHER_UNROLL + j)
            return c
        lax.fori_loop(0, n_groups, group, 0)

        def single(r, c):
            one(r)
            return c
        lax.fori_loop(n_groups * GATHER_UNROLL, n, single, 0)

    def wait_gather(item):
        _wait_rows(item_real_ref[item],
                   lambda m: pltpu.make_async_copy(u_hbm.at[pl.ds(0, m)], xg_s.at[pl.ds(0, m)], sem.at[0]))

    def out_copy(item, j):
        r0 = pl.multiple_of(j * blk, blk)
        dst0 = pl.multiple_of(item_start_ref[item] + r0, blk)
        return pltpu.make_async_copy(acc_s.at[pl.ds(r0, blk)], y_hbm.at[pl.ds(dst0, blk)], sem.at[1])

    def issue_out(item):
        def body(j, c):
            out_copy(item, j).start()
            return c
        lax.fori_loop(0, item_rows_ref[item] // blk, body, 0)

    def wait_out(item):
        def body(j, c):
            out_copy(item, j).wait()
            return c
        lax.fori_loop(0, item_rows_ref[item] // blk, body, 0)

    @pl.when((i == 0) & (f == 0))
    def _():
        xg_s[...] = jnp.zeros_like(xg_s)
        acc_s[pl.ds(0, blk), :] = jnp.zeros((blk, d), acc_s.dtype)
        n_used = n_used_ref[0]
        n_blocks = y_hbm.shape[0] // blk

        def tail_copy(j):
            return pltpu.make_async_copy(acc_s.at[pl.ds(0, blk)], y_hbm.at[pl.ds(pl.multiple_of(j * blk, blk), blk)],
                                         sem.at[1])

        def fill(j, c):
            tail_copy(j).start()
            return c
        lax.fori_loop(n_used, n_blocks, fill, 0)

        def drain(j, c):
            tail_copy(j).wait()
            return c
        lax.fori_loop(n_used, n_blocks, drain, 0)

        @pl.when(active)
        def _():
            issue_gather(0)

    @pl.when(active & (f == 0))
    def _():
        wait_gather(i)
        for lo, m in zip((0,) + EXPERT_ROW_VARIANTS, EXPERT_ROW_VARIANTS):
            @pl.when((n_rows > lo) & (n_rows <= m))
            def _():
                xb_s[pl.ds(0, m), :] = xg_s[pl.ds(0, m), :].astype(xb_s.dtype)

        @pl.when(i + 1 < n_items)
        def _():
            issue_gather(i + 1)

    @pl.when((f == 0) & (i >= 1) & (i - 1 < n_items) & jnp.logical_not(active))
    def _():
        wait_out(i - 1)

    for lo, m in zip((0,) + EXPERT_ROW_VARIANTS, EXPERT_ROW_VARIANTS):
        @pl.when(active & (n_rows > lo) & (n_rows <= m))
        def _():
            rows = pl.ds(0, m)
            x = xb_s[rows, :]
            a = jnp.dot(x, w1_ref[0].astype(MXU_DTYPE), preferred_element_type=jnp.float32)
            b = jnp.dot(x, w3_ref[0].astype(MXU_DTYPE), preferred_element_type=jnp.float32)
            hdn = (_silu(a) * b).astype(MXU_DTYPE)

            @pl.when((f == 0) & (i >= 1))
            def _():
                wait_out(i - 1)

            for c0 in range(0, d, EXPERT_OUT_TILE):
                cols = pl.ds(c0, EXPERT_OUT_TILE)
                part = jnp.dot(hdn, w2_ref[0, :, cols].astype(MXU_DTYPE), preferred_element_type=jnp.float32)

                @pl.when(f == 0)
                def _():
                    acc_s[rows, cols] = part

                @pl.when(f > 0)
                def _():
                    acc_s[rows, cols] += part

    @pl.when(f == nf - 1)
    def _():
        @pl.when(active)
        def _():
            issue_out(i)

        @pl.when(active & (i == pl.num_programs(0) - 1))
        def _():
            wait_out(i)


def _experts(u, sched, w1, w3, w2, n_items_max, p_len):
    tp, d = u.shape
    ff = w1.shape[2]
    tf = EXPERT_FF_TILE
    nf = ff // tf
    n_items, n_used, item_e, item_start, item_rows, item_real, tok_sorted = sched

    def w_in_map(i, f, n_items, n_used, item_e, *_):
        return (item_e[i], 0, jnp.where(i < n_items[0], f, nf - 1))

    def w_out_map(i, f, n_items, n_used, item_e, *_):
        return (item_e[i], jnp.where(i < n_items[0], f, nf - 1), 0)

    grid_spec = pltpu.PrefetchScalarGridSpec(
        num_scalar_prefetch=7,
        grid=(n_items_max, nf),
        in_specs=[
            pl.BlockSpec(memory_space=pl.ANY),
            pl.BlockSpec((1, d, tf), w_in_map),
            pl.BlockSpec((1, d, tf), w_in_map),
            pl.BlockSpec((1, tf, d), w_out_map),
        ],
        out_specs=pl.BlockSpec(memory_space=pl.ANY),
        scratch_shapes=[
            pltpu.VMEM((EXPERT_ROWS, d), jnp.float32),
            pltpu.VMEM((EXPERT_ROWS, d), MXU_DTYPE),
            pltpu.VMEM((EXPERT_ROWS, d), jnp.float32),
            pltpu.SemaphoreType.DMA((2,)),
        ],
    )
    return pl.pallas_call(
        _expert_kernel,
        grid_spec=grid_spec,
        out_shape=jax.ShapeDtypeStruct((p_len, d), jnp.float32),
        compiler_params=_params(("arbitrary", "arbitrary")),
        name="experts",
    )(n_items, n_used, item_e, item_start, item_rows, item_real, tok_sorted, u, w1, w3, w2)


def _expert_schedule(eid, rank, counts, batch, lp, n_pad, n_items_max, p_len):
    i32 = jnp.int32
    padded = (counts + MOE_BLOCK - 1) // MOE_BLOCK * MOE_BLOCK
    pad_end = jnp.cumsum(padded)
    pad_start = pad_end - padded
    real = lambda a: a.reshape(batch, lp, MOE_TOPK)[:, n_pad:]
    dest = real(pad_start[eid] + rank)
    tok = jnp.broadcast_to(jnp.arange(batch * lp, dtype=i32)[:, None], (batch * lp, MOE_TOPK))
    tok_sorted = jnp.zeros((p_len,), i32).at[dest.reshape(-1)].set(real(tok).reshape(-1))
    n_used = (pad_end[-1:] // MOE_BLOCK).astype(i32)

    chunks = (padded + EXPERT_ROWS - 1) // EXPERT_ROWS
    chunk_end = jnp.cumsum(chunks)
    item = jnp.arange(n_items_max, dtype=i32)
    item_e = jnp.minimum(jnp.sum((chunk_end[None, :] <= item[:, None]).astype(i32), axis=1), MOE_EXPERTS - 1)
    n_items = chunk_end[-1:].astype(i32)
    last_e = item_e[jnp.maximum(n_items[0] - 1, 0)]
    item_e = jnp.where(item < n_items[0], item_e, last_e)
    k = item - (chunk_end - chunks)[item_e]
    item_start = (pad_start[item_e] + k * EXPERT_ROWS).astype(i32)
    item_rows = jnp.clip(padded[item_e] - k * EXPERT_ROWS, 0, EXPERT_ROWS).astype(i32)
    item_real = jnp.clip(counts[item_e] - k * EXPERT_ROWS, 0, EXPERT_ROWS).astype(i32)
    live = item < n_items[0]
    item_start = jnp.where(live, item_start, 0)
    item_rows = jnp.where(live, item_rows, 0)
    item_real = jnp.where(live, item_real, 0)
    return (n_items, n_used, item_e, item_start, item_rows, item_real, tok_sorted), dest


def _final_kernel(tm, dest_ref, h_ref, gate_ref, g_ref, ys_hbm, o_ref, ybuf, sem):
    t = pl.program_id(0)
    n_tiles = pl.num_programs(0)

    def issue(tile, buf):
        base = tile * (tm * MOE_TOPK)
        for j in range(tm):
            for k in range(MOE_TOPK):
                pos = dest_ref[base + j * MOE_TOPK + k]
                pltpu.make_async_copy(ys_hbm.at[pl.ds(pos, 1)], ybuf.at[buf, k, pl.ds(j, 1)], sem.at[buf]).start()

    @pl.when(t == 0)
    def _():
        issue(0, 0)

    @pl.when(t + 1 < n_tiles)
    def _():
        issue(t + 1, (t + 1) % 2)

    buf = t % 2
    for k in range(MOE_TOPK):
        pltpu.make_async_copy(ys_hbm.at[pl.ds(0, tm)], ybuf.at[buf, k], sem.at[buf]).wait()
    moe = gate_ref[:, 0:1] * ybuf[buf, 0] + gate_ref[:, 1:2] * ybuf[buf, 1]
    h = h_ref[...] + moe
    o_ref[...] = h * lax.rsqrt(jnp.mean(h * h, axis=-1, keepdims=True) + EPS) * g_ref[...]


def _final(h2, ys, dest_x, gates, g_final, batch, lp, seq):
    tp, d = h2.shape
    tm = SSD_CHUNK
    skip = (lp - seq) // tm
    per_b = seq // tm
    src = lambda w: pl.BlockSpec((tm, w), lambda t, dest: ((t // per_b) * (lp // tm) + skip + t % per_b, 0))
    grid_spec = pltpu.PrefetchScalarGridSpec(
        num_scalar_prefetch=1,
        grid=(batch * per_b,),
        in_specs=[src(d), src(MOE_TOPK), pl.BlockSpec((1, d), lambda t, dest: (0, 0)),
                  pl.BlockSpec(memory_space=pl.ANY)],
        out_specs=pl.BlockSpec((tm, d), lambda t, dest: (t, 0)),
        scratch_shapes=[pltpu.VMEM((2, MOE_TOPK, tm, d), jnp.float32), pltpu.SemaphoreType.DMA((2,))],
    )
    return pl.pallas_call(
        functools.partial(_final_kernel, tm),
        grid_spec=grid_spec,
        out_shape=jax.ShapeDtypeStruct((batch * seq, d), jnp.float32),
        compiler_params=_params(("arbitrary",)),
        name="final_norm",
    )(dest_x, h2, gates, g_final, ys)


def kernel(x, meta_tokens, norm_mix, w_in, ssd_conv_w, ssd_conv_b, ssd_dt_bias, ssd_a_log, ssd_d, ssd_norm, w_ssd_out, lru_conv_w, lru_conv_b, lru_wa, lru_ba, lru_wx, lru_bx, lru_lambda, w_lru_out, gate_bias, w_out, norm_ffn, w_router_group, w_router_expert, w_exp_gate, w_exp_up, w_exp_down, norm_final):
    batch, seq, d = x.shape
    depth = norm_mix.shape[0]
    assert depth == 1 and seq % SSD_CHUNK == 0
    l = N_META + seq
    lp = -(-l // SSD_CHUNK) * SSD_CHUNK
    n_pad = lp - l
    d_inner = d
    heads = d_inner // SSD_HEAD_DIM
    conv_dim = d_inner + 2 * SSD_GROUPS * SSD_STATE
    col_xbc = d_inner
    col_dt = col_xbc + conv_dim
    col_lx = col_dt + heads
    col_ly = col_lx + d
    col_gl = col_ly + d
    f32 = jnp.float32
    lyr = 0

    meta = jnp.broadcast_to(meta_tokens.astype(f32)[None], (batch, N_META, d))
    hp = jnp.concatenate([jnp.zeros((batch, n_pad, d), f32), meta, x], axis=1).reshape(batch * lp, d)

    w = w_in[lyr]
    w_a = w[:, :col_dt].astype(MXU_DTYPE)
    w_b = w[:, col_lx:].astype(MXU_DTYPE)
    w_dt = jnp.zeros((d, LANES), f32).at[:, :heads].set(w[:, col_dt:col_lx]).astype(MXU_DTYPE)
    dtb = jnp.zeros((1, LANES), f32).at[0, :heads].set(ssd_dt_bias[lyr])
    alog = jnp.zeros((1, LANES), f32).at[0, :heads].set(ssd_a_log[lyr])
    proj, dt, acum = _norm_inproj(hp, norm_mix[lyr][None, :], w_a, w_b, w_dt, dtb, alog)
    n_main = w_a.shape[1] + w_b.shape[1]
    proj3 = proj.reshape(batch, lp, n_main)
    lx_col = col_dt
    ly_col = lx_col + d
    gl_col = ly_col + d

    ys = _ssd(proj3, dt.reshape(batch, lp, LANES), acum.reshape(batch, lp, LANES), n_pad, ssd_conv_w[lyr],
              ssd_conv_b[lyr], ssd_d[lyr], ssd_norm[lyr], d_inner)
    yl = _lru(proj3, n_pad, lx_col, ly_col, lru_conv_w[lyr], lru_conv_b[lyr], lru_wa[lyr].astype(MXU_DTYPE),
              lru_ba[lyr], lru_wx[lyr].astype(MXU_DTYPE), lru_bx[lyr], lru_lambda[lyr], d)

    mixed = _merge(ys.reshape(batch * lp, d), yl.reshape(batch * lp, d), w_ssd_out[lyr].astype(MXU_DTYPE),
                   w_lru_out[lyr].astype(MXU_DTYPE), proj, gl_col, gate_bias[lyr])

    w_router = jnp.zeros((d, LANES), f32)
    w_router = w_router.at[:, :MOE_GROUPS].set(w_router_group[lyr])
    w_router = w_router.at[:, MOE_GROUPS:MOE_GROUPS + MOE_EXPERTS].set(w_router_expert[lyr])
    real_rows = jnp.asarray(((np.arange(batch * lp) % lp) >= n_pad).astype(np.float32)[:, None])
    h2, u2, eid, gates, rank, counts = _outproj_router(mixed, hp, w_out[lyr].astype(MXU_DTYPE),
                                                       norm_ffn[lyr][None, :], w_router.astype(MXU_DTYPE), real_rows)

    n_asg = batch * l * MOE_TOPK
    p_max = n_asg + MOE_EXPERTS * (MOE_BLOCK - 1)
    n_items_max = MOE_EXPERTS + p_max // EXPERT_ROWS
    p_len = -(-p_max // MOE_BLOCK) * MOE_BLOCK
    sched, dest = _expert_schedule(eid, rank, counts[0, :MOE_EXPERTS], batch, lp, n_pad, n_items_max, p_len)
    ff = w_exp_gate.shape[-1]
    ys_sorted = _experts(u2, sched, w_exp_gate.reshape(MOE_EXPERTS, d, ff), w_exp_up.reshape(MOE_EXPERTS, d, ff),
                         w_exp_down.reshape(MOE_EXPERTS, ff, d), n_items_max, p_len)

    dest_x = dest[:, N_META:].reshape(-1)
    out = _final(h2, ys_sorted, dest_x, gates, norm_final[None, :], batch, lp, seq)
    return out.reshape(batch, seq, d)
```

```python
import functools
import math

import numpy as np
import jax
import jax.numpy as jnp
from jax import lax
from jax.experimental import pallas as pl
from jax.experimental.pallas import tpu as pltpu

N_META = 16
CONV_K = 4
EPS = 1e-6
SSD_HEAD_DIM = 64
SSD_HEAD_SHIFT = 6
SSD_GROUPS = 8
SSD_STATE = 128
SSD_CHUNK = 128
LRU_HEADS = 8
LRU_C = 8.0
N_BRANCH = 2
MOE_GROUPS = 8
MOE_EXP_PER_GROUP = 8
MOE_EXPERTS = MOE_GROUPS * MOE_EXP_PER_GROUP
MOE_TOPK = 2
MOE_BLOCK = 128

LANES = 128
SUBLANES = 8
VMEM_LIMIT = 56 * 1024 * 1024
MXU_DTYPE = jnp.bfloat16

EXPERT_ROWS = 1024
EXPERT_ROW_VARIANTS = (256, 512, 640, 768, 1024)
EXPERT_FF_TILE = 512
EXPERT_OUT_TILE = 1024
GATHER_UNROLL = 8
SSD_UNROLL = 2


def _pick(n, options):
    for o in options:
        if n % o == 0:
            return o
    raise ValueError(f"no tile in {options} divides {n}")


def _params(sem, vmem=VMEM_LIMIT):
    return pltpu.CompilerParams(dimension_semantics=sem, vmem_limit_bytes=vmem)


def _dot(a, b):
    return jnp.dot(a.astype(MXU_DTYPE), b.astype(MXU_DTYPE), preferred_element_type=jnp.float32)


def _dot_exact_rhs(v, sel):
    sel = sel.astype(jnp.bfloat16)
    hi = v.astype(jnp.bfloat16)
    r1 = v - hi.astype(jnp.float32)
    mid = r1.astype(jnp.bfloat16)
    lo = (r1 - mid.astype(jnp.float32)).astype(jnp.bfloat16)
    f32 = jnp.float32
    return (jnp.dot(hi, sel, preferred_element_type=f32) + jnp.dot(mid, sel, preferred_element_type=f32)
            + jnp.dot(lo, sel, preferred_element_type=f32))


def _dot_exact_lhs(sel, v):
    sel = sel.astype(jnp.bfloat16)
    hi = v.astype(jnp.bfloat16)
    r1 = v - hi.astype(jnp.float32)
    mid = r1.astype(jnp.bfloat16)
    lo = (r1 - mid.astype(jnp.float32)).astype(jnp.bfloat16)
    f32 = jnp.float32
    return (jnp.dot(sel, hi, preferred_element_type=f32) + jnp.dot(sel, mid, preferred_element_type=f32)
            + jnp.dot(sel, lo, preferred_element_type=f32))


def _softplus(x):
    return jnp.maximum(x, 0.0) + jnp.log1p(jnp.exp(-jnp.abs(x)))


def _sigmoid(x):
    return 1.0 / (1.0 + jnp.exp(-x))


def _silu(x):
    return x * _sigmoid(x)


def _causal_conv_chunk(x_ref, w_ref, b_ref, c):
    q = SSD_CHUNK
    r0 = pl.multiple_of(c * q, q)
    cur = x_ref[0, pl.ds(r0, q), :]
    prev = x_ref[0, pl.ds(pl.multiple_of(jnp.maximum(r0 - SUBLANES, 0), SUBLANES), SUBLANES), :]
    prev = jnp.where(c > 0, prev, 0.0)
    sub = lax.broadcasted_iota(jnp.int32, prev.shape, 0)
    acc = b_ref[...] + w_ref[CONV_K - 1:CONV_K, :] * cur
    for k in range(1, CONV_K):
        rolled = pltpu.roll(cur, k, 0)
        head = jnp.where(sub < k, pltpu.roll(prev, k, 0), rolled[:SUBLANES])
        shifted = jnp.concatenate([head, rolled[SUBLANES:]], axis=0)
        acc = acc + w_ref[CONV_K - 1 - k:CONV_K - k, :] * shifted
    return acc


def _norm_inproj_kernel(n_a, x_ref, g_ref, wa_ref, wb_ref, wdt_ref, dtb_ref, alog_ref, o_ref, dt_ref, acum_ref, xn_ref):
    j = pl.program_id(1)

    @pl.when(j == 0)
    def _():
        x = x_ref[...]
        y = x * lax.rsqrt(jnp.mean(x * x, axis=-1, keepdims=True) + EPS)
        xn_ref[...] = (y * g_ref[...]).astype(xn_ref.dtype)
        raw = jnp.dot(xn_ref[...], wdt_ref[...], preferred_element_type=jnp.float32)
        dt = _softplus(raw + dtb_ref[...])
        dt_ref[...] = dt
        adt = dt * (-jnp.exp(alog_ref[...]))
        q = SSD_CHUNK
        tri = (lax.broadcasted_iota(jnp.int32, (q, q), 0) >= lax.broadcasted_iota(jnp.int32, (q, q), 1))
        for c in range(x.shape[0] // q):
            acum_ref[c * q:(c + 1) * q, :] = _dot_exact_lhs(tri.astype(jnp.float32), adt[c * q:(c + 1) * q, :])

    @pl.when(j < n_a)
    def _():
        o_ref[...] = jnp.dot(xn_ref[...], wa_ref[...], preferred_element_type=jnp.float32)

    @pl.when(j >= n_a)
    def _():
        o_ref[...] = jnp.dot(xn_ref[...], wb_ref[...], preferred_element_type=jnp.float32)


def _norm_inproj(hp, g, w_a, w_b, wdt, dtb, alog):
    tp, d = hp.shape
    n = w_a.shape[1] + w_b.shape[1]
    tm = _pick(tp, (1024, 512, 256, 128))
    tn = math.gcd(_pick(w_a.shape[1], (1024, 512, 256, 128)), _pick(w_b.shape[1], (1024, 512, 256, 128)))
    n_a = w_a.shape[1] // tn
    vec = pl.BlockSpec((1, LANES), lambda i, j: (0, 0))
    return pl.pallas_call(
        functools.partial(_norm_inproj_kernel, n_a),
        grid=(tp // tm, n // tn),
        in_specs=[
            pl.BlockSpec((tm, d), lambda i, j: (i, 0)),
            pl.BlockSpec((1, d), lambda i, j: (0, 0)),
            pl.BlockSpec((d, tn), lambda i, j: (0, jnp.minimum(j, n_a - 1))),
            pl.BlockSpec((d, tn), lambda i, j: (0, jnp.maximum(j - n_a, 0))),
            pl.BlockSpec((d, LANES), lambda i, j: (0, 0)),
            vec, vec,
        ],
        out_specs=[
            pl.BlockSpec((tm, tn), lambda i, j: (i, j)),
            pl.BlockSpec((tm, LANES), lambda i, j: (i, 0)),
            pl.BlockSpec((tm, LANES), lambda i, j: (i, 0)),
        ],
        out_shape=[
            jax.ShapeDtypeStruct((tp, n), jnp.float32),
            jax.ShapeDtypeStruct((tp, LANES), jnp.float32),
            jax.ShapeDtypeStruct((tp, LANES), jnp.float32),
        ],
        scratch_shapes=[pltpu.VMEM((tm, d), MXU_DTYPE)],
        compiler_params=_params(("parallel", "arbitrary")),
        name="norm_inproj",
    )(hp, g, w_a, w_b, wdt, dtb, alog)


def _ssd_kernel(n_pad, z_ref, xs_ref, b_ref, c_ref, dt_ref, acum_ref,
                cwx_ref, cwb_ref, cwc_ref, cbx_ref, cbb_ref, cbc_ref,
                d_ref, nw_ref, o_ref, state_s):
    g = pl.program_id(1)
    lp = xs_ref.shape[1]
    q = SSD_CHUNK
    gw = xs_ref.shape[2]
    hpg = gw // SSD_HEAD_DIM
    f32 = jnp.float32

    rows = lax.broadcasted_iota(jnp.int32, (LANES, gw), 0)
    cols = lax.broadcasted_iota(jnp.int32, (LANES, gw), 1)
    expand = (rows == g * hpg + lax.shift_right_logical(cols, SSD_HEAD_SHIFT)).astype(f32)

    state_s[...] = jnp.zeros_like(state_s)
    li = lax.broadcasted_iota(jnp.int32, (q, q), 0)
    si = lax.broadcasted_iota(jnp.int32, (q, q), 1)
    causal = li >= si
    lane = lax.broadcasted_iota(jnp.int32, (q, gw), 1)
    row = lax.broadcasted_iota(jnp.int32, (q, gw), 0)

    def chunk(c, carry):
        r0 = pl.multiple_of(c * q, q)
        sl = pl.ds(r0, q)
        xs = _silu(_causal_conv_chunk(xs_ref, cwx_ref, cbx_ref, c))
        bm = _silu(_causal_conv_chunk(b_ref, cwb_ref, cbb_ref, c))
        cm = _silu(_causal_conv_chunk(c_ref, cwc_ref, cbc_ref, c))
        dt_g = _dot_exact_rhs(dt_ref[0, sl, :], expand)
        acum = _dot_exact_rhs(acum_ref[0, sl, :], expand)
        xdt = jnp.where(row + r0 >= n_pad, xs * dt_g, 0.0)
        acum_t = acum.T
        a_last = acum[q - 1:q, :]

        cb = lax.dot_general(cm.astype(MXU_DTYPE), bm.astype(MXU_DTYPE),
                             (((1,), (1,)), ((), ())), preferred_element_type=f32)
        y = d_ref[...] * xs
        for j in range(hpg):
            c0 = j * SSD_HEAD_DIM
            seg = acum[:, c0:c0 + 1] - acum_t[c0:c0 + 1, :]
            decay = jnp.exp(jnp.where(causal, seg, -jnp.inf))
            in_head = (lane >= c0) & (lane < c0 + SSD_HEAD_DIM)
            y = y + _dot(cb * decay, jnp.where(in_head, xdt, 0.0))
        x_end = xdt * jnp.exp(a_last - acum)

        state = state_s[...]
        y = y + _dot(cm, state) * jnp.exp(acum)
        state_s[...] = state * jnp.exp(a_last) + _dot(bm.T, x_end)

        yz = y * _silu(z_ref[0, sl, :])
        yn = yz * lax.rsqrt(jnp.mean(yz * yz, axis=-1, keepdims=True) + EPS)
        o_ref[0, sl, :] = (yn * nw_ref[...]).astype(o_ref.dtype)
        return carry

    lax.fori_loop(0, lp // q, chunk, 0, unroll=SSD_UNROLL)


def _ssd(proj3, dt3, acum3, n_pad, cw, cb, d_skip, norm_w, d_inner):
    b, lp, _ = proj3.shape
    gw = d_inner // SSD_GROUPS
    ns = SSD_STATE
    xs_blk0 = d_inner // gw
    b_blk0 = 2 * d_inner // ns
    c_blk0 = (2 * d_inner + SSD_GROUPS * ns) // ns

    cwx = cw[:, :d_inner]
    cwb = cw[:, d_inner:d_inner + SSD_GROUPS * ns]
    cwc = cw[:, d_inner + SSD_GROUPS * ns:]
    cbx = cb[None, :d_inner]
    cbb = cb[None, d_inner:d_inner + SSD_GROUPS * ns]
    cbc = cb[None, d_inner + SSD_GROUPS * ns:]
    d_ch = jnp.repeat(d_skip, SSD_HEAD_DIM)[None, :]
    nw = norm_w[None, :]

    seq = lambda w, off: pl.BlockSpec((1, lp, w), lambda i, g: (i, 0, off + g))
    head = pl.BlockSpec((1, lp, LANES), lambda i, g: (i, 0, 0))
    vec = lambda w: pl.BlockSpec((1, w), lambda i, g: (0, g))
    cwspec = lambda w: pl.BlockSpec((CONV_K, w), lambda i, g: (0, g))
    return pl.pallas_call(
        functools.partial(_ssd_kernel, n_pad),
        grid=(b, SSD_GROUPS),
        in_specs=[
            seq(gw, 0), seq(gw, xs_blk0), seq(ns, b_blk0), seq(ns, c_blk0), head, head,
            cwspec(gw), cwspec(ns), cwspec(ns), vec(gw), vec(ns), vec(ns),
            vec(gw), vec(gw),
        ],
        out_specs=pl.BlockSpec((1, lp, gw), lambda i, g: (i, 0, g)),
        out_shape=jax.ShapeDtypeStruct((b, lp, d_inner), MXU_DTYPE),
        scratch_shapes=[pltpu.VMEM((ns, gw), jnp.float32)],
        compiler_params=_params(("parallel", "parallel")),
        name="ssd",
    )(proj3, proj3, proj3, proj3, dt3, acum3, cwx, cwb, cwc, cbx, cbb, cbc, d_ch, nw)


def _lru_kernel(n_pad, lx_ref, ly_ref, cw_ref, cb_ref, wa_ref, ba_ref, wx_ref, bx_ref, lam_ref, o_ref):
    lp = lx_ref.shape[1]
    w = lx_ref.shape[2]
    q = SSD_CHUNK
    neg_c_softplus = (-LRU_C) * _softplus(-lam_ref[...])
    row = lax.broadcasted_iota(jnp.int32, (q, w), 0)
    sub = lax.broadcasted_iota(jnp.int32, (SUBLANES, w), 0)

    def chunk(c, h_prev):
        r0 = pl.multiple_of(c * q, q)
        sl = pl.ds(r0, q)
        xr = _causal_conv_chunk(lx_ref, cw_ref, cb_ref, c)
        gate_r = _sigmoid(_dot(xr, wa_ref[0]) + ba_ref[...])
        gate_i = _sigmoid(_dot(xr, wx_ref[0]) + bx_ref[...])
        log_a = gate_r * neg_c_softplus
        a = jnp.exp(log_a)
        mult = jnp.sqrt(jnp.tanh(-log_a) * (a * a + 1.0))
        u = jnp.where(row + r0 >= n_pad, mult * gate_i * xr, 0.0)
        gate_y = jax.nn.gelu(ly_ref[0, sl, :])

        out = []
        for t in range(q // SUBLANES):
            rows8 = slice(t * SUBLANES, (t + 1) * SUBLANES)
            at, ut = a[rows8], u[rows8]
            for d in (1, 2, 4):
                keep = sub >= d
                ut = jnp.where(keep, at * pltpu.roll(ut, d, 0) + ut, ut)
                at = jnp.where(keep, at * pltpu.roll(at, d, 0), at)
            h = at * h_prev + ut
            h_prev = jnp.broadcast_to(h[SUBLANES - 1:SUBLANES, :], (SUBLANES, w))
            out.append(h * gate_y[rows8])
        o_ref[0, sl, :] = jnp.concatenate(out, axis=0).astype(o_ref.dtype)
        return h_prev

    lax.fori_loop(0, lp // q, chunk, jnp.zeros((SUBLANES, w), jnp.float32))


def _lru(proj3, n_pad, lx_col, ly_col, cw, cb, wa, ba, wx, bx, lam, width):
    b, lp, _ = proj3.shape
    w = width // LRU_HEADS
    seq = lambda off: pl.BlockSpec((1, lp, w), lambda i, h: (i, 0, off + h))
    vec = pl.BlockSpec((1, w), lambda i, h: (0, h))
    mat = pl.BlockSpec((1, w, w), lambda i, h: (h, 0, 0))
    return pl.pallas_call(
        functools.partial(_lru_kernel, n_pad),
        grid=(b, LRU_HEADS),
        in_specs=[seq(lx_col // w), seq(ly_col // w),
                  pl.BlockSpec((CONV_K, w), lambda i, h: (0, h)), vec, mat, vec, mat, vec, vec],
        out_specs=pl.BlockSpec((1, lp, w), lambda i, h: (i, 0, h)),
        out_shape=jax.ShapeDtypeStruct((b, lp, width), MXU_DTYPE),
        compiler_params=_params(("parallel", "parallel")),
        name="rglru",
    )(proj3, proj3, cw, cb[None, :], wa, ba[None, :], wx, bx[None, :], lam[None, :])


def _merge_kernel(ys_ref, yl_ref, ws_ref, wl_ref, g0_ref, g1_ref, gb_ref, o_ref):
    y_ssd = jnp.dot(ys_ref[...], ws_ref[...], preferred_element_type=jnp.float32)
    y_lru = jnp.dot(yl_ref[...], wl_ref[...], preferred_element_type=jnp.float32)
    gate0 = _sigmoid(g0_ref[...] + gb_ref[0:1, :])
    gate1 = _sigmoid(g1_ref[...] + gb_ref[1:2, :])
    o_ref[...] = (gate0 * y_ssd + gate1 * y_lru).astype(o_ref.dtype)


def _merge(ys, yl, ws, wl, proj, gate_col, gate_bias):
    tp, d = ys.shape
    tm = _pick(tp, (1024, 512, 256, 128))
    tn = _pick(d, (512, 256, 128))
    g0 = gate_col // tn
    g1 = (gate_col + d) // tn
    return pl.pallas_call(
        _merge_kernel,
        grid=(tp // tm, d // tn),
        in_specs=[
            pl.BlockSpec((tm, d), lambda i, j: (i, 0)),
            pl.BlockSpec((tm, d), lambda i, j: (i, 0)),
            pl.BlockSpec((d, tn), lambda i, j: (0, j)),
            pl.BlockSpec((d, tn), lambda i, j: (0, j)),
            pl.BlockSpec((tm, tn), lambda i, j: (i, g0 + j)),
            pl.BlockSpec((tm, tn), lambda i, j: (i, g1 + j)),
            pl.BlockSpec((N_BRANCH, tn), lambda i, j: (0, j)),
        ],
        out_specs=pl.BlockSpec((tm, tn), lambda i, j: (i, j)),
        out_shape=jax.ShapeDtypeStruct((tp, d), MXU_DTYPE),
        compiler_params=_params(("parallel", "arbitrary")),
        name="merge",
    )(ys, yl, ws, wl, proj, proj, gate_bias)


def _first_index_of_max(p, valid, lane):
    pm = jnp.where(valid, p, -jnp.inf)
    top = jnp.max(pm, axis=-1, keepdims=True)
    idx = jnp.min(jnp.where(valid & (pm == top), lane, LANES), axis=-1, keepdims=True)
    return top, idx


def _masked_softmax(x, valid):
    m = jnp.max(jnp.where(valid, x, -jnp.inf), axis=-1, keepdims=True)
    e = jnp.where(valid, jnp.exp(x - m), 0.0)
    return e / jnp.sum(e, axis=-1, keepdims=True)


def _outproj_router_kernel(mix_ref, h_ref, wo_ref, g_ref, wr_ref, real_ref,
                           h2_ref, u_ref, gate_ref, info_ref, counts_ref, counts_s):
    @pl.when(pl.program_id(0) == 0)
    def _():
        counts_s[...] = jnp.zeros_like(counts_s)

    h2 = h_ref[...] + jnp.dot(mix_ref[...], wo_ref[...], preferred_element_type=jnp.float32)
    h2_ref[...] = h2
    u = h2 * lax.rsqrt(jnp.mean(h2 * h2, axis=-1, keepdims=True) + EPS) * g_ref[...]
    u_ref[...] = u
    logits = _dot(u, wr_ref[...])
    lane = lax.broadcasted_iota(jnp.int32, logits.shape, 1)

    g_prob = _masked_softmax(logits, lane < MOE_GROUPS)
    g_p, g_idx = _first_index_of_max(g_prob, lane < MOE_GROUPS, lane)

    e_lo = MOE_GROUPS + g_idx * MOE_EXP_PER_GROUP
    in_group = (lane >= e_lo) & (lane < e_lo + MOE_EXP_PER_GROUP)
    e_prob = _masked_softmax(logits, in_group)
    p1, i1 = _first_index_of_max(e_prob, in_group, lane)
    rest = in_group & (lane != i1)
    p2, i2 = _first_index_of_max(e_prob, rest, lane)
    denom = p1 + p2

    col = lax.broadcasted_iota(jnp.int32, gate_ref.shape, 1)
    e1 = i1 - MOE_GROUPS
    e2 = i2 - MOE_GROUPS
    gate_ref[...] = jnp.where(col == 0, g_p * p1 / denom, g_p * p2 / denom)

    m = logits.shape[0]
    hit1 = (lane == e1) & (real_ref[...] > 0.0)
    hit2 = (lane == e2) & (real_ref[...] > 0.0)
    onehot = jnp.where(hit1 | hit2, 1.0, 0.0)
    earlier = (lax.broadcasted_iota(jnp.int32, (m, m), 0) > lax.broadcasted_iota(jnp.int32, (m, m), 1))
    before = counts_s[...] + jnp.dot(earlier.astype(jnp.bfloat16), onehot.astype(jnp.bfloat16),
                                     preferred_element_type=jnp.float32)
    rank1 = jnp.sum(jnp.where(hit1, before, 0.0), axis=-1, keepdims=True)
    rank2 = jnp.sum(jnp.where(hit2, before, 0.0), axis=-1, keepdims=True)
    counts_s[...] += jnp.sum(onehot, axis=0, keepdims=True)
    counts_ref[...] = counts_s[...].astype(jnp.int32)
    f32 = jnp.float32
    info = jnp.where(lane == 0, e1.astype(f32), jnp.where(lane == 1, e2.astype(f32),
                     jnp.where(lane == 2, rank1, jnp.where(lane == 3, rank2, 0.0))))
    info_ref[...] = info.T[:SUBLANES, :]


def _outproj_router(mixed, hp, wo, g_ffn, wr, real_rows):
    tp, d = hp.shape
    tm = _pick(tp, (512, 256, 128))
    row = lambda w: pl.BlockSpec((tm, w), lambda i: (i, 0))
    return pl.pallas_call(
        _outproj_router_kernel,
        grid=(tp // tm,),
        in_specs=[row(d), row(d),
                  pl.BlockSpec((d, d), lambda i: (0, 0)),
                  pl.BlockSpec((1, d), lambda i: (0, 0)),
                  pl.BlockSpec((d, LANES), lambda i: (0, 0)),
                  row(1)],
        out_specs=[row(d), row(d), row(MOE_TOPK),
                   pl.BlockSpec((SUBLANES, tm), lambda i: (0, i)),
                   pl.BlockSpec((1, LANES), lambda i: (0, 0))],
        out_shape=[jax.ShapeDtypeStruct((tp, d), jnp.float32),
                   jax.ShapeDtypeStruct((tp, d), jnp.float32),
                   jax.ShapeDtypeStruct((tp, MOE_TOPK), jnp.float32),
                   jax.ShapeDtypeStruct((SUBLANES, tp), jnp.float32),
                   jax.ShapeDtypeStruct((1, LANES), jnp.int32)],
        scratch_shapes=[pltpu.VMEM((1, LANES), jnp.float32)],
        compiler_params=_params(("arbitrary",)),
        name="outproj_router",
    )(mixed, hp, wo, g_ffn, wr, real_rows)


def _wait_rows(n, make_copy):
    for k in range(EXPERT_ROWS.bit_length() - 1, -1, -1):
        @pl.when((n & (1 << k)) != 0)
        def _():
            make_copy(1 << k).wait()


def _expert_kernel(n_items_ref, n_used_ref, item_e_ref, item_start_ref, item_rows_ref, item_real_ref, tok_ref,
                   u_hbm, w1_ref, w3_ref, w2_ref, y_hbm,
                   xg_s, xb_s, acc_s, sem):
    i = pl.program_id(0)
    f = pl.program_id(1)
    nf = pl.num_programs(1)
    n_items = n_items_ref[0]
    active = i < n_items
    n_rows = item_rows_ref[i]
    d = acc_s.shape[1]
    blk = MOE_BLOCK

    def issue_gather(item):
        start = item_start_ref[item]
        n = item_real_ref[item]
        n_groups = n // GATHER_UNROLL

        def one(r):
            pltpu.make_async_copy(u_hbm.at[pl.ds(tok_ref[start + r], 1)], xg_s.at[pl.ds(r, 1)], sem.at[0]).start()

        def group(g, c):
            for j in range(GATHER_UNROLL):
                one(g * GATHER_UNROLL + j)
            return c
        lax.fori_loop(0, n_groups, group, 0)

        def single(r, c):
            one(r)
            return c
        lax.fori_loop(n_groups * GATHER_UNROLL, n, single, 0)

    def wait_gather(item):
        _wait_rows(item_real_ref[item],
                   lambda m: pltpu.make_async_copy(u_hbm.at[pl.ds(0, m)], xg_s.at[pl.ds(0, m)], sem.at[0]))

    def out_copy(item, j):
        r0 = pl.multiple_of(j * blk, blk)
        dst0 = pl.multiple_of(item_start_ref[item] + r0, blk)
        return pltpu.make_async_copy(acc_s.at[pl.ds(r0, blk)], y_hbm.at[pl.ds(dst0, blk)], sem.at[1])

    def issue_out(item):
        def body(j, c):
            out_copy(item, j).start()
            return c
        lax.fori_loop(0, item_rows_ref[item] // blk, body, 0)

    def wait_out(item):
        def body(j, c):
            out_copy(item, j).wait()
            return c
        lax.fori_loop(0, item_rows_ref[item] // blk, body, 0)

    @pl.when((i == 0) & (f == 0))
    def _():
        xg_s[...] = jnp.zeros_like(xg_s)
        acc_s[pl.ds(0, blk), :] = jnp.zeros((blk, d), acc_s.dtype)
        n_used = n_used_ref[0]
        n_blocks = y_hbm.shape[0] // blk

        def tail_copy(j):
            return pltpu.make_async_copy(acc_s.at[pl.ds(0, blk)], y_hbm.at[pl.ds(pl.multiple_of(j * blk, blk), blk)],
                                         sem.at[1])

        def fill(j, c):
            tail_copy(j).start()
            return c
        lax.fori_loop(n_used, n_blocks, fill, 0)

        def drain(j, c):
            tail_copy(j).wait()
            return c
        lax.fori_loop(n_used, n_blocks, drain, 0)

        @pl.when(active)
        def _():
            issue_gather(0)

    @pl.when(active & (f == 0))
    def _():
        wait_gather(i)
        for lo, m in zip((0,) + EXPERT_ROW_VARIANTS, EXPERT_ROW_VARIANTS):
            @pl.when((n_rows > lo) & (n_rows <= m))
            def _():
                xb_s[pl.ds(0, m), :] = xg_s[pl.ds(0, m), :].astype(xb_s.dtype)

        @pl.when(i + 1 < n_items)
        def _():
            issue_gather(i + 1)

    @pl.when((f == 0) & (i >= 1) & (i - 1 < n_items) & jnp.logical_not(active))
    def _():
        wait_out(i - 1)

    for lo, m in zip((0,) + EXPERT_ROW_VARIANTS, EXPERT_ROW_VARIANTS):
        @pl.when(active & (n_rows > lo) & (n_rows <= m))
        def _():
            rows = pl.ds(0, m)
            x = xb_s[rows, :]
            a = jnp.dot(x, w1_ref[0].astype(MXU_DTYPE), preferred_element_type=jnp.float32)
            b = jnp.dot(x, w3_ref[0].astype(MXU_DTYPE), preferred_element_type=jnp.float32)
            hdn = (_silu(a) * b).astype(MXU_DTYPE)

            @pl.when((f == 0) & (i >= 1))
            def _():
                wait_out(i - 1)

            for c0 in range(0, d, EXPERT_OUT_TILE):
                cols = pl.ds(c0, EXPERT_OUT_TILE)
                part = jnp.dot(hdn, w2_ref[0, :, cols].astype(MXU_DTYPE), preferred_element_type=jnp.float32)

                @pl.when(f == 0)
                def _():
                    acc_s[rows, cols] = part

                @pl.when(f > 0)
                def _():
                    acc_s[rows, cols] += part

    @pl.when(f == nf - 1)
    def _():
        @pl.when(active)
        def _():
            issue_out(i)

        @pl.when(active & (i == pl.num_programs(0) - 1))
        def _():
            wait_out(i)


def _experts(u, sched, w1, w3, w2, n_items_max, p_len):
    tp, d = u.shape
    ff = w1.shape[2]
    tf = EXPERT_FF_TILE
    nf = ff // tf
    n_items, n_used, item_e, item_start, item_rows, item_real, tok_sorted = sched

    def w_in_map(i, f, n_items, n_used, item_e, *_):
        return (item_e[i], 0, jnp.where(i < n_items[0], f, nf - 1))

    def w_out_map(i, f, n_items, n_used, item_e, *_):
        return (item_e[i], jnp.where(i < n_items[0], f, nf - 1), 0)

    grid_spec = pltpu.PrefetchScalarGridSpec(
        num_scalar_prefetch=7,
        grid=(n_items_max, nf),
        in_specs=[
            pl.BlockSpec(memory_space=pl.ANY),
            pl.BlockSpec((1, d, tf), w_in_map),
            pl.BlockSpec((1, d, tf), w_in_map),
            pl.BlockSpec((1, tf, d), w_out_map),
        ],
        out_specs=pl.BlockSpec(memory_space=pl.ANY),
        scratch_shapes=[
            pltpu.VMEM((EXPERT_ROWS, d), jnp.float32),
            pltpu.VMEM((EXPERT_ROWS, d), MXU_DTYPE),
            pltpu.VMEM((EXPERT_ROWS, d), jnp.float32),
            pltpu.SemaphoreType.DMA((2,)),
        ],
    )
    return pl.pallas_call(
        _expert_kernel,
        grid_spec=grid_spec,
        out_shape=jax.ShapeDtypeStruct((p_len, d), jnp.float32),
        compiler_params=_params(("arbitrary", "arbitrary")),
        name="experts",
    )(n_items, n_used, item_e, item_start, item_rows, item_real, tok_sorted, u, w1, w3, w2)


def _expert_schedule(info, counts, batch, lp, n_pad, n_items_max, p_len):
    i32 = jnp.int32
    eid = info[:MOE_TOPK].astype(i32)
    rank = info[MOE_TOPK:2 * MOE_TOPK].astype(i32)
    padded = (counts + MOE_BLOCK - 1) // MOE_BLOCK * MOE_BLOCK
    pad_end = jnp.cumsum(padded)
    pad_start = pad_end - padded
    real = lambda a: a.reshape(MOE_TOPK, batch, lp)[:, :, n_pad:]
    dest = real(pad_start[eid] + rank)
    tok = jnp.broadcast_to(jnp.arange(batch * lp, dtype=i32)[None, :], (MOE_TOPK, batch * lp))
    tok_sorted = jnp.zeros((p_len,), i32).at[dest.reshape(-1)].set(real(tok).reshape(-1))
    n_used = (pad_end[-1:] // MOE_BLOCK).astype(i32)

    chunks = (padded + EXPERT_ROWS - 1) // EXPERT_ROWS
    chunk_end = jnp.cumsum(chunks)
    item = jnp.arange(n_items_max, dtype=i32)
    item_e = jnp.minimum(jnp.sum((chunk_end[None, :] <= item[:, None]).astype(i32), axis=1), MOE_EXPERTS - 1)
    n_items = chunk_end[-1:].astype(i32)
    last_e = item_e[jnp.maximum(n_items[0] - 1, 0)]
    item_e = jnp.where(item < n_items[0], item_e, last_e)
    k = item - (chunk_end - chunks)[item_e]
    item_start = (pad_start[item_e] + k * EXPERT_ROWS).astype(i32)
    item_rows = jnp.clip(padded[item_e] - k * EXPERT_ROWS, 0, EXPERT_ROWS).astype(i32)
    item_real = jnp.clip(counts[item_e] - k * EXPERT_ROWS, 0, EXPERT_ROWS).astype(i32)
    live = item < n_items[0]
    item_start = jnp.where(live, item_start, 0)
    item_rows = jnp.where(live, item_rows, 0)
    item_real = jnp.where(live, item_real, 0)
    return (n_items, n_used, item_e, item_start, item_rows, item_real, tok_sorted), dest


def _final_kernel(tm, dest_ref, h_ref, gate_ref, g_ref, ys_hbm, o_ref, ybuf, sem):
    t = pl.program_id(0)
    n_tiles = pl.num_programs(0)

    def issue(tile, buf):
        n_x = n_tiles * tm
        for j in range(tm):
            for k in range(MOE_TOPK):
                pos = dest_ref[k * n_x + tile * tm + j]
                pltpu.make_async_copy(ys_hbm.at[pl.ds(pos, 1)], ybuf.at[buf, k, pl.ds(j, 1)], sem.at[buf]).start()

    @pl.when(t == 0)
    def _():
        issue(0, 0)

    @pl.when(t + 1 < n_tiles)
    def _():
        issue(t + 1, (t + 1) % 2)

    buf = t % 2
    for k in range(MOE_TOPK):
        pltpu.make_async_copy(ys_hbm.at[pl.ds(0, tm)], ybuf.at[buf, k], sem.at[buf]).wait()
    moe = gate_ref[:, 0:1] * ybuf[buf, 0] + gate_ref[:, 1:2] * ybuf[buf, 1]
    h = h_ref[...] + moe
    o_ref[...] = h * lax.rsqrt(jnp.mean(h * h, axis=-1, keepdims=True) + EPS) * g_ref[...]


def _final(h2, ys, dest_x, gates, g_final, batch, lp, seq):
    tp, d = h2.shape
    tm = SSD_CHUNK
    skip = (lp - seq) // tm
    per_b = seq // tm
    src = lambda w: pl.BlockSpec((tm, w), lambda t, dest: ((t // per_b) * (lp // tm) + skip + t % per_b, 0))
    grid_spec = pltpu.PrefetchScalarGridSpec(
        num_scalar_prefetch=1,
        grid=(batch * per_b,),
        in_specs=[src(d), src(MOE_TOPK), pl.BlockSpec((1, d), lambda t, dest: (0, 0)),
                  pl.BlockSpec(memory_space=pl.ANY)],
        out_specs=pl.BlockSpec((tm, d), lambda t, dest: (t, 0)),
        scratch_shapes=[pltpu.VMEM((2, MOE_TOPK, tm, d), jnp.float32), pltpu.SemaphoreType.DMA((2,))],
    )
    return pl.pallas_call(
        functools.partial(_final_kernel, tm),
        grid_spec=grid_spec,
        out_shape=jax.ShapeDtypeStruct((batch * seq, d), jnp.float32),
        compiler_params=_params(("arbitrary",)),
        name="final_norm",
    )(dest_x, h2, gates, g_final, ys)


def kernel(x, meta_tokens, norm_mix, w_in, ssd_conv_w, ssd_conv_b, ssd_dt_bias, ssd_a_log, ssd_d, ssd_norm, w_ssd_out, lru_conv_w, lru_conv_b, lru_wa, lru_ba, lru_wx, lru_bx, lru_lambda, w_lru_out, gate_bias, w_out, norm_ffn, w_router_group, w_router_expert, w_exp_gate, w_exp_up, w_exp_down, norm_final):
    batch, seq, d = x.shape
    depth = norm_mix.shape[0]
    assert depth == 1 and seq % SSD_CHUNK == 0
    l = N_META + seq
    lp = -(-l // SSD_CHUNK) * SSD_CHUNK
    n_pad = lp - l
    d_inner = d
    heads = d_inner // SSD_HEAD_DIM
    conv_dim = d_inner + 2 * SSD_GROUPS * SSD_STATE
    col_xbc = d_inner
    col_dt = col_xbc + conv_dim
    col_lx = col_dt + heads
    f32 = jnp.float32
    lyr = 0

    meta = jnp.broadcast_to(meta_tokens.astype(f32)[None], (batch, N_META, d))
    hp = jnp.concatenate([jnp.zeros((batch, n_pad, d), f32), meta, x], axis=1).reshape(batch * lp, d)

    w = w_in[lyr]
    w_a = w[:, :col_dt].astype(MXU_DTYPE)
    w_b = w[:, col_lx:].astype(MXU_DTYPE)
    w_dt = jnp.zeros((d, LANES), f32).at[:, :heads].set(w[:, col_dt:col_lx]).astype(MXU_DTYPE)
    dtb = jnp.zeros((1, LANES), f32).at[0, :heads].set(ssd_dt_bias[lyr])
    alog = jnp.zeros((1, LANES), f32).at[0, :heads].set(ssd_a_log[lyr])
    proj, dt, acum = _norm_inproj(hp, norm_mix[lyr][None, :], w_a, w_b, w_dt, dtb, alog)
    n_main = w_a.shape[1] + w_b.shape[1]
    proj3 = proj.reshape(batch, lp, n_main)
    lx_col = col_dt
    ly_col = lx_col + d
    gl_col = ly_col + d

    ys = _ssd(proj3, dt.reshape(batch, lp, LANES), acum.reshape(batch, lp, LANES), n_pad, ssd_conv_w[lyr],
              ssd_conv_b[lyr], ssd_d[lyr], ssd_norm[lyr], d_inner)
    yl = _lru(proj3, n_pad, lx_col, ly_col, lru_conv_w[lyr], lru_conv_b[lyr], lru_wa[lyr].astype(MXU_DTYPE),
              lru_ba[lyr], lru_wx[lyr].astype(MXU_DTYPE), lru_bx[lyr], lru_lambda[lyr], d)

    mixed = _merge(ys.reshape(batch * lp, d), yl.reshape(batch * lp, d), w_ssd_out[lyr].astype(MXU_DTYPE),
                   w_lru_out[lyr].astype(MXU_DTYPE), proj, gl_col, gate_bias[lyr])

    w_router = jnp.zeros((d, LANES), f32)
    w_router = w_router.at[:, :MOE_GROUPS].set(w_router_group[lyr])
    w_router = w_router.at[:, MOE_GROUPS:MOE_GROUPS + MOE_EXPERTS].set(w_router_expert[lyr])
    real_rows = jnp.asarray(((np.arange(batch * lp) % lp) >= n_pad).astype(np.float32)[:, None])
    h2, u2, gates, info, counts = _outproj_router(mixed, hp, w_out[lyr].astype(MXU_DTYPE), norm_ffn[lyr][None, :],
                                                  w_router.astype(MXU_DTYPE), real_rows)

    n_asg = batch * l * MOE_TOPK
    p_max = n_asg + MOE_EXPERTS * (MOE_BLOCK - 1)
    n_items_max = MOE_EXPERTS + p_max // EXPERT_ROWS
    p_len = -(-p_max // MOE_BLOCK) * MOE_BLOCK
    sched, dest = _expert_schedule(info, counts[0, :MOE_EXPERTS], batch, lp, n_pad, n_items_max, p_len)
    ff = w_exp_gate.shape[-1]
    ys_sorted = _experts(u2, sched, w_exp_gate.reshape(MOE_EXPERTS, d, ff), w_exp_up.reshape(MOE_EXPERTS, d, ff),
                         w_exp_down.reshape(MOE_EXPERTS, ff, d), n_items_max, p_len)

    dest_x = dest[:, :, N_META:].reshape(-1)
    out = _final(h2, ys_sorted, dest_x, gates, norm_final[None, :], batch, lp, seq)
    return out.reshape(batch, seq, d)
```

```python
import functools
import math

import numpy as np
import jax
import jax.numpy as jnp
from jax import lax
from jax.experimental import pallas as pl
from jax.experimental.pallas import tpu as pltpu

N_META = 16
CONV_K = 4
EPS = 1e-6
SSD_HEAD_DIM = 64
SSD_HEAD_SHIFT = 6
SSD_GROUPS = 8
SSD_STATE = 128
SSD_CHUNK = 128
LRU_HEADS = 8
LRU_C = 8.0
N_BRANCH = 2
MOE_GROUPS = 8
MOE_EXP_PER_GROUP = 8
MOE_EXPERTS = MOE_GROUPS * MOE_EXP_PER_GROUP
MOE_TOPK = 2
MOE_BLOCK = 128

LANES = 128
SUBLANES = 8
VMEM_LIMIT = 56 * 1024 * 1024
MXU_DTYPE = jnp.bfloat16

EXPERT_ROWS = 1024
EXPERT_ROW_VARIANTS = (256, 512, 640, 768, 1024)
EXPERT_FF_TILE = 512
EXPERT_OUT_TILE = 1024
GATHER_UNROLL = 8
SSD_UNROLL = 2


def _pick(n, options):
    for o in options:
        if n % o == 0:
            return o
    raise ValueError(f"no tile in {options} divides {n}")


def _params(sem, vmem=VMEM_LIMIT):
    return pltpu.CompilerParams(dimension_semantics=sem, vmem_limit_bytes=vmem)


def _dot(a, b):
    return jnp.dot(a.astype(MXU_DTYPE), b.astype(MXU_DTYPE), preferred_element_type=jnp.float32)


def _dot_exact_rhs(v, sel):
    sel = sel.astype(jnp.bfloat16)
    hi = v.astype(jnp.bfloat16)
    r1 = v - hi.astype(jnp.float32)
    mid = r1.astype(jnp.bfloat16)
    lo = (r1 - mid.astype(jnp.float32)).astype(jnp.bfloat16)
    f32 = jnp.float32
    return (jnp.dot(hi, sel, preferred_element_type=f32) + jnp.dot(mid, sel, preferred_element_type=f32)
            + jnp.dot(lo, sel, preferred_element_type=f32))


def _dot_exact_lhs(sel, v):
    sel = sel.astype(jnp.bfloat16)
    hi = v.astype(jnp.bfloat16)
    r1 = v - hi.astype(jnp.float32)
    mid = r1.astype(jnp.bfloat16)
    lo = (r1 - mid.astype(jnp.float32)).astype(jnp.bfloat16)
    f32 = jnp.float32
    return (jnp.dot(sel, hi, preferred_element_type=f32) + jnp.dot(sel, mid, preferred_element_type=f32)
            + jnp.dot(sel, lo, preferred_element_type=f32))


def _softplus(x):
    return jnp.maximum(x, 0.0) + jnp.log1p(jnp.exp(-jnp.abs(x)))


def _sigmoid(x):
    return 1.0 / (1.0 + jnp.exp(-x))


def _silu(x):
    return x * _sigmoid(x)


def _causal_conv_chunk(x_ref, w_ref, b_ref, c):
    q = SSD_CHUNK
    r0 = pl.multiple_of(c * q, q)
    cur = x_ref[0, pl.ds(r0, q), :]
    prev = x_ref[0, pl.ds(pl.multiple_of(jnp.maximum(r0 - SUBLANES, 0), SUBLANES), SUBLANES), :]
    prev = jnp.where(c > 0, prev, 0.0)
    sub = lax.broadcasted_iota(jnp.int32, prev.shape, 0)
    acc = b_ref[...] + w_ref[CONV_K - 1:CONV_K, :] * cur
    for k in range(1, CONV_K):
        rolled = pltpu.roll(cur, k, 0)
        head = jnp.where(sub < k, pltpu.roll(prev, k, 0), rolled[:SUBLANES])
        shifted = jnp.concatenate([head, rolled[SUBLANES:]], axis=0)
        acc = acc + w_ref[CONV_K - 1 - k:CONV_K - k, :] * shifted
    return acc


def _norm_inproj_kernel(n_a, x_ref, g_ref, wa_ref, wb_ref, wdt_ref, dtb_ref, alog_ref, o_ref, dt_ref, acum_ref, xn_ref):
    j = pl.program_id(1)

    @pl.when(j == 0)
    def _():
        x = x_ref[...]
        y = x * lax.rsqrt(jnp.mean(x * x, axis=-1, keepdims=True) + EPS)
        xn_ref[...] = (y * g_ref[...]).astype(xn_ref.dtype)
        raw = jnp.dot(xn_ref[...], wdt_ref[...], preferred_element_type=jnp.float32)
        dt = _softplus(raw + dtb_ref[...])
        dt_ref[...] = dt
        adt = dt * (-jnp.exp(alog_ref[...]))
        q = SSD_CHUNK
        tri = (lax.broadcasted_iota(jnp.int32, (q, q), 0) >= lax.broadcasted_iota(jnp.int32, (q, q), 1))
        for c in range(x.shape[0] // q):
            acum_ref[c * q:(c + 1) * q, :] = _dot_exact_lhs(tri.astype(jnp.float32), adt[c * q:(c + 1) * q, :])

    @pl.when(j < n_a)
    def _():
        o_ref[...] = jnp.dot(xn_ref[...], wa_ref[...], preferred_element_type=jnp.float32)

    @pl.when(j >= n_a)
    def _():
        o_ref[...] = jnp.dot(xn_ref[...], wb_ref[...], preferred_element_type=jnp.float32)


def _norm_inproj(hp, g, w_a, w_b, wdt, dtb, alog):
    tp, d = hp.shape
    n = w_a.shape[1] + w_b.shape[1]
    tm = _pick(tp, (1024, 512, 256, 128))
    tn = math.gcd(_pick(w_a.shape[1], (1024, 512, 256, 128)), _pick(w_b.shape[1], (1024, 512, 256, 128)))
    n_a = w_a.shape[1] // tn
    vec = pl.BlockSpec((1, LANES), lambda i, j: (0, 0))
    return pl.pallas_call(
        functools.partial(_norm_inproj_kernel, n_a),
        grid=(tp // tm, n // tn),
        in_specs=[
            pl.BlockSpec((tm, d), lambda i, j: (i, 0)),
            pl.BlockSpec((1, d), lambda i, j: (0, 0)),
            pl.BlockSpec((d, tn), lambda i, j: (0, jnp.minimum(j, n_a - 1))),
            pl.BlockSpec((d, tn), lambda i, j: (0, jnp.maximum(j - n_a, 0))),
            pl.BlockSpec((d, LANES), lambda i, j: (0, 0)),
            vec, vec,
        ],
        out_specs=[
            pl.BlockSpec((tm, tn), lambda i, j: (i, j)),
            pl.BlockSpec((tm, LANES), lambda i, j: (i, 0)),
            pl.BlockSpec((tm, LANES), lambda i, j: (i, 0)),
        ],
        out_shape=[
            jax.ShapeDtypeStruct((tp, n), jnp.float32),
            jax.ShapeDtypeStruct((tp, LANES), jnp.float32),
            jax.ShapeDtypeStruct((tp, LANES), jnp.float32),
        ],
        scratch_shapes=[pltpu.VMEM((tm, d), MXU_DTYPE)],
        compiler_params=_params(("parallel", "arbitrary")),
        name="norm_inproj",
    )(hp, g, w_a, w_b, wdt, dtb, alog)


def _ssd_kernel(n_pad, z_ref, xs_ref, b_ref, c_ref, dt_ref, acum_ref,
                cwx_ref, cwb_ref, cwc_ref, cbx_ref, cbb_ref, cbc_ref,
                d_ref, nw_ref, o_ref, state_s):
    g = pl.program_id(1)
    lp = xs_ref.shape[1]
    q = SSD_CHUNK
    gw = xs_ref.shape[2]
    hpg = gw // SSD_HEAD_DIM
    f32 = jnp.float32

    rows = lax.broadcasted_iota(jnp.int32, (LANES, gw), 0)
    cols = lax.broadcasted_iota(jnp.int32, (LANES, gw), 1)
    expand = (rows == g * hpg + lax.shift_right_logical(cols, SSD_HEAD_SHIFT)).astype(f32)

    state_s[...] = jnp.zeros_like(state_s)
    li = lax.broadcasted_iota(jnp.int32, (q, q), 0)
    si = lax.broadcasted_iota(jnp.int32, (q, q), 1)
    causal = li >= si
    lane = lax.broadcasted_iota(jnp.int32, (q, gw), 1)
    row = lax.broadcasted_iota(jnp.int32, (q, gw), 0)

    def chunk(c, carry):
        r0 = pl.multiple_of(c * q, q)
        sl = pl.ds(r0, q)
        xs = _silu(_causal_conv_chunk(xs_ref, cwx_ref, cbx_ref, c))
        bm = _silu(_causal_conv_chunk(b_ref, cwb_ref, cbb_ref, c))
        cm = _silu(_causal_conv_chunk(c_ref, cwc_ref, cbc_ref, c))
        dt_g = _dot_exact_rhs(dt_ref[0, sl, :], expand)
        acum = _dot_exact_rhs(acum_ref[0, sl, :], expand)
        xdt = jnp.where(row + r0 >= n_pad, xs * dt_g, 0.0)
        acum_t = acum.T
        a_last = acum[q - 1:q, :]

        cb = lax.dot_general(cm.astype(MXU_DTYPE), bm.astype(MXU_DTYPE),
                             (((1,), (1,)), ((), ())), preferred_element_type=f32)
        y = d_ref[...] * xs
        for j in range(hpg):
            c0 = j * SSD_HEAD_DIM
            seg = acum[:, c0:c0 + 1] - acum_t[c0:c0 + 1, :]
            decay = jnp.exp(jnp.where(causal, seg, -jnp.inf))
            in_head = (lane >= c0) & (lane < c0 + SSD_HEAD_DIM)
            y = y + _dot(cb * decay, jnp.where(in_head, xdt, 0.0))
        x_end = xdt * jnp.exp(a_last - acum)

        state = state_s[...]
        y = y + _dot(cm, state) * jnp.exp(acum)
        state_s[...] = state * jnp.exp(a_last) + _dot(bm.T, x_end)

        yz = y * _silu(z_ref[0, sl, :])
        yn = yz * lax.rsqrt(jnp.mean(yz * yz, axis=-1, keepdims=True) + EPS)
        o_ref[0, sl, :] = (yn * nw_ref[...]).astype(o_ref.dtype)
        return carry

    lax.fori_loop(0, lp // q, chunk, 0, unroll=SSD_UNROLL)


def _ssd(proj3, dt3, acum3, n_pad, cw, cb, d_skip, norm_w, d_inner):
    b, lp, _ = proj3.shape
    gw = d_inner // SSD_GROUPS
    ns = SSD_STATE
    xs_blk0 = d_inner // gw
    b_blk0 = 2 * d_inner // ns
    c_blk0 = (2 * d_inner + SSD_GROUPS * ns) // ns

    cwx = cw[:, :d_inner]
    cwb = cw[:, d_inner:d_inner + SSD_GROUPS * ns]
    cwc = cw[:, d_inner + SSD_GROUPS * ns:]
    cbx = cb[None, :d_inner]
    cbb = cb[None, d_inner:d_inner + SSD_GROUPS * ns]
    cbc = cb[None, d_inner + SSD_GROUPS * ns:]
    d_ch = jnp.repeat(d_skip, SSD_HEAD_DIM)[None, :]
    nw = norm_w[None, :]

    seq = lambda w, off: pl.BlockSpec((1, lp, w), lambda i, g: (i, 0, off + g))
    head = pl.BlockSpec((1, lp, LANES), lambda i, g: (i, 0, 0))
    vec = lambda w: pl.BlockSpec((1, w), lambda i, g: (0, g))
    cwspec = lambda w: pl.BlockSpec((CONV_K, w), lambda i, g: (0, g))
    return pl.pallas_call(
        functools.partial(_ssd_kernel, n_pad),
        grid=(b, SSD_GROUPS),
        in_specs=[
            seq(gw, 0), seq(gw, xs_blk0), seq(ns, b_blk0), seq(ns, c_blk0), head, head,
            cwspec(gw), cwspec(ns), cwspec(ns), vec(gw), vec(ns), vec(ns),
            vec(gw), vec(gw),
        ],
        out_specs=pl.BlockSpec((1, lp, gw), lambda i, g: (i, 0, g)),
        out_shape=jax.ShapeDtypeStruct((b, lp, d_inner), MXU_DTYPE),
        scratch_shapes=[pltpu.VMEM((ns, gw), jnp.float32)],
        compiler_params=_params(("parallel", "parallel")),
        name="ssd",
    )(proj3, proj3, proj3, proj3, dt3, acum3, cwx, cwb, cwc, cbx, cbb, cbc, d_ch, nw)


def _lru_kernel(n_pad, lx_ref, ly_ref, cw_ref, cb_ref, wa_ref, ba_ref, wx_ref, bx_ref, lam_ref, o_ref):
    lp = lx_ref.shape[1]
    w = lx_ref.shape[2]
    q = SSD_CHUNK
    neg_c_softplus = (-LRU_C) * _softplus(-lam_ref[...])
    row = lax.broadcasted_iota(jnp.int32, (q, w), 0)
    sub = lax.broadcasted_iota(jnp.int32, (SUBLANES, w), 0)

    def chunk(c, h_prev):
        r0 = pl.multiple_of(c * q, q)
        sl = pl.ds(r0, q)
        xr = _causal_conv_chunk(lx_ref, cw_ref, cb_ref, c)
        gate_r = _sigmoid(_dot(xr, wa_ref[0]) + ba_ref[...])
        gate_i = _sigmoid(_dot(xr, wx_ref[0]) + bx_ref[...])
        log_a = gate_r * neg_c_softplus
        a = jnp.exp(log_a)
        mult = jnp.sqrt(jnp.tanh(-log_a) * (a * a + 1.0))
        u = jnp.where(row + r0 >= n_pad, mult * gate_i * xr, 0.0)
        gate_y = jax.nn.gelu(ly_ref[0, sl, :])

        out = []
        for t in range(q // SUBLANES):
            rows8 = slice(t * SUBLANES, (t + 1) * SUBLANES)
            at, ut = a[rows8], u[rows8]
            for d in (1, 2, 4):
                keep = sub >= d
                ut = jnp.where(keep, at * pltpu.roll(ut, d, 0) + ut, ut)
                at = jnp.where(keep, at * pltpu.roll(at, d, 0), at)
            h = at * h_prev + ut
            h_prev = jnp.broadcast_to(h[SUBLANES - 1:SUBLANES, :], (SUBLANES, w))
            out.append(h * gate_y[rows8])
        o_ref[0, sl, :] = jnp.concatenate(out, axis=0).astype(o_ref.dtype)
        return h_prev

    lax.fori_loop(0, lp // q, chunk, jnp.zeros((SUBLANES, w), jnp.float32), unroll=SSD_UNROLL)


def _lru(proj3, n_pad, lx_col, ly_col, cw, cb, wa, ba, wx, bx, lam, width):
    b, lp, _ = proj3.shape
    w = width // LRU_HEADS
    seq = lambda off: pl.BlockSpec((1, lp, w), lambda i, h: (i, 0, off + h))
    vec = pl.BlockSpec((1, w), lambda i, h: (0, h))
    mat = pl.BlockSpec((1, w, w), lambda i, h: (h, 0, 0))
    return pl.pallas_call(
        functools.partial(_lru_kernel, n_pad),
        grid=(b, LRU_HEADS),
        in_specs=[seq(lx_col // w), seq(ly_col // w),
                  pl.BlockSpec((CONV_K, w), lambda i, h: (0, h)), vec, mat, vec, mat, vec, vec],
        out_specs=pl.BlockSpec((1, lp, w), lambda i, h: (i, 0, h)),
        out_shape=jax.ShapeDtypeStruct((b, lp, width), MXU_DTYPE),
        compiler_params=_params(("parallel", "parallel")),
        name="rglru",
    )(proj3, proj3, cw, cb[None, :], wa, ba[None, :], wx, bx[None, :], lam[None, :])


def _merge_kernel(ys_ref, yl_ref, ws_ref, wl_ref, g0_ref, g1_ref, gb_ref, o_ref):
    y_ssd = jnp.dot(ys_ref[...], ws_ref[...], preferred_element_type=jnp.float32)
    y_lru = jnp.dot(yl_ref[...], wl_ref[...], preferred_element_type=jnp.float32)
    gate0 = _sigmoid(g0_ref[...] + gb_ref[0:1, :])
    gate1 = _sigmoid(g1_ref[...] + gb_ref[1:2, :])
    o_ref[...] = (gate0 * y_ssd + gate1 * y_lru).astype(o_ref.dtype)


def _merge(ys, yl, ws, wl, proj, gate_col, gate_bias):
    tp, d = ys.shape
    tm = _pick(tp, (1024, 512, 256, 128))
    tn = _pick(d, (512, 256, 128))
    g0 = gate_col // tn
    g1 = (gate_col + d) // tn
    return pl.pallas_call(
        _merge_kernel,
        grid=(tp // tm, d // tn),
        in_specs=[
            pl.BlockSpec((tm, d), lambda i, j: (i, 0)),
            pl.BlockSpec((tm, d), lambda i, j: (i, 0)),
            pl.BlockSpec((d, tn), lambda i, j: (0, j)),
            pl.BlockSpec((d, tn), lambda i, j: (0, j)),
            pl.BlockSpec((tm, tn), lambda i, j: (i, g0 + j)),
            pl.BlockSpec((tm, tn), lambda i, j: (i, g1 + j)),
            pl.BlockSpec((N_BRANCH, tn), lambda i, j: (0, j)),
        ],
        out_specs=pl.BlockSpec((tm, tn), lambda i, j: (i, j)),
        out_shape=jax.ShapeDtypeStruct((tp, d), MXU_DTYPE),
        compiler_params=_params(("parallel", "arbitrary")),
        name="merge",
    )(ys, yl, ws, wl, proj, proj, gate_bias)


def _first_index_of_max(p, valid, lane):
    pm = jnp.where(valid, p, -jnp.inf)
    top = jnp.max(pm, axis=-1, keepdims=True)
    idx = jnp.min(jnp.where(valid & (pm == top), lane, LANES), axis=-1, keepdims=True)
    return top, idx


def _masked_softmax(x, valid):
    m = jnp.max(jnp.where(valid, x, -jnp.inf), axis=-1, keepdims=True)
    e = jnp.where(valid, jnp.exp(x - m), 0.0)
    return e / jnp.sum(e, axis=-1, keepdims=True)


def _outproj_router_kernel(mix_ref, h_ref, wo_ref, g_ref, wr_ref, real_ref,
                           h2_ref, u_ref, gate_ref, info_ref, counts_ref, counts_s):
    @pl.when(pl.program_id(0) == 0)
    def _():
        counts_s[...] = jnp.zeros_like(counts_s)

    h2 = h_ref[...] + jnp.dot(mix_ref[...], wo_ref[...], preferred_element_type=jnp.float32)
    h2_ref[...] = h2
    u = h2 * lax.rsqrt(jnp.mean(h2 * h2, axis=-1, keepdims=True) + EPS) * g_ref[...]
    u_ref[...] = u
    logits = _dot(u, wr_ref[...])
    lane = lax.broadcasted_iota(jnp.int32, logits.shape, 1)

    g_prob = _masked_softmax(logits, lane < MOE_GROUPS)
    g_p, g_idx = _first_index_of_max(g_prob, lane < MOE_GROUPS, lane)

    e_lo = MOE_GROUPS + g_idx * MOE_EXP_PER_GROUP
    in_group = (lane >= e_lo) & (lane < e_lo + MOE_EXP_PER_GROUP)
    e_prob = _masked_softmax(logits, in_group)
    p1, i1 = _first_index_of_max(e_prob, in_group, lane)
    rest = in_group & (lane != i1)
    p2, i2 = _first_index_of_max(e_prob, rest, lane)
    denom = p1 + p2

    col = lax.broadcasted_iota(jnp.int32, gate_ref.shape, 1)
    e1 = i1 - MOE_GROUPS
    e2 = i2 - MOE_GROUPS
    gate_ref[...] = jnp.where(col == 0, g_p * p1 / denom, g_p * p2 / denom)

    m = logits.shape[0]
    hit1 = (lane == e1) & (real_ref[...] > 0.0)
    hit2 = (lane == e2) & (real_ref[...] > 0.0)
    onehot = jnp.where(hit1 | hit2, 1.0, 0.0)
    earlier = (lax.broadcasted_iota(jnp.int32, (m, m), 0) > lax.broadcasted_iota(jnp.int32, (m, m), 1))
    before = counts_s[...] + jnp.dot(earlier.astype(jnp.bfloat16), onehot.astype(jnp.bfloat16),
                                     preferred_element_type=jnp.float32)
    rank1 = jnp.sum(jnp.where(hit1, before, 0.0), axis=-1, keepdims=True)
    rank2 = jnp.sum(jnp.where(hit2, before, 0.0), axis=-1, keepdims=True)
    counts_s[...] += jnp.sum(onehot, axis=0, keepdims=True)
    counts_ref[...] = counts_s[...].astype(jnp.int32)
    f32 = jnp.float32
    info = jnp.where(lane == 0, e1.astype(f32), jnp.where(lane == 1, e2.astype(f32),
                     jnp.where(lane == 2, rank1, jnp.where(lane == 3, rank2, 0.0))))
    info_ref[...] = info.T[:SUBLANES, :]


def _outproj_router(mixed, hp, wo, g_ffn, wr, real_rows):
    tp, d = hp.shape
    tm = _pick(tp, (512, 256, 128))
    row = lambda w: pl.BlockSpec((tm, w), lambda i: (i, 0))
    return pl.pallas_call(
        _outproj_router_kernel,
        grid=(tp // tm,),
        in_specs=[row(d), row(d),
                  pl.BlockSpec((d, d), lambda i: (0, 0)),
                  pl.BlockSpec((1, d), lambda i: (0, 0)),
                  pl.BlockSpec((d, LANES), lambda i: (0, 0)),
                  row(1)],
        out_specs=[row(d), row(d), row(MOE_TOPK),
                   pl.BlockSpec((SUBLANES, tm), lambda i: (0, i)),
                   pl.BlockSpec((1, LANES), lambda i: (0, 0))],
        out_shape=[jax.ShapeDtypeStruct((tp, d), jnp.float32),
                   jax.ShapeDtypeStruct((tp, d), jnp.float32),
                   jax.ShapeDtypeStruct((tp, MOE_TOPK), jnp.float32),
                   jax.ShapeDtypeStruct((SUBLANES, tp), jnp.float32),
                   jax.ShapeDtypeStruct((1, LANES), jnp.int32)],
        scratch_shapes=[pltpu.VMEM((1, LANES), jnp.float32)],
        compiler_params=_params(("arbitrary",)),
        name="outproj_router",
    )(mixed, hp, wo, g_ffn, wr, real_rows)


def _wait_rows(n, make_copy):
    for k in range(EXPERT_ROWS.bit_length() - 1, -1, -1):
        @pl.when((n & (1 << k)) != 0)
        def _():
            make_copy(1 << k).wait()


def _expert_kernel(layout, n_items_ref, n_used_ref, item_e_ref, item_start_ref, item_rows_ref, item_real_ref,
                   dest_ref, u_hbm, w1_ref, w3_ref, w2_ref, y_hbm,
                   xg_s, xb_s, acc_s, tok_ref, sem):
    n_batch, l_real, lp, n_pad = layout
    i = pl.program_id(0)
    f = pl.program_id(1)
    nf = pl.num_programs(1)
    n_items = n_items_ref[0]
    active = i < n_items
    n_rows = item_rows_ref[i]
    d = acc_s.shape[1]
    blk = MOE_BLOCK

    def issue_gather(item):
        start = item_start_ref[item]
        n = item_real_ref[item]
        n_groups = n // GATHER_UNROLL

        def one(r):
            pltpu.make_async_copy(u_hbm.at[pl.ds(tok_ref[start + r], 1)], xg_s.at[pl.ds(r, 1)], sem.at[0]).start()

        def group(g, c):
            for j in range(GATHER_UNROLL):
                one(g * GATHER_UNROLL + j)
            return c
        lax.fori_loop(0, n_groups, group, 0)

        def single(r, c):
            one(r)
            return c
        lax.fori_loop(n_groups * GATHER_UNROLL, n, single, 0)

    def wait_gather(item):
        _wait_rows(item_real_ref[item],
                   lambda m: pltpu.make_async_copy(u_hbm.at[pl.ds(0, m)], xg_s.at[pl.ds(0, m)], sem.at[0]))

    def out_copy(item, j):
        r0 = pl.multiple_of(j * blk, blk)
        dst0 = pl.multiple_of(item_start_ref[item] + r0, blk)
        return pltpu.make_async_copy(acc_s.at[pl.ds(r0, blk)], y_hbm.at[pl.ds(dst0, blk)], sem.at[1])

    def issue_out(item):
        def body(j, c):
            out_copy(item, j).start()
            return c
        lax.fori_loop(0, item_rows_ref[item] // blk, body, 0)

    def wait_out(item):
        def body(j, c):
            out_copy(item, j).wait()
            return c
        lax.fori_loop(0, item_rows_ref[item] // blk, body, 0)

    @pl.when((i == 0) & (f == 0))
    def _():
        xg_s[...] = jnp.zeros_like(xg_s)
        acc_s[pl.ds(0, blk), :] = jnp.zeros((blk, d), acc_s.dtype)
        n_used = n_used_ref[0]
        n_blocks = y_hbm.shape[0] // blk

        def tail_copy(j):
            return pltpu.make_async_copy(acc_s.at[pl.ds(0, blk)], y_hbm.at[pl.ds(pl.multiple_of(j * blk, blk), blk)],
                                         sem.at[1])

        def fill(j, c):
            tail_copy(j).start()
            return c
        lax.fori_loop(n_used, n_blocks, fill, 0)

        def drain(j, c):
            tail_copy(j).wait()
            return c
        lax.fori_loop(n_used, n_blocks, drain, 0)

        for k in range(MOE_TOPK):
            for b in range(n_batch):
                base = (k * n_batch + b) * l_real
                row0 = b * lp + n_pad

                def invert(g, c, base=base, row0=row0):
                    for j in range(GATHER_UNROLL):
                        p = g * GATHER_UNROLL + j
                        tok_ref[dest_ref[base + p]] = row0 + p
                    return c
                lax.fori_loop(0, l_real // GATHER_UNROLL, invert, 0)

        @pl.when(active)
        def _():
            issue_gather(0)

    @pl.when(active & (f == 0))
    def _():
        wait_gather(i)
        for lo, m in zip((0,) + EXPERT_ROW_VARIANTS, EXPERT_ROW_VARIANTS):
            @pl.when((n_rows > lo) & (n_rows <= m))
            def _():
                xb_s[pl.ds(0, m), :] = xg_s[pl.ds(0, m), :].astype(xb_s.dtype)

        @pl.when(i + 1 < n_items)
        def _():
            issue_gather(i + 1)

    @pl.when((f == 0) & (i >= 1) & (i - 1 < n_items) & jnp.logical_not(active))
    def _():
        wait_out(i - 1)

    for lo, m in zip((0,) + EXPERT_ROW_VARIANTS, EXPERT_ROW_VARIANTS):
        @pl.when(active & (n_rows > lo) & (n_rows <= m))
        def _():
            rows = pl.ds(0, m)
            x = xb_s[rows, :]
            a = jnp.dot(x, w1_ref[0].astype(MXU_DTYPE), preferred_element_type=jnp.float32)
            b = jnp.dot(x, w3_ref[0].astype(MXU_DTYPE), preferred_element_type=jnp.float32)
            hdn = (_silu(a) * b).astype(MXU_DTYPE)

            @pl.when((f == 0) & (i >= 1))
            def _():
                wait_out(i - 1)

            for c0 in range(0, d, EXPERT_OUT_TILE):
                cols = pl.ds(c0, EXPERT_OUT_TILE)
                part = jnp.dot(hdn, w2_ref[0, :, cols].astype(MXU_DTYPE), preferred_element_type=jnp.float32)

                @pl.when(f == 0)
                def _():
                    acc_s[rows, cols] = part

                @pl.when(f > 0)
                def _():
                    acc_s[rows, cols] += part

    @pl.when(f == nf - 1)
    def _():
        @pl.when(active)
        def _():
            issue_out(i)

        @pl.when(active & (i == pl.num_programs(0) - 1))
        def _():
            wait_out(i)


def _experts(u, sched, w1, w3, w2, n_items_max, p_len, layout):
    tp, d = u.shape
    ff = w1.shape[2]
    tf = EXPERT_FF_TILE
    nf = ff // tf
    n_items, n_used, item_e, item_start, item_rows, item_real, dest = sched
    assert layout[1] % GATHER_UNROLL == 0

    def w_in_map(i, f, n_items, n_used, item_e, *_):
        return (item_e[i], 0, jnp.where(i < n_items[0], f, nf - 1))

    def w_out_map(i, f, n_items, n_used, item_e, *_):
        return (item_e[i], jnp.where(i < n_items[0], f, nf - 1), 0)

    grid_spec = pltpu.PrefetchScalarGridSpec(
        num_scalar_prefetch=7,
        grid=(n_items_max, nf),
        in_specs=[
            pl.BlockSpec(memory_space=pl.ANY),
            pl.BlockSpec((1, d, tf), w_in_map),
            pl.BlockSpec((1, d, tf), w_in_map),
            pl.BlockSpec((1, tf, d), w_out_map),
        ],
        out_specs=pl.BlockSpec(memory_space=pl.ANY),
        scratch_shapes=[
            pltpu.VMEM((EXPERT_ROWS, d), jnp.float32),
            pltpu.VMEM((EXPERT_ROWS, d), MXU_DTYPE),
            pltpu.VMEM((EXPERT_ROWS, d), jnp.float32),
            pltpu.SMEM((p_len,), jnp.int32),
            pltpu.SemaphoreType.DMA((2,)),
        ],
    )
    return pl.pallas_call(
        functools.partial(_expert_kernel, layout),
        grid_spec=grid_spec,
        out_shape=jax.ShapeDtypeStruct((p_len, d), jnp.float32),
        compiler_params=_params(("arbitrary", "arbitrary")),
        name="experts",
    )(n_items, n_used, item_e, item_start, item_rows, item_real, dest, u, w1, w3, w2)


def _expert_schedule(info, counts, batch, lp, n_pad, n_items_max, p_len):
    i32 = jnp.int32
    eid = info[:MOE_TOPK].astype(i32)
    rank = info[MOE_TOPK:2 * MOE_TOPK].astype(i32)
    padded = (counts + MOE_BLOCK - 1) // MOE_BLOCK * MOE_BLOCK
    pad_end = jnp.cumsum(padded)
    pad_start = pad_end - padded
    experts = jnp.arange(MOE_EXPERTS, dtype=i32)[:, None, None]
    seg_start = jnp.sum(jnp.where(eid[None] == experts, pad_start[:, None, None], 0), axis=0)
    dest = (seg_start + rank).reshape(MOE_TOPK, batch, lp)[:, :, n_pad:]
    n_used = (pad_end[-1:] // MOE_BLOCK).astype(i32)

    chunks = (padded + EXPERT_ROWS - 1) // EXPERT_ROWS
    chunk_end = jnp.cumsum(chunks)
    item = jnp.arange(n_items_max, dtype=i32)
    item_e = jnp.minimum(jnp.sum((chunk_end[None, :] <= item[:, None]).astype(i32), axis=1), MOE_EXPERTS - 1)
    n_items = chunk_end[-1:].astype(i32)
    last_e = item_e[jnp.maximum(n_items[0] - 1, 0)]
    item_e = jnp.where(item < n_items[0], item_e, last_e)
    k = item - (chunk_end - chunks)[item_e]
    item_start = (pad_start[item_e] + k * EXPERT_ROWS).astype(i32)
    item_rows = jnp.clip(padded[item_e] - k * EXPERT_ROWS, 0, EXPERT_ROWS).astype(i32)
    item_real = jnp.clip(counts[item_e] - k * EXPERT_ROWS, 0, EXPERT_ROWS).astype(i32)
    live = item < n_items[0]
    item_start = jnp.where(live, item_start, 0)
    item_rows = jnp.where(live, item_rows, 0)
    item_real = jnp.where(live, item_real, 0)
    return (n_items, n_used, item_e, item_start, item_rows, item_real, dest.reshape(-1)), dest


def _final_kernel(tm, dest_ref, h_ref, gate_ref, g_ref, ys_hbm, o_ref, ybuf, sem):
    t = pl.program_id(0)
    n_tiles = pl.num_programs(0)

    def issue(tile, buf):
        n_x = n_tiles * tm
        for j in range(tm):
            for k in range(MOE_TOPK):
                pos = dest_ref[k * n_x + tile * tm + j]
                pltpu.make_async_copy(ys_hbm.at[pl.ds(pos, 1)], ybuf.at[buf, k, pl.ds(j, 1)], sem.at[buf]).start()

    @pl.when(t == 0)
    def _():
        issue(0, 0)

    @pl.when(t + 1 < n_tiles)
    def _():
        issue(t + 1, (t + 1) % 2)

    buf = t % 2
    for k in range(MOE_TOPK):
        pltpu.make_async_copy(ys_hbm.at[pl.ds(0, tm)], ybuf.at[buf, k], sem.at[buf]).wait()
    moe = gate_ref[:, 0:1] * ybuf[buf, 0] + gate_ref[:, 1:2] * ybuf[buf, 1]
    h = h_ref[...] + moe
    o_ref[...] = h * lax.rsqrt(jnp.mean(h * h, axis=-1, keepdims=True) + EPS) * g_ref[...]


def _final(h2, ys, dest_x, gates, g_final, batch, lp, seq):
    tp, d = h2.shape
    tm = SSD_CHUNK
    skip = (lp - seq) // tm
    per_b = seq // tm
    src = lambda w: pl.BlockSpec((tm, w), lambda t, dest: ((t // per_b) * (lp // tm) + skip + t % per_b, 0))
    grid_spec = pltpu.PrefetchScalarGridSpec(
        num_scalar_prefetch=1,
        grid=(batch * per_b,),
        in_specs=[src(d), src(MOE_TOPK), pl.BlockSpec((1, d), lambda t, dest: (0, 0)),
                  pl.BlockSpec(memory_space=pl.ANY)],
        out_specs=pl.BlockSpec((tm, d), lambda t, dest: (t, 0)),
        scratch_shapes=[pltpu.VMEM((2, MOE_TOPK, tm, d), jnp.float32), pltpu.SemaphoreType.DMA((2,))],
    )
    return pl.pallas_call(
        functools.partial(_final_kernel, tm),
        grid_spec=grid_spec,
        out_shape=jax.ShapeDtypeStruct((batch * seq, d), jnp.float32),
        compiler_params=_params(("arbitrary",)),
        name="final_norm",
    )(dest_x, h2, gates, g_final, ys)


def kernel(x, meta_tokens, norm_mix, w_in, ssd_conv_w, ssd_conv_b, ssd_dt_bias, ssd_a_log, ssd_d, ssd_norm, w_ssd_out, lru_conv_w, lru_conv_b, lru_wa, lru_ba, lru_wx, lru_bx, lru_lambda, w_lru_out, gate_bias, w_out, norm_ffn, w_router_group, w_router_expert, w_exp_gate, w_exp_up, w_exp_down, norm_final):
    batch, seq, d = x.shape
    depth = norm_mix.shape[0]
    assert depth == 1 and seq % SSD_CHUNK == 0
    l = N_META + seq
    lp = -(-l // SSD_CHUNK) * SSD_CHUNK
    n_pad = lp - l
    d_inner = d
    heads = d_inner // SSD_HEAD_DIM
    conv_dim = d_inner + 2 * SSD_GROUPS * SSD_STATE
    col_xbc = d_inner
    col_dt = col_xbc + conv_dim
    col_lx = col_dt + heads
    f32 = jnp.float32
    lyr = 0

    meta = jnp.broadcast_to(meta_tokens.astype(f32)[None], (batch, N_META, d))
    hp = jnp.concatenate([jnp.zeros((batch, n_pad, d), f32), meta, x], axis=1).reshape(batch * lp, d)

    w = w_in[lyr]
    w_a = w[:, :col_dt].astype(MXU_DTYPE)
    w_b = w[:, col_lx:].astype(MXU_DTYPE)
    w_dt = jnp.zeros((d, LANES), f32).at[:, :heads].set(w[:, col_dt:col_lx]).astype(MXU_DTYPE)
    dtb = jnp.zeros((1, LANES), f32).at[0, :heads].set(ssd_dt_bias[lyr])
    alog = jnp.zeros((1, LANES), f32).at[0, :heads].set(ssd_a_log[lyr])
    proj, dt, acum = _norm_inproj(hp, norm_mix[lyr][None, :], w_a, w_b, w_dt, dtb, alog)
    n_main = w_a.shape[1] + w_b.shape[1]
    proj3 = proj.reshape(batch, lp, n_main)
    lx_col = col_dt
    ly_col = lx_col + d
    gl_col = ly_col + d

    ys = _ssd(proj3, dt.reshape(batch, lp, LANES), acum.reshape(batch, lp, LANES), n_pad, ssd_conv_w[lyr],
              ssd_conv_b[lyr], ssd_d[lyr], ssd_norm[lyr], d_inner)
    yl = _lru(proj3, n_pad, lx_col, ly_col, lru_conv_w[lyr], lru_conv_b[lyr], lru_wa[lyr].astype(MXU_DTYPE),
              lru_ba[lyr], lru_wx[lyr].astype(MXU_DTYPE), lru_bx[lyr], lru_lambda[lyr], d)

    mixed = _merge(ys.reshape(batch * lp, d), yl.reshape(batch * lp, d), w_ssd_out[lyr].astype(MXU_DTYPE),
                   w_lru_out[lyr].astype(MXU_DTYPE), proj, gl_col, gate_bias[lyr])

    w_router = jnp.zeros((d, LANES), f32)
    w_router = w_router.at[:, :MOE_GROUPS].set(w_router_group[lyr])
    w_router = w_router.at[:, MOE_GROUPS:MOE_GROUPS + MOE_EXPERTS].set(w_router_expert[lyr])
    real_rows = jnp.asarray(((np.arange(batch * lp) % lp) >= n_pad).astype(np.float32)[:, None])
    h2, u2, gates, info, counts = _outproj_router(mixed, hp, w_out[lyr].astype(MXU_DTYPE), norm_ffn[lyr][None, :],
                                                  w_router.astype(MXU_DTYPE), real_rows)

    n_asg = batch * l * MOE_TOPK
    p_max = n_asg + MOE_EXPERTS * (MOE_BLOCK - 1)
    n_items_max = MOE_EXPERTS + p_max // EXPERT_ROWS
    p_len = -(-p_max // MOE_BLOCK) * MOE_BLOCK
    sched, dest = _expert_schedule(info, counts[0, :MOE_EXPERTS], batch, lp, n_pad, n_items_max, p_len)
    ff = w_exp_gate.shape[-1]
    ys_sorted = _experts(u2, sched, w_exp_gate.reshape(MOE_EXPERTS, d, ff), w_exp_up.reshape(MOE_EXPERTS, d, ff),
                         w_exp_down.reshape(MOE_EXPERTS, ff, d), n_items_max, p_len, (batch, l, lp, n_pad))

    dest_x = dest[:, :, N_META:].reshape(-1)
    out = _final(h2, ys_sorted, dest_x, gates, norm_final[None, :], batch, lp, seq)
    return out.reshape(batch, seq, d)
```

```python
import functools
import math

import numpy as np
import jax
import jax.numpy as jnp
from jax import lax
from jax.experimental import pallas as pl
from jax.experimental.pallas import tpu as pltpu

N_META = 16
CONV_K = 4
EPS = 1e-6
SSD_HEAD_DIM = 64
SSD_HEAD_SHIFT = 6
SSD_GROUPS = 8
SSD_STATE = 128
SSD_CHUNK = 128
LRU_HEADS = 8
LRU_C = 8.0
N_BRANCH = 2
MOE_GROUPS = 8
MOE_EXP_PER_GROUP = 8
MOE_EXPERTS = MOE_GROUPS * MOE_EXP_PER_GROUP
MOE_TOPK = 2
MOE_BLOCK = 64

LANES = 128
SUBLANES = 8
VMEM_LIMIT = 56 * 1024 * 1024
MXU_DTYPE = jnp.bfloat16

EXPERT_ROWS = 1024
EXPERT_ROW_VARIANTS = (256, 512, 576, 640, 1024)
EXPERT_FF_TILE = 512
EXPERT_OUT_TILE = 1024
GATHER_UNROLL = 8
SSD_UNROLL = 2


def _pick(n, options):
    for o in options:
        if n % o == 0:
            return o
    raise ValueError(f"no tile in {options} divides {n}")


def _params(sem, vmem=VMEM_LIMIT):
    return pltpu.CompilerParams(dimension_semantics=sem, vmem_limit_bytes=vmem)


def _dot(a, b):
    return jnp.dot(a.astype(MXU_DTYPE), b.astype(MXU_DTYPE), preferred_element_type=jnp.float32)


def _dot_exact_rhs(v, sel):
    sel = sel.astype(jnp.bfloat16)
    hi = v.astype(jnp.bfloat16)
    r1 = v - hi.astype(jnp.float32)
    mid = r1.astype(jnp.bfloat16)
    lo = (r1 - mid.astype(jnp.float32)).astype(jnp.bfloat16)
    f32 = jnp.float32
    return (jnp.dot(hi, sel, preferred_element_type=f32) + jnp.dot(mid, sel, preferred_element_type=f32)
            + jnp.dot(lo, sel, preferred_element_type=f32))


def _dot_exact_lhs(sel, v):
    sel = sel.astype(jnp.bfloat16)
    hi = v.astype(jnp.bfloat16)
    r1 = v - hi.astype(jnp.float32)
    mid = r1.astype(jnp.bfloat16)
    lo = (r1 - mid.astype(jnp.float32)).astype(jnp.bfloat16)
    f32 = jnp.float32
    return (jnp.dot(sel, hi, preferred_element_type=f32) + jnp.dot(sel, mid, preferred_element_type=f32)
            + jnp.dot(sel, lo, preferred_element_type=f32))


def _softplus(x):
    return jnp.maximum(x, 0.0) + jnp.log1p(jnp.exp(-jnp.abs(x)))


def _sigmoid(x):
    return 1.0 / (1.0 + jnp.exp(-x))


def _silu(x):
    return x * _sigmoid(x)


def _causal_conv_chunk(x_ref, w_ref, b_ref, c):
    q = SSD_CHUNK
    r0 = pl.multiple_of(c * q, q)
    cur = x_ref[0, pl.ds(r0, q), :]
    prev = x_ref[0, pl.ds(pl.multiple_of(jnp.maximum(r0 - SUBLANES, 0), SUBLANES), SUBLANES), :]
    prev = jnp.where(c > 0, prev, 0.0)
    sub = lax.broadcasted_iota(jnp.int32, prev.shape, 0)
    acc = b_ref[...] + w_ref[CONV_K - 1:CONV_K, :] * cur
    for k in range(1, CONV_K):
        rolled = pltpu.roll(cur, k, 0)
        head = jnp.where(sub < k, pltpu.roll(prev, k, 0), rolled[:SUBLANES])
        shifted = jnp.concatenate([head, rolled[SUBLANES:]], axis=0)
        acc = acc + w_ref[CONV_K - 1 - k:CONV_K - k, :] * shifted
    return acc


def _norm_inproj_kernel(n_a, x_ref, g_ref, wa_ref, wb_ref, wdt_ref, dtb_ref, alog_ref, o_ref, dt_ref, acum_ref, xn_ref):
    j = pl.program_id(1)

    @pl.when(j == 0)
    def _():
        x = x_ref[...]
        y = x * lax.rsqrt(jnp.mean(x * x, axis=-1, keepdims=True) + EPS)
        xn_ref[...] = (y * g_ref[...]).astype(xn_ref.dtype)
        raw = jnp.dot(xn_ref[...], wdt_ref[...], preferred_element_type=jnp.float32)
        dt = _softplus(raw + dtb_ref[...])
        dt_ref[...] = dt
        adt = dt * (-jnp.exp(alog_ref[...]))
        q = SSD_CHUNK
        tri = (lax.broadcasted_iota(jnp.int32, (q, q), 0) >= lax.broadcasted_iota(jnp.int32, (q, q), 1))
        for c in range(x.shape[0] // q):
            acum_ref[c * q:(c + 1) * q, :] = _dot_exact_lhs(tri.astype(jnp.float32), adt[c * q:(c + 1) * q, :])

    @pl.when(j < n_a)
    def _():
        o_ref[...] = jnp.dot(xn_ref[...], wa_ref[...], preferred_element_type=jnp.float32)

    @pl.when(j >= n_a)
    def _():
        o_ref[...] = jnp.dot(xn_ref[...], wb_ref[...], preferred_element_type=jnp.float32)


def _norm_inproj(hp, g, w_a, cols_a, w_b, wdt, dtb, alog):
    tp, d = hp.shape
    n = cols_a + w_b.shape[1]
    tm = _pick(tp, (1024, 512, 256, 128))
    tn = math.gcd(_pick(cols_a, (1024, 512, 256, 128)), _pick(w_b.shape[1], (1024, 512, 256, 128)))
    n_a = cols_a // tn
    vec = pl.BlockSpec((1, LANES), lambda i, j: (0, 0))
    return pl.pallas_call(
        functools.partial(_norm_inproj_kernel, n_a),
        grid=(tp // tm, n // tn),
        in_specs=[
            pl.BlockSpec((tm, d), lambda i, j: (i, 0)),
            pl.BlockSpec((1, d), lambda i, j: (0, 0)),
            pl.BlockSpec((d, tn), lambda i, j: (0, jnp.minimum(j, n_a - 1))),
            pl.BlockSpec((d, tn), lambda i, j: (0, jnp.maximum(j - n_a, 0))),
            pl.BlockSpec((d, LANES), lambda i, j: (0, 0)),
            vec, vec,
        ],
        out_specs=[
            pl.BlockSpec((tm, tn), lambda i, j: (i, j)),
            pl.BlockSpec((tm, LANES), lambda i, j: (i, 0)),
            pl.BlockSpec((tm, LANES), lambda i, j: (i, 0)),
        ],
        out_shape=[
            jax.ShapeDtypeStruct((tp, n), jnp.float32),
            jax.ShapeDtypeStruct((tp, LANES), jnp.float32),
            jax.ShapeDtypeStruct((tp, LANES), jnp.float32),
        ],
        scratch_shapes=[pltpu.VMEM((tm, d), MXU_DTYPE)],
        compiler_params=_params(("parallel", "arbitrary")),
        name="norm_inproj",
    )(hp, g, w_a, w_b, wdt, dtb, alog)


def _ssd_kernel(n_pad, z_ref, xs_ref, b_ref, c_ref, dt_ref, acum_ref,
                cwx_ref, cwb_ref, cwc_ref, cbx_ref, cbb_ref, cbc_ref,
                d_ref, nw_ref, o_ref, state_s):
    g = pl.program_id(1)
    lp = xs_ref.shape[1]
    q = SSD_CHUNK
    gw = xs_ref.shape[2]
    hpg = gw // SSD_HEAD_DIM
    f32 = jnp.float32

    rows = lax.broadcasted_iota(jnp.int32, (LANES, gw), 0)
    cols = lax.broadcasted_iota(jnp.int32, (LANES, gw), 1)
    expand = (rows == g * hpg + lax.shift_right_logical(cols, SSD_HEAD_SHIFT)).astype(f32)

    state_s[...] = jnp.zeros_like(state_s)
    li = lax.broadcasted_iota(jnp.int32, (q, q), 0)
    si = lax.broadcasted_iota(jnp.int32, (q, q), 1)
    causal = li >= si
    lane = lax.broadcasted_iota(jnp.int32, (q, gw), 1)
    row = lax.broadcasted_iota(jnp.int32, (q, gw), 0)

    def chunk(c, carry):
        r0 = pl.multiple_of(c * q, q)
        sl = pl.ds(r0, q)
        xs = _silu(_causal_conv_chunk(xs_ref, cwx_ref, cbx_ref, c))
        bm = _silu(_causal_conv_chunk(b_ref, cwb_ref, cbb_ref, c))
        cm = _silu(_causal_conv_chunk(c_ref, cwc_ref, cbc_ref, c))
        dt_g = _dot_exact_rhs(dt_ref[0, sl, :], expand)
        acum = _dot_exact_rhs(acum_ref[0, sl, :], expand)
        xdt = jnp.where(row + r0 >= n_pad, xs * dt_g, 0.0)
        acum_t = acum.T
        a_last = acum[q - 1:q, :]

        cb = lax.dot_general(cm.astype(MXU_DTYPE), bm.astype(MXU_DTYPE),
                             (((1,), (1,)), ((), ())), preferred_element_type=f32)
        y = d_ref[...] * xs
        for j in range(hpg):
            c0 = j * SSD_HEAD_DIM
            seg = acum[:, c0:c0 + 1] - acum_t[c0:c0 + 1, :]
            decay = jnp.exp(jnp.where(causal, seg, -jnp.inf))
            in_head = (lane >= c0) & (lane < c0 + SSD_HEAD_DIM)
            y = y + _dot(cb * decay, jnp.where(in_head, xdt, 0.0))
        x_end = xdt * jnp.exp(a_last - acum)

        state = state_s[...]
        y = y + _dot(cm, state) * jnp.exp(acum)
        state_s[...] = state * jnp.exp(a_last) + _dot(bm.T, x_end)

        yz = y * _silu(z_ref[0, sl, :])
        yn = yz * lax.rsqrt(jnp.mean(yz * yz, axis=-1, keepdims=True) + EPS)
        o_ref[0, sl, :] = (yn * nw_ref[...]).astype(o_ref.dtype)
        return carry

    lax.fori_loop(0, lp // q, chunk, 0, unroll=2 * SSD_UNROLL)


def _ssd(proj3, dt3, acum3, n_pad, cw, cb, d_skip, norm_w, d_inner):
    b, lp, _ = proj3.shape
    gw = d_inner // SSD_GROUPS
    ns = SSD_STATE
    xs_blk0 = d_inner // gw
    b_blk0 = 2 * d_inner // ns
    c_blk0 = (2 * d_inner + SSD_GROUPS * ns) // ns

    cwx = cw[:, :d_inner]
    cwb = cw[:, d_inner:d_inner + SSD_GROUPS * ns]
    cwc = cw[:, d_inner + SSD_GROUPS * ns:]
    cbx = cb[None, :d_inner]
    cbb = cb[None, d_inner:d_inner + SSD_GROUPS * ns]
    cbc = cb[None, d_inner + SSD_GROUPS * ns:]
    d_ch = jnp.repeat(d_skip, SSD_HEAD_DIM)[None, :]
    nw = norm_w[None, :]

    seq = lambda w, off: pl.BlockSpec((1, lp, w), lambda i, g: (i, 0, off + g))
    head = pl.BlockSpec((1, lp, LANES), lambda i, g: (i, 0, 0))
    vec = lambda w: pl.BlockSpec((1, w), lambda i, g: (0, g))
    cwspec = lambda w: pl.BlockSpec((CONV_K, w), lambda i, g: (0, g))
    return pl.pallas_call(
        functools.partial(_ssd_kernel, n_pad),
        grid=(b, SSD_GROUPS),
        in_specs=[
            seq(gw, 0), seq(gw, xs_blk0), seq(ns, b_blk0), seq(ns, c_blk0), head, head,
            cwspec(gw), cwspec(ns), cwspec(ns), vec(gw), vec(ns), vec(ns),
            vec(gw), vec(gw),
        ],
        out_specs=pl.BlockSpec((1, lp, gw), lambda i, g: (i, 0, g)),
        out_shape=jax.ShapeDtypeStruct((b, lp, d_inner), MXU_DTYPE),
        scratch_shapes=[pltpu.VMEM((ns, gw), jnp.float32)],
        compiler_params=_params(("parallel", "parallel")),
        name="ssd",
    )(proj3, proj3, proj3, proj3, dt3, acum3, cwx, cwb, cwc, cbx, cbb, cbc, d_ch, nw)


def _lru_kernel(n_pad, lx_ref, ly_ref, cw_ref, cb_ref, wa_ref, ba_ref, wx_ref, bx_ref, lam_ref, o_ref):
    lp = lx_ref.shape[1]
    w = lx_ref.shape[2]
    q = SSD_CHUNK
    neg_c_softplus = (-LRU_C) * _softplus(-lam_ref[...])
    row = lax.broadcasted_iota(jnp.int32, (q, w), 0)
    sub = lax.broadcasted_iota(jnp.int32, (SUBLANES, w), 0)

    def chunk(c, h_prev):
        r0 = pl.multiple_of(c * q, q)
        sl = pl.ds(r0, q)
        xr = _causal_conv_chunk(lx_ref, cw_ref, cb_ref, c)
        gate_r = _sigmoid(_dot(xr, wa_ref[0]) + ba_ref[...])
        gate_i = _sigmoid(_dot(xr, wx_ref[0]) + bx_ref[...])
        log_a = gate_r * neg_c_softplus
        a = jnp.exp(log_a)
        mult = jnp.sqrt(jnp.tanh(-log_a) * (a * a + 1.0))
        u = jnp.where(row + r0 >= n_pad, mult * gate_i * xr, 0.0)
        gate_y = jax.nn.gelu(ly_ref[0, sl, :])

        out = []
        for t in range(q // SUBLANES):
            rows8 = slice(t * SUBLANES, (t + 1) * SUBLANES)
            at, ut = a[rows8], u[rows8]
            for d in (1, 2, 4):
                keep = sub >= d
                ut = jnp.where(keep, at * pltpu.roll(ut, d, 0) + ut, ut)
                at = jnp.where(keep, at * pltpu.roll(at, d, 0), at)
            h = at * h_prev + ut
            h_prev = jnp.broadcast_to(h[SUBLANES - 1:SUBLANES, :], (SUBLANES, w))
            out.append(h * gate_y[rows8])
        o_ref[0, sl, :] = jnp.concatenate(out, axis=0).astype(o_ref.dtype)
        return h_prev

    lax.fori_loop(0, lp // q, chunk, jnp.zeros((SUBLANES, w), jnp.float32), unroll=SSD_UNROLL)


def _lru(proj3, n_pad, lx_col, ly_col, cw, cb, wa, ba, wx, bx, lam, width):
    b, lp, _ = proj3.shape
    w = width // LRU_HEADS
    seq = lambda off: pl.BlockSpec((1, lp, w), lambda i, h: (i, 0, off + h))
    vec = pl.BlockSpec((1, w), lambda i, h: (0, h))
    mat = pl.BlockSpec((1, w, w), lambda i, h: (h, 0, 0))
    return pl.pallas_call(
        functools.partial(_lru_kernel, n_pad),
        grid=(b, LRU_HEADS),
        in_specs=[seq(lx_col // w), seq(ly_col // w),
                  pl.BlockSpec((CONV_K, w), lambda i, h: (0, h)), vec, mat, vec, mat, vec, vec],
        out_specs=pl.BlockSpec((1, lp, w), lambda i, h: (i, 0, h)),
        out_shape=jax.ShapeDtypeStruct((b, lp, width), MXU_DTYPE),
        compiler_params=_params(("parallel", "parallel")),
        name="rglru",
    )(proj3, proj3, cw, cb[None, :], wa, ba[None, :], wx, bx[None, :], lam[None, :])


def _merge_kernel(ys_ref, yl_ref, ws_ref, wl_ref, g0_ref, g1_ref, gb_ref, o_ref):
    y_ssd = jnp.dot(ys_ref[...], ws_ref[...], preferred_element_type=jnp.float32)
    y_lru = jnp.dot(yl_ref[...], wl_ref[...], preferred_element_type=jnp.float32)
    gate0 = _sigmoid(g0_ref[...] + gb_ref[0:1, :])
    gate1 = _sigmoid(g1_ref[...] + gb_ref[1:2, :])
    o_ref[...] = (gate0 * y_ssd + gate1 * y_lru).astype(o_ref.dtype)


def _merge(ys, yl, ws, wl, proj, gate_col, gate_bias):
    tp, d = ys.shape
    tm = _pick(tp, (1024, 512, 256, 128))
    tn = _pick(d, (512, 256, 128))
    g0 = gate_col // tn
    g1 = (gate_col + d) // tn
    return pl.pallas_call(
        _merge_kernel,
        grid=(tp // tm, d // tn),
        in_specs=[
            pl.BlockSpec((tm, d), lambda i, j: (i, 0)),
            pl.BlockSpec((tm, d), lambda i, j: (i, 0)),
            pl.BlockSpec((d, tn), lambda i, j: (0, j)),
            pl.BlockSpec((d, tn), lambda i, j: (0, j)),
            pl.BlockSpec((tm, tn), lambda i, j: (i, g0 + j)),
            pl.BlockSpec((tm, tn), lambda i, j: (i, g1 + j)),
            pl.BlockSpec((N_BRANCH, tn), lambda i, j: (0, j)),
        ],
        out_specs=pl.BlockSpec((tm, tn), lambda i, j: (i, j)),
        out_shape=jax.ShapeDtypeStruct((tp, d), MXU_DTYPE),
        compiler_params=_params(("parallel", "arbitrary")),
        name="merge",
    )(ys, yl, ws, wl, proj, proj, gate_bias)


def _first_index_of_max(p, valid, lane):
    pm = jnp.where(valid, p, -jnp.inf)
    top = jnp.max(pm, axis=-1, keepdims=True)
    idx = jnp.min(jnp.where(valid & (pm == top), lane, LANES), axis=-1, keepdims=True)
    return top, idx


def _masked_softmax(x, valid):
    m = jnp.max(jnp.where(valid, x, -jnp.inf), axis=-1, keepdims=True)
    e = jnp.where(valid, jnp.exp(x - m), 0.0)
    return e / jnp.sum(e, axis=-1, keepdims=True)


def _outproj_router_kernel(mix_ref, h_ref, wo_ref, g_ref, wr_ref, real_ref,
                           h2_ref, u_ref, gate_ref, info_ref, counts_ref, counts_s):
    @pl.when(pl.program_id(0) == 0)
    def _():
        counts_s[...] = jnp.zeros_like(counts_s)

    h2 = h_ref[...] + jnp.dot(mix_ref[...], wo_ref[...], preferred_element_type=jnp.float32)
    h2_ref[...] = h2
    u = h2 * lax.rsqrt(jnp.mean(h2 * h2, axis=-1, keepdims=True) + EPS) * g_ref[...]
    u_ref[...] = u
    logits = _dot(u, wr_ref[...])
    lane = lax.broadcasted_iota(jnp.int32, logits.shape, 1)

    g_prob = _masked_softmax(logits, lane < MOE_GROUPS)
    g_p, g_idx = _first_index_of_max(g_prob, lane < MOE_GROUPS, lane)

    e_lo = MOE_GROUPS + g_idx * MOE_EXP_PER_GROUP
    in_group = (lane >= e_lo) & (lane < e_lo + MOE_EXP_PER_GROUP)
    e_prob = _masked_softmax(logits, in_group)
    p1, i1 = _first_index_of_max(e_prob, in_group, lane)
    rest = in_group & (lane != i1)
    p2, i2 = _first_index_of_max(e_prob, rest, lane)
    denom = p1 + p2

    col = lax.broadcasted_iota(jnp.int32, gate_ref.shape, 1)
    e1 = i1 - MOE_GROUPS
    e2 = i2 - MOE_GROUPS
    gate_ref[...] = jnp.where(col == 0, g_p * p1 / denom, g_p * p2 / denom)

    m = logits.shape[0]
    hit1 = (lane == e1) & (real_ref[...] > 0.0)
    hit2 = (lane == e2) & (real_ref[...] > 0.0)
    onehot = jnp.where(hit1 | hit2, 1.0, 0.0)
    earlier = (lax.broadcasted_iota(jnp.int32, (m, m), 0) > lax.broadcasted_iota(jnp.int32, (m, m), 1))
    before = counts_s[...] + jnp.dot(earlier.astype(jnp.bfloat16), onehot.astype(jnp.bfloat16),
                                     preferred_element_type=jnp.float32)
    rank1 = jnp.sum(jnp.where(hit1, before, 0.0), axis=-1, keepdims=True)
    rank2 = jnp.sum(jnp.where(hit2, before, 0.0), axis=-1, keepdims=True)
    counts_s[...] += jnp.sum(onehot, axis=0, keepdims=True)
    counts_ref[...] = counts_s[...].astype(jnp.int32)
    f32 = jnp.float32
    info = jnp.where(lane == 0, e1.astype(f32), jnp.where(lane == 1, e2.astype(f32),
                     jnp.where(lane == 2, rank1, jnp.where(lane == 3, rank2, 0.0))))
    info_ref[...] = info.T[:SUBLANES, :]


def _outproj_router(mixed, hp, wo, g_ffn, wr, real_rows):
    tp, d = hp.shape
    tm = _pick(tp, (512, 256, 128))
    row = lambda w: pl.BlockSpec((tm, w), lambda i: (i, 0))
    return pl.pallas_call(
        _outproj_router_kernel,
        grid=(tp // tm,),
        in_specs=[row(d), row(d),
                  pl.BlockSpec((d, d), lambda i: (0, 0)),
                  pl.BlockSpec((1, d), lambda i: (0, 0)),
                  pl.BlockSpec((d, LANES), lambda i: (0, 0)),
                  row(1)],
        out_specs=[row(d), row(d), row(MOE_TOPK),
                   pl.BlockSpec((SUBLANES, tm), lambda i: (0, i)),
                   pl.BlockSpec((1, LANES), lambda i: (0, 0))],
        out_shape=[jax.ShapeDtypeStruct((tp, d), jnp.float32),
                   jax.ShapeDtypeStruct((tp, d), jnp.float32),
                   jax.ShapeDtypeStruct((tp, MOE_TOPK), jnp.float32),
                   jax.ShapeDtypeStruct((SUBLANES, tp), jnp.float32),
                   jax.ShapeDtypeStruct((1, LANES), jnp.int32)],
        scratch_shapes=[pltpu.VMEM((1, LANES), jnp.float32)],
        compiler_params=_params(("arbitrary",)),
        name="outproj_router",
    )(mixed, hp, wo, g_ffn, wr, real_rows)


def _wait_rows(n, make_copy):
    for k in range(EXPERT_ROWS.bit_length() - 1, -1, -1):
        @pl.when((n & (1 << k)) != 0)
        def _():
            make_copy(1 << k).wait()


def _expert_kernel(layout, n_items_ref, n_used_ref, item_e_ref, item_start_ref, item_rows_ref, item_real_ref,
                   dest_ref, u_hbm, w1_ref, w3_ref, w2_ref, y_hbm,
                   xg_s, xb_s, acc_s, tok_ref, sem):
    n_batch, l_real, lp, n_pad = layout
    i = pl.program_id(0)
    f = pl.program_id(1)
    nf = pl.num_programs(1)
    n_items = n_items_ref[0]
    active = i < n_items
    n_rows = item_rows_ref[i]
    d = acc_s.shape[1]
    blk = MOE_BLOCK

    def issue_gather(item):
        start = item_start_ref[item]
        n = item_real_ref[item]
        n_groups = n // GATHER_UNROLL

        def one(r):
            pltpu.make_async_copy(u_hbm.at[pl.ds(tok_ref[start + r], 1)], xg_s.at[pl.ds(r, 1)], sem.at[0]).start()

        def group(g, c):
            for j in range(GATHER_UNROLL):
                one(g * GATHER_UNROLL + j)
            return c
        lax.fori_loop(0, n_groups, group, 0)

        def single(r, c):
            one(r)
            return c
        lax.fori_loop(n_groups * GATHER_UNROLL, n, single, 0)

    def wait_gather(item):
        _wait_rows(item_real_ref[item],
                   lambda m: pltpu.make_async_copy(u_hbm.at[pl.ds(0, m)], xg_s.at[pl.ds(0, m)], sem.at[0]))

    def out_copy(item, j):
        r0 = pl.multiple_of(j * blk, blk)
        dst0 = pl.multiple_of(item_start_ref[item] + r0, blk)
        return pltpu.make_async_copy(acc_s.at[pl.ds(r0, blk)], y_hbm.at[pl.ds(dst0, blk)], sem.at[1])

    def issue_out(item):
        def body(j, c):
            out_copy(item, j).start()
            return c
        lax.fori_loop(0, item_rows_ref[item] // blk, body, 0)

    def wait_out(item):
        def body(j, c):
            out_copy(item, j).wait()
            return c
        lax.fori_loop(0, item_rows_ref[item] // blk, body, 0)

    @pl.when((i == 0) & (f == 0))
    def _():
        xg_s[...] = jnp.zeros_like(xg_s)
        acc_s[pl.ds(0, blk), :] = jnp.zeros((blk, d), acc_s.dtype)
        n_used = n_used_ref[0]
        n_blocks = y_hbm.shape[0] // blk

        def tail_copy(j):
            return pltpu.make_async_copy(acc_s.at[pl.ds(0, blk)], y_hbm.at[pl.ds(pl.multiple_of(j * blk, blk), blk)],
                                         sem.at[1])

        def fill(j, c):
            tail_copy(j).start()
            return c
        lax.fori_loop(n_used, n_blocks, fill, 0)

        def drain(j, c):
            tail_copy(j).wait()
            return c
        lax.fori_loop(n_used, n_blocks, drain, 0)

        for k in range(MOE_TOPK):
            for b in range(n_batch):
                base = (k * n_batch + b) * l_real
                row0 = b * lp + n_pad

                def invert(g, c, base=base, row0=row0):
                    for j in range(GATHER_UNROLL):
                        p = g * GATHER_UNROLL + j
                        tok_ref[dest_ref[base + p]] = row0 + p
                    return c
                lax.fori_loop(0, l_real // GATHER_UNROLL, invert, 0)

        @pl.when(active)
        def _():
            issue_gather(0)

    @pl.when(active & (f == 0))
    def _():
        wait_gather(i)
        for lo, m in zip((0,) + EXPERT_ROW_VARIANTS, EXPERT_ROW_VARIANTS):
            @pl.when((n_rows > lo) & (n_rows <= m))
            def _():
                xb_s[pl.ds(0, m), :] = xg_s[pl.ds(0, m), :].astype(xb_s.dtype)

        @pl.when(i + 1 < n_items)
        def _():
            issue_gather(i + 1)

    @pl.when((f == 0) & (i >= 1) & (i - 1 < n_items) & jnp.logical_not(active))
    def _():
        wait_out(i - 1)

    for lo, m in zip((0,) + EXPERT_ROW_VARIANTS, EXPERT_ROW_VARIANTS):
        @pl.when(active & (n_rows > lo) & (n_rows <= m))
        def _():
            rows = pl.ds(0, m)
            x = xb_s[rows, :]
            a = jnp.dot(x, w1_ref[0].astype(MXU_DTYPE), preferred_element_type=jnp.float32)
            b = jnp.dot(x, w3_ref[0].astype(MXU_DTYPE), preferred_element_type=jnp.float32)
            hdn = (_silu(a) * b).astype(MXU_DTYPE)

            @pl.when((f == 0) & (i >= 1))
            def _():
                wait_out(i - 1)

            for c0 in range(0, d, EXPERT_OUT_TILE):
                cols = pl.ds(c0, EXPERT_OUT_TILE)
                part = jnp.dot(hdn, w2_ref[0, :, cols].astype(MXU_DTYPE), preferred_element_type=jnp.float32)

                @pl.when(f == 0)
                def _():
                    acc_s[rows, cols] = part

                @pl.when(f > 0)
                def _():
                    acc_s[rows, cols] += part

    @pl.when(f == nf - 1)
    def _():
        @pl.when(active)
        def _():
            issue_out(i)

        @pl.when(active & (i == pl.num_programs(0) - 1))
        def _():
            wait_out(i)


def _experts(u, sched, w1, w3, w2, n_items_max, p_len, layout):
    tp, d = u.shape
    ff = w1.shape[2]
    tf = EXPERT_FF_TILE
    nf = ff // tf
    n_items, n_used, item_e, item_start, item_rows, item_real, dest = sched
    assert layout[1] % GATHER_UNROLL == 0

    def w_in_map(i, f, n_items, n_used, item_e, *_):
        return (item_e[i], 0, jnp.where(i < n_items[0], f, nf - 1))

    def w_out_map(i, f, n_items, n_used, item_e, *_):
        return (item_e[i], jnp.where(i < n_items[0], f, nf - 1), 0)

    grid_spec = pltpu.PrefetchScalarGridSpec(
        num_scalar_prefetch=7,
        grid=(n_items_max, nf),
        in_specs=[
            pl.BlockSpec(memory_space=pl.ANY),
            pl.BlockSpec((1, d, tf), w_in_map),
            pl.BlockSpec((1, d, tf), w_in_map),
            pl.BlockSpec((1, tf, d), w_out_map),
        ],
        out_specs=pl.BlockSpec(memory_space=pl.ANY),
        scratch_shapes=[
            pltpu.VMEM((EXPERT_ROWS, d), jnp.float32),
            pltpu.VMEM((EXPERT_ROWS, d), MXU_DTYPE),
            pltpu.VMEM((EXPERT_ROWS, d), jnp.float32),
            pltpu.SMEM((p_len,), jnp.int32),
            pltpu.SemaphoreType.DMA((2,)),
        ],
    )
    return pl.pallas_call(
        functools.partial(_expert_kernel, layout),
        grid_spec=grid_spec,
        out_shape=jax.ShapeDtypeStruct((p_len, d), jnp.float32),
        compiler_params=_params(("arbitrary", "arbitrary")),
        name="experts",
    )(n_items, n_used, item_e, item_start, item_rows, item_real, dest, u, w1, w3, w2)


def _expert_schedule(info, counts, batch, lp, n_pad, n_items_max, p_len):
    i32 = jnp.int32
    eid = info[:MOE_TOPK].astype(i32)
    rank = info[MOE_TOPK:2 * MOE_TOPK].astype(i32)
    padded = (counts + MOE_BLOCK - 1) // MOE_BLOCK * MOE_BLOCK
    pad_end = jnp.cumsum(padded)
    pad_start = pad_end - padded
    experts = jnp.arange(MOE_EXPERTS, dtype=i32)[:, None, None]
    seg_start = jnp.sum(jnp.where(eid[None] == experts, pad_start[:, None, None], 0), axis=0)
    dest = (seg_start + rank).reshape(MOE_TOPK, batch, lp)[:, :, n_pad:]
    n_used = (pad_end[-1:] // MOE_BLOCK).astype(i32)

    chunks = (padded + EXPERT_ROWS - 1) // EXPERT_ROWS
    chunk_end = jnp.cumsum(chunks)
    item = jnp.arange(n_items_max, dtype=i32)
    item_e = jnp.minimum(jnp.sum((chunk_end[None, :] <= item[:, None]).astype(i32), axis=1), MOE_EXPERTS - 1)
    n_items = chunk_end[-1:].astype(i32)
    last_e = item_e[jnp.maximum(n_items[0] - 1, 0)]
    item_e = jnp.where(item < n_items[0], item_e, last_e)
    k = item - (chunk_end - chunks)[item_e]
    item_start = (pad_start[item_e] + k * EXPERT_ROWS).astype(i32)
    item_rows = jnp.clip(padded[item_e] - k * EXPERT_ROWS, 0, EXPERT_ROWS).astype(i32)
    item_real = jnp.clip(counts[item_e] - k * EXPERT_ROWS, 0, EXPERT_ROWS).astype(i32)
    live = item < n_items[0]
    item_start = jnp.where(live, item_start, 0)
    item_rows = jnp.where(live, item_rows, 0)
    item_real = jnp.where(live, item_real, 0)
    return (n_items, n_used, item_e, item_start, item_rows, item_real, dest.reshape(-1)), dest


def _final_kernel(tm, dest_ref, h_ref, gate_ref, g_ref, ys_hbm, o_ref, ybuf, sem):
    t = pl.program_id(0)
    n_tiles = pl.num_programs(0)

    def issue(tile, buf):
        n_x = n_tiles * tm
        for j in range(tm):
            for k in range(MOE_TOPK):
                pos = dest_ref[k * n_x + tile * tm + j]
                pltpu.make_async_copy(ys_hbm.at[pl.ds(pos, 1)], ybuf.at[buf, k, pl.ds(j, 1)], sem.at[buf]).start()

    @pl.when(t == 0)
    def _():
        issue(0, 0)

    @pl.when(t + 1 < n_tiles)
    def _():
        issue(t + 1, (t + 1) % 2)

    buf = t % 2
    for k in range(MOE_TOPK):
        pltpu.make_async_copy(ys_hbm.at[pl.ds(0, tm)], ybuf.at[buf, k], sem.at[buf]).wait()
    moe = gate_ref[:, 0:1] * ybuf[buf, 0] + gate_ref[:, 1:2] * ybuf[buf, 1]
    h = h_ref[...] + moe
    o_ref[...] = h * lax.rsqrt(jnp.mean(h * h, axis=-1, keepdims=True) + EPS) * g_ref[...]


def _final(h2, ys, dest_x, gates, g_final, batch, lp, seq):
    tp, d = h2.shape
    tm = SSD_CHUNK
    skip = (lp - seq) // tm
    per_b = seq // tm
    src = lambda w: pl.BlockSpec((tm, w), lambda t, dest: ((t // per_b) * (lp // tm) + skip + t % per_b, 0))
    grid_spec = pltpu.PrefetchScalarGridSpec(
        num_scalar_prefetch=1,
        grid=(batch * per_b,),
        in_specs=[src(d), src(MOE_TOPK), pl.BlockSpec((1, d), lambda t, dest: (0, 0)),
                  pl.BlockSpec(memory_space=pl.ANY)],
        out_specs=pl.BlockSpec((tm, d), lambda t, dest: (t, 0)),
        scratch_shapes=[pltpu.VMEM((2, MOE_TOPK, tm, d), jnp.float32), pltpu.SemaphoreType.DMA((2,))],
    )
    return pl.pallas_call(
        functools.partial(_final_kernel, tm),
        grid_spec=grid_spec,
        out_shape=jax.ShapeDtypeStruct((batch * seq, d), jnp.float32),
        compiler_params=_params(("arbitrary",)),
        name="final_norm",
    )(dest_x, h2, gates, g_final, ys)


def kernel(x, meta_tokens, norm_mix, w_in, ssd_conv_w, ssd_conv_b, ssd_dt_bias, ssd_a_log, ssd_d, ssd_norm, w_ssd_out, lru_conv_w, lru_conv_b, lru_wa, lru_ba, lru_wx, lru_bx, lru_lambda, w_lru_out, gate_bias, w_out, norm_ffn, w_router_group, w_router_expert, w_exp_gate, w_exp_up, w_exp_down, norm_final):
    batch, seq, d = x.shape
    depth = norm_mix.shape[0]
    assert depth == 1 and seq % SSD_CHUNK == 0
    l = N_META + seq
    lp = -(-l // SSD_CHUNK) * SSD_CHUNK
    n_pad = lp - l
    d_inner = d
    heads = d_inner // SSD_HEAD_DIM
    conv_dim = d_inner + 2 * SSD_GROUPS * SSD_STATE
    col_xbc = d_inner
    col_dt = col_xbc + conv_dim
    col_lx = col_dt + heads
    f32 = jnp.float32
    lyr = 0

    meta = jnp.broadcast_to(meta_tokens.astype(f32)[None], (batch, N_META, d))
    hp = jnp.concatenate([jnp.zeros((batch, n_pad, d), f32), meta, x], axis=1).reshape(batch * lp, d)

    w = w_in[lyr]
    w_all = w.astype(MXU_DTYPE)
    w_b = w_all[:, col_lx:]
    w_dt = jnp.zeros((d, LANES), MXU_DTYPE).at[:, :heads].set(w_all[:, col_dt:col_lx])
    dtb = jnp.zeros((1, LANES), f32).at[0, :heads].set(ssd_dt_bias[lyr])
    alog = jnp.zeros((1, LANES), f32).at[0, :heads].set(ssd_a_log[lyr])
    proj, dt, acum = _norm_inproj(hp, norm_mix[lyr][None, :], w_all, col_dt, w_b, w_dt, dtb, alog)
    n_main = col_dt + w_b.shape[1]
    proj3 = proj.reshape(batch, lp, n_main)
    lx_col = col_dt
    ly_col = lx_col + d
    gl_col = ly_col + d

    ys = _ssd(proj3, dt.reshape(batch, lp, LANES), acum.reshape(batch, lp, LANES), n_pad, ssd_conv_w[lyr],
              ssd_conv_b[lyr], ssd_d[lyr], ssd_norm[lyr], d_inner)
    yl = _lru(proj3, n_pad, lx_col, ly_col, lru_conv_w[lyr], lru_conv_b[lyr], lru_wa[lyr].astype(MXU_DTYPE),
              lru_ba[lyr], lru_wx[lyr].astype(MXU_DTYPE), lru_bx[lyr], lru_lambda[lyr], d)

    mixed = _merge(ys.reshape(batch * lp, d), yl.reshape(batch * lp, d), w_ssd_out[lyr].astype(MXU_DTYPE),
                   w_lru_out[lyr].astype(MXU_DTYPE), proj, gl_col, gate_bias[lyr])

    w_router = jnp.zeros((d, LANES), f32)
    w_router = w_router.at[:, :MOE_GROUPS].set(w_router_group[lyr])
    w_router = w_router.at[:, MOE_GROUPS:MOE_GROUPS + MOE_EXPERTS].set(w_router_expert[lyr])
    real_rows = jnp.asarray(((np.arange(batch * lp) % lp) >= n_pad).astype(np.float32)[:, None])
    h2, u2, gates, info, counts = _outproj_router(mixed, hp, w_out[lyr].astype(MXU_DTYPE), norm_ffn[lyr][None, :],
                                                  w_router.astype(MXU_DTYPE), real_rows)

    n_asg = batch * l * MOE_TOPK
    p_max = n_asg + MOE_EXPERTS * (MOE_BLOCK - 1)
    n_items_max = MOE_EXPERTS + p_max // EXPERT_ROWS
    p_len = -(-p_max // MOE_BLOCK) * MOE_BLOCK
    sched, dest = _expert_schedule(info, counts[0, :MOE_EXPERTS], batch, lp, n_pad, n_items_max, p_len)
    ff = w_exp_gate.shape[-1]
    ys_sorted = _experts(u2, sched, w_exp_gate.reshape(MOE_EXPERTS, d, ff), w_exp_up.reshape(MOE_EXPERTS, d, ff),
                         w_exp_down.reshape(MOE_EXPERTS, ff, d), n_items_max, p_len, (batch, l, lp, n_pad))

    dest_x = dest[:, :, N_META:].reshape(-1)
    out = _final(h2, ys_sorted, dest_x, gates, norm_final[None, :], batch, lp, seq)
    return out.reshape(batch, seq, d)
```

```python
import functools
import math

import numpy as np
import jax
import jax.numpy as jnp
from jax import lax
from jax.experimental import pallas as pl
from jax.experimental.pallas import tpu as pltpu

N_META = 16
CONV_K = 4
EPS = 1e-6
SSD_HEAD_DIM = 64
SSD_HEAD_SHIFT = 6
SSD_GROUPS = 8
SSD_STATE = 128
SSD_CHUNK = 128
LRU_HEADS = 8
LRU_C = 8.0
N_BRANCH = 2
MOE_GROUPS = 8
MOE_EXP_PER_GROUP = 8
MOE_EXPERTS = MOE_GROUPS * MOE_EXP_PER_GROUP
MOE_TOPK = 2
MOE_BLOCK = 64

LANES = 128
SUBLANES = 8
VMEM_LIMIT = 56 * 1024 * 1024
MXU_DTYPE = jnp.bfloat16

EXPERT_ROWS = 1024
EXPERT_ROW_VARIANTS = (256, 512, 576, 640, 1024)
EXPERT_FF_TILE = 512
EXPERT_OUT_TILE = 1024
GATHER_UNROLL = 8
SSD_UNROLL = 2


def _pick(n, options):
    for o in options:
        if n % o == 0:
            return o
    raise ValueError(f"no tile in {options} divides {n}")


def _params(sem, vmem=VMEM_LIMIT):
    return pltpu.CompilerParams(dimension_semantics=sem, vmem_limit_bytes=vmem)


def _dot(a, b):
    return jnp.dot(a.astype(MXU_DTYPE), b.astype(MXU_DTYPE), preferred_element_type=jnp.float32)


def _dot_exact_rhs(v, sel):
    sel = sel.astype(jnp.bfloat16)
    hi = v.astype(jnp.bfloat16)
    r1 = v - hi.astype(jnp.float32)
    mid = r1.astype(jnp.bfloat16)
    lo = (r1 - mid.astype(jnp.float32)).astype(jnp.bfloat16)
    f32 = jnp.float32
    return (jnp.dot(hi, sel, preferred_element_type=f32) + jnp.dot(mid, sel, preferred_element_type=f32)
            + jnp.dot(lo, sel, preferred_element_type=f32))


def _dot_exact_lhs(sel, v):
    sel = sel.astype(jnp.bfloat16)
    hi = v.astype(jnp.bfloat16)
    r1 = v - hi.astype(jnp.float32)
    mid = r1.astype(jnp.bfloat16)
    lo = (r1 - mid.astype(jnp.float32)).astype(jnp.bfloat16)
    f32 = jnp.float32
    return (jnp.dot(sel, hi, preferred_element_type=f32) + jnp.dot(sel, mid, preferred_element_type=f32)
            + jnp.dot(sel, lo, preferred_element_type=f32))


def _softplus(x):
    return jnp.maximum(x, 0.0) + jnp.log1p(jnp.exp(-jnp.abs(x)))


def _sigmoid(x):
    return 1.0 / (1.0 + jnp.exp(-x))


def _silu(x):
    return x * _sigmoid(x)


def _causal_conv_chunk(x_ref, w_ref, b_ref, c):
    q = SSD_CHUNK
    if isinstance(c, int) and c > 0:
        acc = b_ref[...] + w_ref[CONV_K - 1:CONV_K, :] * x_ref[0, pl.ds(c * q, q), :]
        for k in range(1, CONV_K):
            acc = acc + w_ref[CONV_K - 1 - k:CONV_K - k, :] * x_ref[0, pl.ds(c * q - k, q), :]
        return acc
    c = jnp.int32(c) if isinstance(c, int) else c
    r0 = pl.multiple_of(c * q, q)
    cur = x_ref[0, pl.ds(r0, q), :]
    prev = x_ref[0, pl.ds(pl.multiple_of(jnp.maximum(r0 - SUBLANES, 0), SUBLANES), SUBLANES), :]
    prev = jnp.where(c > 0, prev, 0.0)
    sub = lax.broadcasted_iota(jnp.int32, prev.shape, 0)
    acc = b_ref[...] + w_ref[CONV_K - 1:CONV_K, :] * cur
    for k in range(1, CONV_K):
        rolled = pltpu.roll(cur, k, 0)
        head = jnp.where(sub < k, pltpu.roll(prev, k, 0), rolled[:SUBLANES])
        shifted = jnp.concatenate([head, rolled[SUBLANES:]], axis=0)
        acc = acc + w_ref[CONV_K - 1 - k:CONV_K - k, :] * shifted
    return acc


def _norm_inproj_kernel(n_a, x_ref, g_ref, wa_ref, wb_ref, wdt_ref, dtb_ref, alog_ref, o_ref, dt_ref, acum_ref, xn_ref):
    j = pl.program_id(1)

    @pl.when(j == 0)
    def _():
        x = x_ref[...]
        y = x * lax.rsqrt(jnp.mean(x * x, axis=-1, keepdims=True) + EPS)
        xn_ref[...] = (y * g_ref[...]).astype(xn_ref.dtype)
        raw = jnp.dot(xn_ref[...], wdt_ref[...], preferred_element_type=jnp.float32)
        dt = _softplus(raw + dtb_ref[...])
        dt_ref[...] = dt
        adt = dt * (-jnp.exp(alog_ref[...]))
        q = SSD_CHUNK
        tri = (lax.broadcasted_iota(jnp.int32, (q, q), 0) >= lax.broadcasted_iota(jnp.int32, (q, q), 1))
        for c in range(x.shape[0] // q):
            acum_ref[c * q:(c + 1) * q, :] = _dot_exact_lhs(tri.astype(jnp.float32), adt[c * q:(c + 1) * q, :])

    @pl.when(j < n_a)
    def _():
        o_ref[...] = jnp.dot(xn_ref[...], wa_ref[...], preferred_element_type=jnp.float32)

    @pl.when(j >= n_a)
    def _():
        o_ref[...] = jnp.dot(xn_ref[...], wb_ref[...], preferred_element_type=jnp.float32)


def _norm_inproj(hp, g, w_a, cols_a, w_b, wdt, dtb, alog):
    tp, d = hp.shape
    n = cols_a + w_b.shape[1]
    tm = _pick(tp, (1024, 512, 256, 128))
    tn = math.gcd(_pick(cols_a, (1024, 512, 256, 128)), _pick(w_b.shape[1], (1024, 512, 256, 128)))
    n_a = cols_a // tn
    vec = pl.BlockSpec((1, LANES), lambda i, j: (0, 0))
    return pl.pallas_call(
        functools.partial(_norm_inproj_kernel, n_a),
        grid=(tp // tm, n // tn),
        in_specs=[
            pl.BlockSpec((tm, d), lambda i, j: (i, 0)),
            pl.BlockSpec((1, d), lambda i, j: (0, 0)),
            pl.BlockSpec((d, tn), lambda i, j: (0, jnp.minimum(j, n_a - 1))),
            pl.BlockSpec((d, tn), lambda i, j: (0, jnp.maximum(j - n_a, 0))),
            pl.BlockSpec((d, LANES), lambda i, j: (0, 0)),
            vec, vec,
        ],
        out_specs=[
            pl.BlockSpec((tm, tn), lambda i, j: (i, j)),
            pl.BlockSpec((tm, LANES), lambda i, j: (i, 0)),
            pl.BlockSpec((tm, LANES), lambda i, j: (i, 0)),
        ],
        out_shape=[
            jax.ShapeDtypeStruct((tp, n), jnp.float32),
            jax.ShapeDtypeStruct((tp, LANES), jnp.float32),
            jax.ShapeDtypeStruct((tp, LANES), jnp.float32),
        ],
        scratch_shapes=[pltpu.VMEM((tm, d), MXU_DTYPE)],
        compiler_params=_params(("parallel", "arbitrary")),
        name="norm_inproj",
    )(hp, g, w_a, w_b, wdt, dtb, alog)


def _ssd_kernel(n_pad, z_ref, xs_ref, b_ref, c_ref, dt_ref, acum_ref,
                cwx_ref, cwb_ref, cwc_ref, cbx_ref, cbb_ref, cbc_ref,
                d_ref, nw_ref, o_ref, state_s):
    g = pl.program_id(1)
    lp = xs_ref.shape[1]
    q = SSD_CHUNK
    gw = xs_ref.shape[2]
    hpg = gw // SSD_HEAD_DIM
    f32 = jnp.float32

    rows = lax.broadcasted_iota(jnp.int32, (LANES, gw), 0)
    cols = lax.broadcasted_iota(jnp.int32, (LANES, gw), 1)
    expand = (rows == g * hpg + lax.shift_right_logical(cols, SSD_HEAD_SHIFT)).astype(f32)

    state_s[...] = jnp.zeros_like(state_s)
    li = lax.broadcasted_iota(jnp.int32, (q, q), 0)
    si = lax.broadcasted_iota(jnp.int32, (q, q), 1)
    causal = li >= si
    lane = lax.broadcasted_iota(jnp.int32, (q, gw), 1)
    row = lax.broadcasted_iota(jnp.int32, (q, gw), 0)

    def chunk(c, carry):
        r0 = c * q
        sl = pl.ds(r0, q)
        xs = _silu(_causal_conv_chunk(xs_ref, cwx_ref, cbx_ref, c))
        bm = _silu(_causal_conv_chunk(b_ref, cwb_ref, cbb_ref, c))
        cm = _silu(_causal_conv_chunk(c_ref, cwc_ref, cbc_ref, c))
        dt_g = _dot_exact_rhs(dt_ref[0, sl, :], expand)
        acum = _dot_exact_rhs(acum_ref[0, sl, :], expand)
        xdt = jnp.where(row + r0 >= n_pad, xs * dt_g, 0.0)
        acum_t = acum.T
        a_last = acum[q - 1:q, :]

        cb = lax.dot_general(cm.astype(MXU_DTYPE), bm.astype(MXU_DTYPE),
                             (((1,), (1,)), ((), ())), preferred_element_type=f32)
        y = d_ref[...] * xs
        for j in range(hpg):
            c0 = j * SSD_HEAD_DIM
            seg = acum[:, c0:c0 + 1] - acum_t[c0:c0 + 1, :]
            decay = jnp.exp(jnp.where(causal, seg, -jnp.inf))
            in_head = (lane >= c0) & (lane < c0 + SSD_HEAD_DIM)
            y = y + _dot(cb * decay, jnp.where(in_head, xdt, 0.0))
        x_end = xdt * jnp.exp(a_last - acum)

        state = state_s[...]
        y = y + _dot(cm, state) * jnp.exp(acum)
        state_s[...] = state * jnp.exp(a_last) + _dot(bm.T, x_end)

        yz = y * _silu(z_ref[0, sl, :])
        yn = yz * lax.rsqrt(jnp.mean(yz * yz, axis=-1, keepdims=True) + EPS)
        o_ref[0, sl, :] = (yn * nw_ref[...]).astype(o_ref.dtype)
        return carry

    for c in range(lp // q):
        chunk(c, 0)


def _ssd(proj3, dt3, acum3, n_pad, cw, cb, d_skip, norm_w, d_inner):
    b, lp, _ = proj3.shape
    gw = d_inner // SSD_GROUPS
    ns = SSD_STATE
    xs_blk0 = d_inner // gw
    b_blk0 = 2 * d_inner // ns
    c_blk0 = (2 * d_inner + SSD_GROUPS * ns) // ns

    cwx = cw[:, :d_inner]
    cwb = cw[:, d_inner:d_inner + SSD_GROUPS * ns]
    cwc = cw[:, d_inner + SSD_GROUPS * ns:]
    cbx = cb[None, :d_inner]
    cbb = cb[None, d_inner:d_inner + SSD_GROUPS * ns]
    cbc = cb[None, d_inner + SSD_GROUPS * ns:]
    d_ch = jnp.repeat(d_skip, SSD_HEAD_DIM)[None, :]
    nw = norm_w[None, :]

    seq = lambda w, off: pl.BlockSpec((1, lp, w), lambda i, g: (i, 0, off + g))
    head = pl.BlockSpec((1, lp, LANES), lambda i, g: (i, 0, 0))
    vec = lambda w: pl.BlockSpec((1, w), lambda i, g: (0, g))
    cwspec = lambda w: pl.BlockSpec((CONV_K, w), lambda i, g: (0, g))
    return pl.pallas_call(
        functools.partial(_ssd_kernel, n_pad),
        grid=(b, SSD_GROUPS),
        in_specs=[
            seq(gw, 0), seq(gw, xs_blk0), seq(ns, b_blk0), seq(ns, c_blk0), head, head,
            cwspec(gw), cwspec(ns), cwspec(ns), vec(gw), vec(ns), vec(ns),
            vec(gw), vec(gw),
        ],
        out_specs=pl.BlockSpec((1, lp, gw), lambda i, g: (i, 0, g)),
        out_shape=jax.ShapeDtypeStruct((b, lp, d_inner), MXU_DTYPE),
        scratch_shapes=[pltpu.VMEM((ns, gw), jnp.float32)],
        compiler_params=_params(("parallel", "parallel")),
        name="ssd",
    )(proj3, proj3, proj3, proj3, dt3, acum3, cwx, cwb, cwc, cbx, cbb, cbc, d_ch, nw)


def _lru_kernel(n_pad, lx_ref, ly_ref, cw_ref, cb_ref, wa_ref, ba_ref, wx_ref, bx_ref, lam_ref, o_ref):
    lp = lx_ref.shape[1]
    w = lx_ref.shape[2]
    q = SSD_CHUNK
    neg_c_softplus = (-LRU_C) * _softplus(-lam_ref[...])
    row = lax.broadcasted_iota(jnp.int32, (q, w), 0)
    sub = lax.broadcasted_iota(jnp.int32, (SUBLANES, w), 0)

    def chunk(c, h_prev):
        r0 = c * q
        sl = pl.ds(r0, q)
        xr = _causal_conv_chunk(lx_ref, cw_ref, cb_ref, c)
        gate_r = _sigmoid(_dot(xr, wa_ref[0]) + ba_ref[...])
        gate_i = _sigmoid(_dot(xr, wx_ref[0]) + bx_ref[...])
        log_a = gate_r * neg_c_softplus
        a = jnp.exp(log_a)
        mult = jnp.sqrt(jnp.tanh(-log_a) * (a * a + 1.0))
        u = jnp.where(row + r0 >= n_pad, mult * gate_i * xr, 0.0)
        gate_y = jax.nn.gelu(ly_ref[0, sl, :])

        out = []
        for t in range(q // SUBLANES):
            rows8 = slice(t * SUBLANES, (t + 1) * SUBLANES)
            at, ut = a[rows8], u[rows8]
            for d in (1, 2, 4):
                keep = sub >= d
                ut = jnp.where(keep, at * pltpu.roll(ut, d, 0) + ut, ut)
                at = jnp.where(keep, at * pltpu.roll(at, d, 0), at)
            h = at * h_prev + ut
            h_prev = jnp.broadcast_to(h[SUBLANES - 1:SUBLANES, :], (SUBLANES, w))
            out.append(h * gate_y[rows8])
        o_ref[0, sl, :] = jnp.concatenate(out, axis=0).astype(o_ref.dtype)
        return h_prev

    h_carry = jnp.zeros((SUBLANES, w), jnp.float32)
    for c in range(lp // q):
        h_carry = chunk(c, h_carry)


def _lru(proj3, n_pad, lx_col, ly_col, cw, cb, wa, ba, wx, bx, lam, width):
    b, lp, _ = proj3.shape
    w = width // LRU_HEADS
    seq = lambda off: pl.BlockSpec((1, lp, w), lambda i, h: (i, 0, off + h))
    vec = pl.BlockSpec((1, w), lambda i, h: (0, h))
    mat = pl.BlockSpec((1, w, w), lambda i, h: (h, 0, 0))
    return pl.pallas_call(
        functools.partial(_lru_kernel, n_pad),
        grid=(b, LRU_HEADS),
        in_specs=[seq(lx_col // w), seq(ly_col // w),
                  pl.BlockSpec((CONV_K, w), lambda i, h: (0, h)), vec, mat, vec, mat, vec, vec],
        out_specs=pl.BlockSpec((1, lp, w), lambda i, h: (i, 0, h)),
        out_shape=jax.ShapeDtypeStruct((b, lp, width), MXU_DTYPE),
        compiler_params=_params(("parallel", "parallel")),
        name="rglru",
    )(proj3, proj3, cw, cb[None, :], wa, ba[None, :], wx, bx[None, :], lam[None, :])


def _merge_kernel(ys_ref, yl_ref, ws_ref, wl_ref, g0_ref, g1_ref, gb_ref, o_ref):
    y_ssd = jnp.dot(ys_ref[...], ws_ref[...], preferred_element_type=jnp.float32)
    y_lru = jnp.dot(yl_ref[...], wl_ref[...], preferred_element_type=jnp.float32)
    gate0 = _sigmoid(g0_ref[...] + gb_ref[0:1, :])
    gate1 = _sigmoid(g1_ref[...] + gb_ref[1:2, :])
    o_ref[...] = (gate0 * y_ssd + gate1 * y_lru).astype(o_ref.dtype)


def _merge(ys, yl, ws, wl, proj, gate_col, gate_bias):
    tp, d = ys.shape
    tm = _pick(tp, (1024, 512, 256, 128))
    tn = _pick(d, (512, 256, 128))
    g0 = gate_col // tn
    g1 = (gate_col + d) // tn
    return pl.pallas_call(
        _merge_kernel,
        grid=(tp // tm, d // tn),
        in_specs=[
            pl.BlockSpec((tm, d), lambda i, j: (i, 0)),
            pl.BlockSpec((tm, d), lambda i, j: (i, 0)),
            pl.BlockSpec((d, tn), lambda i, j: (0, j)),
            pl.BlockSpec((d, tn), lambda i, j: (0, j)),
            pl.BlockSpec((tm, tn), lambda i, j: (i, g0 + j)),
            pl.BlockSpec((tm, tn), lambda i, j: (i, g1 + j)),
            pl.BlockSpec((N_BRANCH, tn), lambda i, j: (0, j)),
        ],
        out_specs=pl.BlockSpec((tm, tn), lambda i, j: (i, j)),
        out_shape=jax.ShapeDtypeStruct((tp, d), MXU_DTYPE),
        compiler_params=_params(("parallel", "arbitrary")),
        name="merge",
    )(ys, yl, ws, wl, proj, proj, gate_bias)


def _first_index_of_max(p, valid, lane):
    pm = jnp.where(valid, p, -jnp.inf)
    top = jnp.max(pm, axis=-1, keepdims=True)
    idx = jnp.min(jnp.where(valid & (pm == top), lane, LANES), axis=-1, keepdims=True)
    return top, idx


def _masked_softmax(x, valid):
    m = jnp.max(jnp.where(valid, x, -jnp.inf), axis=-1, keepdims=True)
    e = jnp.where(valid, jnp.exp(x - m), 0.0)
    return e / jnp.sum(e, axis=-1, keepdims=True)


def _outproj_router_kernel(mix_ref, h_ref, wo_ref, g_ref, wr_ref, real_ref,
                           h2_ref, u_ref, gate_ref, info_ref, counts_ref, counts_s):
    @pl.when(pl.program_id(0) == 0)
    def _():
        counts_s[...] = jnp.zeros_like(counts_s)

    h2 = h_ref[...] + jnp.dot(mix_ref[...], wo_ref[...], preferred_element_type=jnp.float32)
    h2_ref[...] = h2
    u = h2 * lax.rsqrt(jnp.mean(h2 * h2, axis=-1, keepdims=True) + EPS) * g_ref[...]
    u_ref[...] = u
    logits = _dot(u, wr_ref[...])
    lane = lax.broadcasted_iota(jnp.int32, logits.shape, 1)

    g_prob = _masked_softmax(logits, lane < MOE_GROUPS)
    g_p, g_idx = _first_index_of_max(g_prob, lane < MOE_GROUPS, lane)

    e_lo = MOE_GROUPS + g_idx * MOE_EXP_PER_GROUP
    in_group = (lane >= e_lo) & (lane < e_lo + MOE_EXP_PER_GROUP)
    e_prob = _masked_softmax(logits, in_group)
    p1, i1 = _first_index_of_max(e_prob, in_group, lane)
    rest = in_group & (lane != i1)
    p2, i2 = _first_index_of_max(e_prob, rest, lane)
    denom = p1 + p2

    col = lax.broadcasted_iota(jnp.int32, gate_ref.shape, 1)
    e1 = i1 - MOE_GROUPS
    e2 = i2 - MOE_GROUPS
    gate_ref[...] = jnp.where(col == 0, g_p * p1 / denom, g_p * p2 / denom)

    m = logits.shape[0]
    hit1 = (lane == e1) & (real_ref[...] > 0.0)
    hit2 = (lane == e2) & (real_ref[...] > 0.0)
    onehot = jnp.where(hit1 | hit2, 1.0, 0.0)
    earlier = (lax.broadcasted_iota(jnp.int32, (m, m), 0) > lax.broadcasted_iota(jnp.int32, (m, m), 1))
    before = counts_s[...] + jnp.dot(earlier.astype(jnp.bfloat16), onehot.astype(jnp.bfloat16),
                                     preferred_element_type=jnp.float32)
    rank1 = jnp.sum(jnp.where(hit1, before, 0.0), axis=-1, keepdims=True)
    rank2 = jnp.sum(jnp.where(hit2, before, 0.0), axis=-1, keepdims=True)
    counts_s[...] += jnp.sum(onehot, axis=0, keepdims=True)
    counts_ref[...] = counts_s[...].astype(jnp.int32)
    f32 = jnp.float32
    info = jnp.where(lane == 0, e1.astype(f32), jnp.where(lane == 1, e2.astype(f32),
                     jnp.where(lane == 2, rank1, jnp.where(lane == 3, rank2, 0.0))))
    info_ref[...] = info.T[:SUBLANES, :]


def _outproj_router(mixed, hp, wo, g_ffn, wr, real_rows):
    tp, d = hp.shape
    tm = _pick(tp, (512, 256, 128))
    row = lambda w: pl.BlockSpec((tm, w), lambda i: (i, 0))
    return pl.pallas_call(
        _outproj_router_kernel,
        grid=(tp // tm,),
        in_specs=[row(d), row(d),
                  pl.BlockSpec((d, d), lambda i: (0, 0)),
                  pl.BlockSpec((1, d), lambda i: (0, 0)),
                  pl.BlockSpec((d, LANES), lambda i: (0, 0)),
                  row(1)],
        out_specs=[row(d), row(d), row(MOE_TOPK),
                   pl.BlockSpec((SUBLANES, tm), lambda i: (0, i)),
                   pl.BlockSpec((1, LANES), lambda i: (0, 0))],
        out_shape=[jax.ShapeDtypeStruct((tp, d), jnp.float32),
                   jax.ShapeDtypeStruct((tp, d), jnp.float32),
                   jax.ShapeDtypeStruct((tp, MOE_TOPK), jnp.float32),
                   jax.ShapeDtypeStruct((SUBLANES, tp), jnp.float32),
                   jax.ShapeDtypeStruct((1, LANES), jnp.int32)],
        scratch_shapes=[pltpu.VMEM((1, LANES), jnp.float32)],
        compiler_params=_params(("arbitrary",)),
        name="outproj_router",
    )(mixed, hp, wo, g_ffn, wr, real_rows)


def _wait_rows(n, make_copy):
    for k in range(EXPERT_ROWS.bit_length() - 1, -1, -1):
        @pl.when((n & (1 << k)) != 0)
        def _():
            make_copy(1 << k).wait()


def _expert_kernel(layout, n_items_ref, n_used_ref, item_e_ref, item_start_ref, item_rows_ref, item_real_ref,
                   dest_ref, u_hbm, w1_ref, w3_ref, w2_ref, y_hbm,
                   xg_s, xb_s, acc_s, tok_ref, sem):
    n_batch, l_real, lp, n_pad = layout
    i = pl.program_id(0)
    f = pl.program_id(1)
    nf = pl.num_programs(1)
    n_items = n_items_ref[0]
    active = i < n_items
    n_rows = item_rows_ref[i]
    d = acc_s.shape[1]
    blk = MOE_BLOCK

    def issue_gather(item):
        start = item_start_ref[item]
        n = item_real_ref[item]
        n_groups = n // GATHER_UNROLL

        def one(r):
            pltpu.make_async_copy(u_hbm.at[pl.ds(tok_ref[start + r], 1)], xg_s.at[pl.ds(r, 1)], sem.at[0]).start()

        def group(g, c):
            for j in range(GATHER_UNROLL):
                one(g * GATHER_UNROLL + j)
            return c
        lax.fori_loop(0, n_groups, group, 0)

        def single(r, c):
            one(r)
            return c
        lax.fori_loop(n_groups * GATHER_UNROLL, n, single, 0)

    def wait_gather(item):
        _wait_rows(item_real_ref[item],
                   lambda m: pltpu.make_async_copy(u_hbm.at[pl.ds(0, m)], xg_s.at[pl.ds(0, m)], sem.at[0]))

    def out_copy(item, j):
        r0 = pl.multiple_of(j * blk, blk)
        dst0 = pl.multiple_of(item_start_ref[item] + r0, blk)
        return pltpu.make_async_copy(acc_s.at[pl.ds(r0, blk)], y_hbm.at[pl.ds(dst0, blk)], sem.at[1])

    def issue_out(item):
        def body(j, c):
            out_copy(item, j).start()
            return c
        lax.fori_loop(0, item_rows_ref[item] // blk, body, 0)

    def wait_out(item):
        def body(j, c):
            out_copy(item, j).wait()
            return c
        lax.fori_loop(0, item_rows_ref[item] // blk, body, 0)

    @pl.when((i == 0) & (f == 0))
    def _():
        xg_s[...] = jnp.zeros_like(xg_s)
        acc_s[pl.ds(0, blk), :] = jnp.zeros((blk, d), acc_s.dtype)
        n_used = n_used_ref[0]
        n_blocks = y_hbm.shape[0] // blk

        def tail_copy(j):
            return pltpu.make_async_copy(acc_s.at[pl.ds(0, blk)], y_hbm.at[pl.ds(pl.multiple_of(j * blk, blk), blk)],
                                         sem.at[1])

        def fill(j, c):
            tail_copy(j).start()
            return c
        lax.fori_loop(n_used, n_blocks, fill, 0)

        def drain(j, c):
            tail_copy(j).wait()
            return c
        lax.fori_loop(n_used, n_blocks, drain, 0)

        for k in range(MOE_TOPK):
            for b in range(n_batch):
                base = (k * n_batch + b) * l_real
                row0 = b * lp + n_pad

                def invert(g, c, base=base, row0=row0):
                    for j in range(GATHER_UNROLL):
                        p = g * GATHER_UNROLL + j
                        tok_ref[dest_ref[base + p]] = row0 + p
                    return c
                lax.fori_loop(0, l_real // GATHER_UNROLL, invert, 0)

        @pl.when(active)
        def _():
            issue_gather(0)

    @pl.when(active & (f == 0))
    def _():
        wait_gather(i)
        for lo, m in zip((0,) + EXPERT_ROW_VARIANTS, EXPERT_ROW_VARIANTS):
            @pl.when((n_rows > lo) & (n_rows <= m))
            def _():
                xb_s[pl.ds(0, m), :] = xg_s[pl.ds(0, m), :].astype(xb_s.dtype)

        @pl.when(i + 1 < n_items)
        def _():
            issue_gather(i + 1)

    @pl.when((f == 0) & (i >= 1) & (i - 1 < n_items) & jnp.logical_not(active))
    def _():
        wait_out(i - 1)

    for lo, m in zip((0,) + EXPERT_ROW_VARIANTS, EXPERT_ROW_VARIANTS):
        @pl.when(active & (n_rows > lo) & (n_rows <= m))
        def _():
            rows = pl.ds(0, m)
            x = xb_s[rows, :]
            a = jnp.dot(x, w1_ref[0].astype(MXU_DTYPE), preferred_element_type=jnp.float32)
            b = jnp.dot(x, w3_ref[0].astype(MXU_DTYPE), preferred_element_type=jnp.float32)
            hdn = (_silu(a) * b).astype(MXU_DTYPE)

            @pl.when((f == 0) & (i >= 1))
            def _():
                wait_out(i - 1)

            for c0 in range(0, d, EXPERT_OUT_TILE):
                cols = pl.ds(c0, EXPERT_OUT_TILE)
                part = jnp.dot(hdn, w2_ref[0, :, cols].astype(MXU_DTYPE), preferred_element_type=jnp.float32)

                @pl.when(f == 0)
                def _():
                    acc_s[rows, cols] = part

                @pl.when(f > 0)
                def _():
                    acc_s[rows, cols] += part

    @pl.when(f == nf - 1)
    def _():
        @pl.when(active)
        def _():
            issue_out(i)

        @pl.when(active & (i == pl.num_programs(0) - 1))
        def _():
            wait_out(i)


def _experts(u, sched, w1, w3, w2, n_items_max, p_len, layout):
    tp, d = u.shape
    ff = w1.shape[2]
    tf = EXPERT_FF_TILE
    nf = ff // tf
    n_items, n_used, item_e, item_start, item_rows, item_real, dest = sched
    assert layout[1] % GATHER_UNROLL == 0

    def w_in_map(i, f, n_items, n_used, item_e, *_):
        return (item_e[i], 0, jnp.where(i < n_items[0], f, nf - 1))

    def w_out_map(i, f, n_items, n_used, item_e, *_):
        return (item_e[i], jnp.where(i < n_items[0], f, nf - 1), 0)

    grid_spec = pltpu.PrefetchScalarGridSpec(
        num_scalar_prefetch=7,
        grid=(n_items_max, nf),
        in_specs=[
            pl.BlockSpec(memory_space=pl.ANY),
            pl.BlockSpec((1, d, tf), w_in_map),
            pl.BlockSpec((1, d, tf), w_in_map),
            pl.BlockSpec((1, tf, d), w_out_map),
        ],
        out_specs=pl.BlockSpec(memory_space=pl.ANY),
        scratch_shapes=[
            pltpu.VMEM((EXPERT_ROWS, d), jnp.float32),
            pltpu.VMEM((EXPERT_ROWS, d), MXU_DTYPE),
            pltpu.VMEM((EXPERT_ROWS, d), jnp.float32),
            pltpu.SMEM((p_len,), jnp.int32),
            pltpu.SemaphoreType.DMA((2,)),
        ],
    )
    return pl.pallas_call(
        functools.partial(_expert_kernel, layout),
        grid_spec=grid_spec,
        out_shape=jax.ShapeDtypeStruct((p_len, d), jnp.float32),
        compiler_params=_params(("arbitrary", "arbitrary")),
        name="experts",
    )(n_items, n_used, item_e, item_start, item_rows, item_real, dest, u, w1, w3, w2)


def _expert_schedule(info, counts, batch, lp, n_pad, n_items_max, p_len):
    i32 = jnp.int32
    eid = info[:MOE_TOPK].astype(i32)
    rank = info[MOE_TOPK:2 * MOE_TOPK].astype(i32)
    padded = (counts + MOE_BLOCK - 1) // MOE_BLOCK * MOE_BLOCK
    pad_end = jnp.cumsum(padded)
    pad_start = pad_end - padded
    experts = jnp.arange(MOE_EXPERTS, dtype=i32)[:, None, None]
    seg_start = jnp.sum(jnp.where(eid[None] == experts, pad_start[:, None, None], 0), axis=0)
    dest = (seg_start + rank).reshape(MOE_TOPK, batch, lp)[:, :, n_pad:]
    n_used = (pad_end[-1:] // MOE_BLOCK).astype(i32)

    chunks = (padded + EXPERT_ROWS - 1) // EXPERT_ROWS
    chunk_end = jnp.cumsum(chunks)
    item = jnp.arange(n_items_max, dtype=i32)
    item_e = jnp.minimum(jnp.sum((chunk_end[None, :] <= item[:, None]).astype(i32), axis=1), MOE_EXPERTS - 1)
    n_items = chunk_end[-1:].astype(i32)
    last_e = item_e[jnp.maximum(n_items[0] - 1, 0)]
    item_e = jnp.where(item < n_items[0], item_e, last_e)
    k = item - (chunk_end - chunks)[item_e]
    item_start = (pad_start[item_e] + k * EXPERT_ROWS).astype(i32)
    item_rows = jnp.clip(padded[item_e] - k * EXPERT_ROWS, 0, EXPERT_ROWS).astype(i32)
    item_real = jnp.clip(counts[item_e] - k * EXPERT_ROWS, 0, EXPERT_ROWS).astype(i32)
    live = item < n_items[0]
    item_start = jnp.where(live, item_start, 0)
    item_rows = jnp.where(live, item_rows, 0)
    item_real = jnp.where(live, item_real, 0)
    return (n_items, n_used, item_e, item_start, item_rows, item_real, dest.reshape(-1)), dest


def _final_kernel(tm, dest_ref, h_ref, gate_ref, g_ref, ys_hbm, o_ref, ybuf, sem):
    t = pl.program_id(0)
    n_tiles = pl.num_programs(0)

    def issue(tile, buf):
        n_x = n_tiles * tm
        for j in range(tm):
            for k in range(MOE_TOPK):
                pos = dest_ref[k * n_x + tile * tm + j]
                pltpu.make_async_copy(ys_hbm.at[pl.ds(pos, 1)], ybuf.at[buf, k, pl.ds(j, 1)], sem.at[buf]).start()

    @pl.when(t == 0)
    def _():
        issue(0, 0)

    @pl.when(t + 1 < n_tiles)
    def _():
        issue(t + 1, (t + 1) % 2)

    buf = t % 2
    for k in range(MOE_TOPK):
        pltpu.make_async_copy(ys_hbm.at[pl.ds(0, tm)], ybuf.at[buf, k], sem.at[buf]).wait()
    moe = gate_ref[:, 0:1] * ybuf[buf, 0] + gate_ref[:, 1:2] * ybuf[buf, 1]
    h = h_ref[...] + moe
    o_ref[...] = h * lax.rsqrt(jnp.mean(h * h, axis=-1, keepdims=True) + EPS) * g_ref[...]


def _final(h2, ys, dest_x, gates, g_final, batch, lp, seq):
    tp, d = h2.shape
    tm = SSD_CHUNK
    skip = (lp - seq) // tm
    per_b = seq // tm
    src = lambda w: pl.BlockSpec((tm, w), lambda t, dest: ((t // per_b) * (lp // tm) + skip + t % per_b, 0))
    grid_spec = pltpu.PrefetchScalarGridSpec(
        num_scalar_prefetch=1,
        grid=(batch * per_b,),
        in_specs=[src(d), src(MOE_TOPK), pl.BlockSpec((1, d), lambda t, dest: (0, 0)),
                  pl.BlockSpec(memory_space=pl.ANY)],
        out_specs=pl.BlockSpec((tm, d), lambda t, dest: (t, 0)),
        scratch_shapes=[pltpu.VMEM((2, MOE_TOPK, tm, d), jnp.float32), pltpu.SemaphoreType.DMA((2,))],
    )
    return pl.pallas_call(
        functools.partial(_final_kernel, tm),
        grid_spec=grid_spec,
        out_shape=jax.ShapeDtypeStruct((batch * seq, d), jnp.float32),
        compiler_params=_params(("arbitrary",)),
        name="final_norm",
    )(dest_x, h2, gates, g_final, ys)


def kernel(x, meta_tokens, norm_mix, w_in, ssd_conv_w, ssd_conv_b, ssd_dt_bias, ssd_a_log, ssd_d, ssd_norm, w_ssd_out, lru_conv_w, lru_conv_b, lru_wa, lru_ba, lru_wx, lru_bx, lru_lambda, w_lru_out, gate_bias, w_out, norm_ffn, w_router_group, w_router_expert, w_exp_gate, w_exp_up, w_exp_down, norm_final):
    batch, seq, d = x.shape
    depth = norm_mix.shape[0]
    assert depth == 1 and seq % SSD_CHUNK == 0
    l = N_META + seq
    lp = -(-l // SSD_CHUNK) * SSD_CHUNK
    n_pad = lp - l
    d_inner = d
    heads = d_inner // SSD_HEAD_DIM
    conv_dim = d_inner + 2 * SSD_GROUPS * SSD_STATE
    col_xbc = d_inner
    col_dt = col_xbc + conv_dim
    col_lx = col_dt + heads
    f32 = jnp.float32
    lyr = 0

    meta = jnp.broadcast_to(meta_tokens.astype(f32)[None], (batch, N_META, d))
    hp = jnp.concatenate([jnp.zeros((batch, n_pad, d), f32), meta, x], axis=1).reshape(batch * lp, d)

    w = w_in[lyr]
    w_all = w.astype(MXU_DTYPE)
    w_b = w_all[:, col_lx:]
    w_dt = jnp.zeros((d, LANES), MXU_DTYPE).at[:, :heads].set(w_all[:, col_dt:col_lx])
    dtb = jnp.zeros((1, LANES), f32).at[0, :heads].set(ssd_dt_bias[lyr])
    alog = jnp.zeros((1, LANES), f32).at[0, :heads].set(ssd_a_log[lyr])
    proj, dt, acum = _norm_inproj(hp, norm_mix[lyr][None, :], w_all, col_dt, w_b, w_dt, dtb, alog)
    n_main = col_dt + w_b.shape[1]
    proj3 = proj.reshape(batch, lp, n_main)
    lx_col = col_dt
    ly_col = lx_col + d
    gl_col = ly_col + d

    ys = _ssd(proj3, dt.reshape(batch, lp, LANES), acum.reshape(batch, lp, LANES), n_pad, ssd_conv_w[lyr],
              ssd_conv_b[lyr], ssd_d[lyr], ssd_norm[lyr], d_inner)
    yl = _lru(proj3, n_pad, lx_col, ly_col, lru_conv_w[lyr], lru_conv_b[lyr], lru_wa[lyr].astype(MXU_DTYPE),
              lru_ba[lyr], lru_wx[lyr].astype(MXU_DTYPE), lru_bx[lyr], lru_lambda[lyr], d)

    mixed = _merge(ys.reshape(batch * lp, d), yl.reshape(batch * lp, d), w_ssd_out[lyr].astype(MXU_DTYPE),
                   w_lru_out[lyr].astype(MXU_DTYPE), proj, gl_col, gate_bias[lyr])

    w_router = jnp.zeros((d, LANES), f32)
    w_router = w_router.at[:, :MOE_GROUPS].set(w_router_group[lyr])
    w_router = w_router.at[:, MOE_GROUPS:MOE_GROUPS + MOE_EXPERTS].set(w_router_expert[lyr])
    real_rows = jnp.asarray(((np.arange(batch * lp) % lp) >= n_pad).astype(np.float32)[:, None])
    h2, u2, gates, info, counts = _outproj_router(mixed, hp, w_out[lyr].astype(MXU_DTYPE), norm_ffn[lyr][None, :],
                                                  w_router.astype(MXU_DTYPE), real_rows)

    n_asg = batch * l * MOE_TOPK
    p_max = n_asg + MOE_EXPERTS * (MOE_BLOCK - 1)
    n_items_max = MOE_EXPERTS + p_max // EXPERT_ROWS
    p_len = -(-p_max // MOE_BLOCK) * MOE_BLOCK
    sched, dest = _expert_schedule(info, counts[0, :MOE_EXPERTS], batch, lp, n_pad, n_items_max, p_len)
    ff = w_exp_gate.shape[-1]
    ys_sorted = _experts(u2, sched, w_exp_gate.reshape(MOE_EXPERTS, d, ff), w_exp_up.reshape(MOE_EXPERTS, d, ff),
                         w_exp_down.reshape(MOE_EXPERTS, ff, d), n_items_max, p_len, (batch, l, lp, n_pad))

    dest_x = dest[:, :, N_META:].reshape(-1)
    out = _final(h2, ys_sorted, dest_x, gates, norm_final[None, :], batch, lp, seq)
    return out.reshape(batch, seq, d)
```

```python
import functools
import math

import numpy as np
import jax
import jax.numpy as jnp
from jax import lax
from jax.experimental import pallas as pl
from jax.experimental.pallas import tpu as pltpu

N_META = 16
CONV_K = 4
EPS = 1e-6
SSD_HEAD_DIM = 64
SSD_HEAD_SHIFT = 6
SSD_GROUPS = 8
SSD_STATE = 128
SSD_CHUNK = 128
LRU_HEADS = 8
LRU_C = 8.0
N_BRANCH = 2
MOE_GROUPS = 8
MOE_EXP_PER_GROUP = 8
MOE_EXPERTS = MOE_GROUPS * MOE_EXP_PER_GROUP
MOE_TOPK = 2
MOE_BLOCK = 64

LANES = 128
SUBLANES = 8
VMEM_LIMIT = 56 * 1024 * 1024
MXU_DTYPE = jnp.bfloat16

EXPERT_ROWS = 1024
EXPERT_ROW_VARIANTS = (256, 512, 576, 640, 1024)
EXPERT_FF_TILE = 512
EXPERT_OUT_TILE = 1024
GATHER_UNROLL = 16
SSD_UNROLL = 2


def _pick(n, options):
    for o in options:
        if n % o == 0:
            return o
    raise ValueError(f"no tile in {options} divides {n}")


def _params(sem, vmem=VMEM_LIMIT):
    return pltpu.CompilerParams(dimension_semantics=sem, vmem_limit_bytes=vmem)


def _dot(a, b):
    return jnp.dot(a.astype(MXU_DTYPE), b.astype(MXU_DTYPE), preferred_element_type=jnp.float32)


def _dot_exact_rhs(v, sel):
    sel = sel.astype(jnp.bfloat16)
    hi = v.astype(jnp.bfloat16)
    r1 = v - hi.astype(jnp.float32)
    mid = r1.astype(jnp.bfloat16)
    lo = (r1 - mid.astype(jnp.float32)).astype(jnp.bfloat16)
    f32 = jnp.float32
    return (jnp.dot(hi, sel, preferred_element_type=f32) + jnp.dot(mid, sel, preferred_element_type=f32)
            + jnp.dot(lo, sel, preferred_element_type=f32))


def _dot_exact_lhs(sel, v):
    sel = sel.astype(jnp.bfloat16)
    hi = v.astype(jnp.bfloat16)
    r1 = v - hi.astype(jnp.float32)
    mid = r1.astype(jnp.bfloat16)
    lo = (r1 - mid.astype(jnp.float32)).astype(jnp.bfloat16)
    f32 = jnp.float32
    return (jnp.dot(sel, hi, preferred_element_type=f32) + jnp.dot(sel, mid, preferred_element_type=f32)
            + jnp.dot(sel, lo, preferred_element_type=f32))


def _softplus(x):
    return jnp.maximum(x, 0.0) + jnp.log1p(jnp.exp(-jnp.abs(x)))


def _sigmoid(x):
    return 1.0 / (1.0 + jnp.exp(-x))


def _silu(x):
    return x * _sigmoid(x)


def _causal_conv_chunk(x_ref, w_ref, b_ref, c):
    q = SSD_CHUNK
    if isinstance(c, int) and c > 0:
        acc = b_ref[...] + w_ref[CONV_K - 1:CONV_K, :] * x_ref[0, pl.ds(c * q, q), :]
        for k in range(1, CONV_K):
            acc = acc + w_ref[CONV_K - 1 - k:CONV_K - k, :] * x_ref[0, pl.ds(c * q - k, q), :]
        return acc
    c = jnp.int32(c) if isinstance(c, int) else c
    r0 = pl.multiple_of(c * q, q)
    cur = x_ref[0, pl.ds(r0, q), :]
    prev = x_ref[0, pl.ds(pl.multiple_of(jnp.maximum(r0 - SUBLANES, 0), SUBLANES), SUBLANES), :]
    prev = jnp.where(c > 0, prev, 0.0)
    sub = lax.broadcasted_iota(jnp.int32, prev.shape, 0)
    acc = b_ref[...] + w_ref[CONV_K - 1:CONV_K, :] * cur
    for k in range(1, CONV_K):
        rolled = pltpu.roll(cur, k, 0)
        head = jnp.where(sub < k, pltpu.roll(prev, k, 0), rolled[:SUBLANES])
        shifted = jnp.concatenate([head, rolled[SUBLANES:]], axis=0)
        acc = acc + w_ref[CONV_K - 1 - k:CONV_K - k, :] * shifted
    return acc


def _norm_inproj_kernel(n_a, x_ref, g_ref, wa_ref, wb_ref, wdt_ref, dtb_ref, alog_ref, o_ref, dt_ref, acum_ref, xn_ref):
    j = pl.program_id(1)

    @pl.when(j == 0)
    def _():
        x = x_ref[...]
        y = x * lax.rsqrt(jnp.mean(x * x, axis=-1, keepdims=True) + EPS)
        xn_ref[...] = (y * g_ref[...]).astype(xn_ref.dtype)
        raw = jnp.dot(xn_ref[...], wdt_ref[...], preferred_element_type=jnp.float32)
        dt = _softplus(raw + dtb_ref[...])
        dt_ref[...] = dt
        adt = dt * (-jnp.exp(alog_ref[...]))
        q = SSD_CHUNK
        tri = (lax.broadcasted_iota(jnp.int32, (q, q), 0) >= lax.broadcasted_iota(jnp.int32, (q, q), 1))
        for c in range(x.shape[0] // q):
            acum_ref[c * q:(c + 1) * q, :] = _dot_exact_lhs(tri.astype(jnp.float32), adt[c * q:(c + 1) * q, :])

    @pl.when(j < n_a)
    def _():
        o_ref[...] = jnp.dot(xn_ref[...], wa_ref[...], preferred_element_type=jnp.float32)

    @pl.when(j >= n_a)
    def _():
        o_ref[...] = jnp.dot(xn_ref[...], wb_ref[...], preferred_element_type=jnp.float32)


def _norm_inproj(hp, g, w_a, cols_a, w_b, wdt, dtb, alog):
    tp, d = hp.shape
    n = cols_a + w_b.shape[1]
    tm = _pick(tp, (1024, 512, 256, 128))
    tn = math.gcd(_pick(cols_a, (1024, 512, 256, 128)), _pick(w_b.shape[1], (1024, 512, 256, 128)))
    n_a = cols_a // tn
    vec = pl.BlockSpec((1, LANES), lambda i, j: (0, 0))
    return pl.pallas_call(
        functools.partial(_norm_inproj_kernel, n_a),
        grid=(tp // tm, n // tn),
        in_specs=[
            pl.BlockSpec((tm, d), lambda i, j: (i, 0)),
            pl.BlockSpec((1, d), lambda i, j: (0, 0)),
            pl.BlockSpec((d, tn), lambda i, j: (0, jnp.minimum(j, n_a - 1))),
            pl.BlockSpec((d, tn), lambda i, j: (0, jnp.maximum(j - n_a, 0))),
            pl.BlockSpec((d, LANES), lambda i, j: (0, 0)),
            vec, vec,
        ],
        out_specs=[
            pl.BlockSpec((tm, tn), lambda i, j: (i, j)),
            pl.BlockSpec((tm, LANES), lambda i, j: (i, 0)),
            pl.BlockSpec((tm, LANES), lambda i, j: (i, 0)),
        ],
        out_shape=[
            jax.ShapeDtypeStruct((tp, n), jnp.float32),
            jax.ShapeDtypeStruct((tp, LANES), jnp.float32),
            jax.ShapeDtypeStruct((tp, LANES), jnp.float32),
        ],
        scratch_shapes=[pltpu.VMEM((tm, d), MXU_DTYPE)],
        compiler_params=_params(("parallel", "arbitrary")),
        name="norm_inproj",
    )(hp, g, w_a, w_b, wdt, dtb, alog)


def _ssd_kernel(n_pad, z_ref, xs_ref, b_ref, c_ref, dt_ref, acum_ref,
                cwx_ref, cwb_ref, cwc_ref, cbx_ref, cbb_ref, cbc_ref,
                d_ref, nw_ref, o_ref, state_s):
    g = pl.program_id(1)
    lp = xs_ref.shape[1]
    q = SSD_CHUNK
    gw = xs_ref.shape[2]
    hpg = gw // SSD_HEAD_DIM
    f32 = jnp.float32

    rows = lax.broadcasted_iota(jnp.int32, (LANES, gw), 0)
    cols = lax.broadcasted_iota(jnp.int32, (LANES, gw), 1)
    expand = (rows == g * hpg + lax.shift_right_logical(cols, SSD_HEAD_SHIFT)).astype(f32)

    state_s[...] = jnp.zeros_like(state_s)
    li = lax.broadcasted_iota(jnp.int32, (q, q), 0)
    si = lax.broadcasted_iota(jnp.int32, (q, q), 1)
    causal = li >= si
    lane = lax.broadcasted_iota(jnp.int32, (q, gw), 1)
    row = lax.broadcasted_iota(jnp.int32, (q, gw), 0)

    def chunk(c, carry):
        r0 = c * q
        sl = pl.ds(r0, q)
        xs = _silu(_causal_conv_chunk(xs_ref, cwx_ref, cbx_ref, c))
        bm = _silu(_causal_conv_chunk(b_ref, cwb_ref, cbb_ref, c))
        cm = _silu(_causal_conv_chunk(c_ref, cwc_ref, cbc_ref, c))
        dt_g = _dot_exact_rhs(dt_ref[0, sl, :], expand)
        acum = _dot_exact_rhs(acum_ref[0, sl, :], expand)
        xdt = xs * dt_g
        if r0 < n_pad:
            xdt = jnp.where(row + r0 >= n_pad, xdt, 0.0)
        acum_t = acum.T
        a_last = acum[q - 1:q, :]

        cb = lax.dot_general(cm.astype(MXU_DTYPE), bm.astype(MXU_DTYPE),
                             (((1,), (1,)), ((), ())), preferred_element_type=f32)
        y = d_ref[...] * xs
        for j in range(hpg):
            c0 = j * SSD_HEAD_DIM
            seg = acum[:, c0:c0 + 1] - acum_t[c0:c0 + 1, :]
            decay = jnp.exp(jnp.where(causal, seg, -jnp.inf))
            in_head = (lane >= c0) & (lane < c0 + SSD_HEAD_DIM)
            y = y + _dot(cb * decay, jnp.where(in_head, xdt, 0.0))
        x_end = xdt * jnp.exp(a_last - acum)

        state = state_s[...]
        y = y + _dot(cm, state) * jnp.exp(acum)
        state_s[...] = state * jnp.exp(a_last) + _dot(bm.T, x_end)

        yz = y * _silu(z_ref[0, sl, :])
        yn = yz * lax.rsqrt(jnp.mean(yz * yz, axis=-1, keepdims=True) + EPS)
        o_ref[0, sl, :] = (yn * nw_ref[...]).astype(o_ref.dtype)
        return carry

    for c in range(lp // q):
        chunk(c, 0)


def _ssd(proj3, dt3, acum3, n_pad, cw, cb, d_skip, norm_w, d_inner):
    b, lp, _ = proj3.shape
    gw = d_inner // SSD_GROUPS
    ns = SSD_STATE
    xs_blk0 = d_inner // gw
    b_blk0 = 2 * d_inner // ns
    c_blk0 = (2 * d_inner + SSD_GROUPS * ns) // ns

    cwx = cw[:, :d_inner]
    cwb = cw[:, d_inner:d_inner + SSD_GROUPS * ns]
    cwc = cw[:, d_inner + SSD_GROUPS * ns:]
    cbx = cb[None, :d_inner]
    cbb = cb[None, d_inner:d_inner + SSD_GROUPS * ns]
    cbc = cb[None, d_inner + SSD_GROUPS * ns:]
    d_ch = jnp.repeat(d_skip, SSD_HEAD_DIM)[None, :]
    nw = norm_w[None, :]

    seq = lambda w, off: pl.BlockSpec((1, lp, w), lambda i, g: (i, 0, off + g))
    head = pl.BlockSpec((1, lp, LANES), lambda i, g: (i, 0, 0))
    vec = lambda w: pl.BlockSpec((1, w), lambda i, g: (0, g))
    cwspec = lambda w: pl.BlockSpec((CONV_K, w), lambda i, g: (0, g))
    return pl.pallas_call(
        functools.partial(_ssd_kernel, n_pad),
        grid=(b, SSD_GROUPS),
        in_specs=[
            seq(gw, 0), seq(gw, xs_blk0), seq(ns, b_blk0), seq(ns, c_blk0), head, head,
            cwspec(gw), cwspec(ns), cwspec(ns), vec(gw), vec(ns), vec(ns),
            vec(gw), vec(gw),
        ],
        out_specs=pl.BlockSpec((1, lp, gw), lambda i, g: (i, 0, g)),
        out_shape=jax.ShapeDtypeStruct((b, lp, d_inner), MXU_DTYPE),
        scratch_shapes=[pltpu.VMEM((ns, gw), jnp.float32)],
        compiler_params=_params(("parallel", "parallel")),
        name="ssd",
    )(proj3, proj3, proj3, proj3, dt3, acum3, cwx, cwb, cwc, cbx, cbb, cbc, d_ch, nw)


def _lru_kernel(n_pad, lx_ref, ly_ref, cw_ref, cb_ref, wa_ref, ba_ref, wx_ref, bx_ref, lam_ref, o_ref):
    lp = lx_ref.shape[1]
    w = lx_ref.shape[2]
    q = SSD_CHUNK
    neg_c_softplus = (-LRU_C) * _softplus(-lam_ref[...])
    row = lax.broadcasted_iota(jnp.int32, (q, w), 0)
    sub = lax.broadcasted_iota(jnp.int32, (SUBLANES, w), 0)

    def chunk(c, h_prev):
        r0 = c * q
        sl = pl.ds(r0, q)
        xr = _causal_conv_chunk(lx_ref, cw_ref, cb_ref, c)
        gate_r = _sigmoid(_dot(xr, wa_ref[0]) + ba_ref[...])
        gate_i = _sigmoid(_dot(xr, wx_ref[0]) + bx_ref[...])
        log_a = gate_r * neg_c_softplus
        a = jnp.exp(log_a)
        mult = jnp.sqrt(jnp.tanh(-log_a) * (a * a + 1.0))
        u = mult * gate_i * xr
        if r0 < n_pad:
            u = jnp.where(row + r0 >= n_pad, u, 0.0)
        gate_y = jax.nn.gelu(ly_ref[0, sl, :])

        out = []
        for t in range(q // SUBLANES):
            rows8 = slice(t * SUBLANES, (t + 1) * SUBLANES)
            at, ut = a[rows8], u[rows8]
            for d in (1, 2, 4):
                keep = sub >= d
                ut = jnp.where(keep, at * pltpu.roll(ut, d, 0) + ut, ut)
                at = jnp.where(keep, at * pltpu.roll(at, d, 0), at)
            h = at * h_prev + ut
            h_prev = jnp.broadcast_to(h[SUBLANES - 1:SUBLANES, :], (SUBLANES, w))
            out.append(h * gate_y[rows8])
        o_ref[0, sl, :] = jnp.concatenate(out, axis=0).astype(o_ref.dtype)
        return h_prev

    h_carry = jnp.zeros((SUBLANES, w), jnp.float32)
    for c in range(lp // q):
        h_carry = chunk(c, h_carry)


def _lru(proj3, n_pad, lx_col, ly_col, cw, cb, wa, ba, wx, bx, lam, width):
    b, lp, _ = proj3.shape
    w = width // LRU_HEADS
    seq = lambda off: pl.BlockSpec((1, lp, w), lambda i, h: (i, 0, off + h))
    vec = pl.BlockSpec((1, w), lambda i, h: (0, h))
    mat = pl.BlockSpec((1, w, w), lambda i, h: (h, 0, 0))
    return pl.pallas_call(
        functools.partial(_lru_kernel, n_pad),
        grid=(b, LRU_HEADS),
        in_specs=[seq(lx_col // w), seq(ly_col // w),
                  pl.BlockSpec((CONV_K, w), lambda i, h: (0, h)), vec, mat, vec, mat, vec, vec],
        out_specs=pl.BlockSpec((1, lp, w), lambda i, h: (i, 0, h)),
        out_shape=jax.ShapeDtypeStruct((b, lp, width), MXU_DTYPE),
        compiler_params=_params(("parallel", "parallel")),
        name="rglru",
    )(proj3, proj3, cw, cb[None, :], wa, ba[None, :], wx, bx[None, :], lam[None, :])


def _merge_kernel(ys_ref, yl_ref, ws_ref, wl_ref, g0_ref, g1_ref, gb_ref, o_ref):
    y_ssd = jnp.dot(ys_ref[...], ws_ref[...], preferred_element_type=jnp.float32)
    y_lru = jnp.dot(yl_ref[...], wl_ref[...], preferred_element_type=jnp.float32)
    gate0 = _sigmoid(g0_ref[...] + gb_ref[0:1, :])
    gate1 = _sigmoid(g1_ref[...] + gb_ref[1:2, :])
    o_ref[...] = (gate0 * y_ssd + gate1 * y_lru).astype(o_ref.dtype)


def _merge(ys, yl, ws, wl, proj, gate_col, gate_bias):
    tp, d = ys.shape
    tm = _pick(tp, (1024, 512, 256, 128))
    tn = _pick(d, (512, 256, 128))
    g0 = gate_col // tn
    g1 = (gate_col + d) // tn
    return pl.pallas_call(
        _merge_kernel,
        grid=(tp // tm, d // tn),
        in_specs=[
            pl.BlockSpec((tm, d), lambda i, j: (i, 0)),
            pl.BlockSpec((tm, d), lambda i, j: (i, 0)),
            pl.BlockSpec((d, tn), lambda i, j: (0, j)),
            pl.BlockSpec((d, tn), lambda i, j: (0, j)),
            pl.BlockSpec((tm, tn), lambda i, j: (i, g0 + j)),
            pl.BlockSpec((tm, tn), lambda i, j: (i, g1 + j)),
            pl.BlockSpec((N_BRANCH, tn), lambda i, j: (0, j)),
        ],
        out_specs=pl.BlockSpec((tm, tn), lambda i, j: (i, j)),
        out_shape=jax.ShapeDtypeStruct((tp, d), MXU_DTYPE),
        compiler_params=_params(("parallel", "arbitrary")),
        name="merge",
    )(ys, yl, ws, wl, proj, proj, gate_bias)


def _first_index_of_max(p, valid, lane):
    pm = jnp.where(valid, p, -jnp.inf)
    top = jnp.max(pm, axis=-1, keepdims=True)
    idx = jnp.min(jnp.where(valid & (pm == top), lane, LANES), axis=-1, keepdims=True)
    return top, idx


def _masked_softmax(x, valid):
    m = jnp.max(jnp.where(valid, x, -jnp.inf), axis=-1, keepdims=True)
    e = jnp.where(valid, jnp.exp(x - m), 0.0)
    return e / jnp.sum(e, axis=-1, keepdims=True)


def _outproj_router_kernel(mix_ref, h_ref, wo_ref, g_ref, wr_ref, real_ref,
                           h2_ref, u_ref, gate_ref, info_ref, counts_ref, counts_s):
    @pl.when(pl.program_id(0) == 0)
    def _():
        counts_s[...] = jnp.zeros_like(counts_s)

    h2 = h_ref[...] + jnp.dot(mix_ref[...], wo_ref[...], preferred_element_type=jnp.float32)
    h2_ref[...] = h2
    u = h2 * lax.rsqrt(jnp.mean(h2 * h2, axis=-1, keepdims=True) + EPS) * g_ref[...]
    u_ref[...] = u
    logits = _dot(u, wr_ref[...])
    lane = lax.broadcasted_iota(jnp.int32, logits.shape, 1)

    g_prob = _masked_softmax(logits, lane < MOE_GROUPS)
    g_p, g_idx = _first_index_of_max(g_prob, lane < MOE_GROUPS, lane)

    e_lo = MOE_GROUPS + g_idx * MOE_EXP_PER_GROUP
    in_group = (lane >= e_lo) & (lane < e_lo + MOE_EXP_PER_GROUP)
    e_prob = _masked_softmax(logits, in_group)
    p1, i1 = _first_index_of_max(e_prob, in_group, lane)
    rest = in_group & (lane != i1)
    p2, i2 = _first_index_of_max(e_prob, rest, lane)
    denom = p1 + p2

    col = lax.broadcasted_iota(jnp.int32, gate_ref.shape, 1)
    e1 = i1 - MOE_GROUPS
    e2 = i2 - MOE_GROUPS
    gate_ref[...] = jnp.where(col == 0, g_p * p1 / denom, g_p * p2 / denom)

    m = logits.shape[0]
    hit1 = (lane == e1) & (real_ref[...] > 0.0)
    hit2 = (lane == e2) & (real_ref[...] > 0.0)
    onehot = jnp.where(hit1 | hit2, 1.0, 0.0)
    earlier = (lax.broadcasted_iota(jnp.int32, (m, m), 0) > lax.broadcasted_iota(jnp.int32, (m, m), 1))
    before = counts_s[...] + jnp.dot(earlier.astype(jnp.bfloat16), onehot.astype(jnp.bfloat16),
                                     preferred_element_type=jnp.float32)
    rank1 = jnp.sum(jnp.where(hit1, before, 0.0), axis=-1, keepdims=True)
    rank2 = jnp.sum(jnp.where(hit2, before, 0.0), axis=-1, keepdims=True)
    counts_s[...] += jnp.sum(onehot, axis=0, keepdims=True)
    counts_ref[...] = counts_s[...].astype(jnp.int32)
    f32 = jnp.float32
    info = jnp.where(lane == 0, e1.astype(f32), jnp.where(lane == 1, e2.astype(f32),
                     jnp.where(lane == 2, rank1, jnp.where(lane == 3, rank2, 0.0))))
    info_ref[...] = info.T[:SUBLANES, :]


def _outproj_router(mixed, hp, wo, g_ffn, wr, real_rows):
    tp, d = hp.shape
    tm = _pick(tp, (512, 256, 128))
    row = lambda w: pl.BlockSpec((tm, w), lambda i: (i, 0))
    return pl.pallas_call(
        _outproj_router_kernel,
        grid=(tp // tm,),
        in_specs=[row(d), row(d),
                  pl.BlockSpec((d, d), lambda i: (0, 0)),
                  pl.BlockSpec((1, d), lambda i: (0, 0)),
                  pl.BlockSpec((d, LANES), lambda i: (0, 0)),
                  row(1)],
        out_specs=[row(d), row(d), row(MOE_TOPK),
                   pl.BlockSpec((SUBLANES, tm), lambda i: (0, i)),
                   pl.BlockSpec((1, LANES), lambda i: (0, 0))],
        out_shape=[jax.ShapeDtypeStruct((tp, d), jnp.float32),
                   jax.ShapeDtypeStruct((tp, d), jnp.float32),
                   jax.ShapeDtypeStruct((tp, MOE_TOPK), jnp.float32),
                   jax.ShapeDtypeStruct((SUBLANES, tp), jnp.float32),
                   jax.ShapeDtypeStruct((1, LANES), jnp.int32)],
        scratch_shapes=[pltpu.VMEM((1, LANES), jnp.float32)],
        compiler_params=_params(("arbitrary",)),
        name="outproj_router",
    )(mixed, hp, wo, g_ffn, wr, real_rows)


def _wait_rows(n, make_copy):
    for k in range(EXPERT_ROWS.bit_length() - 1, -1, -1):
        @pl.when((n & (1 << k)) != 0)
        def _():
            make_copy(1 << k).wait()


def _expert_kernel(layout, n_items_ref, n_used_ref, item_e_ref, item_start_ref, item_rows_ref, item_real_ref,
                   dest_ref, u_hbm, w1_ref, w3_ref, w2_ref, y_hbm,
                   xg_s, xb_s, acc_s, tok_ref, sem):
    n_batch, l_real, lp, n_pad = layout
    i = pl.program_id(0)
    f = pl.program_id(1)
    nf = pl.num_programs(1)
    n_items = n_items_ref[0]
    active = i < n_items
    n_rows = item_rows_ref[i]
    d = acc_s.shape[1]
    blk = MOE_BLOCK

    def issue_gather(item):
        start = item_start_ref[item]
        n = item_real_ref[item]
        n_groups = n // GATHER_UNROLL

        def one(r):
            pltpu.make_async_copy(u_hbm.at[pl.ds(tok_ref[start + r], 1)], xg_s.at[pl.ds(r, 1)], sem.at[0]).start()

        def group(g, c):
            for j in range(GATHER_UNROLL):
                one(g * GATHER_UNROLL + j)
            return c
        lax.fori_loop(0, n_groups, group, 0)

        def single(r, c):
            one(r)
            return c
        lax.fori_loop(n_groups * GATHER_UNROLL, n, single, 0)

    def wait_gather(item):
        _wait_rows(item_real_ref[item],
                   lambda m: pltpu.make_async_copy(u_hbm.at[pl.ds(0, m)], xg_s.at[pl.ds(0, m)], sem.at[0]))

    def out_copy(item, j):
        r0 = pl.multiple_of(j * blk, blk)
        dst0 = pl.multiple_of(item_start_ref[item] + r0, blk)
        return pltpu.make_async_copy(acc_s.at[pl.ds(r0, blk)], y_hbm.at[pl.ds(dst0, blk)], sem.at[1])

    def issue_out(item):
        def body(j, c):
            out_copy(item, j).start()
            return c
        lax.fori_loop(0, item_rows_ref[item] // blk, body, 0)

    def wait_out(item):
        def body(j, c):
            out_copy(item, j).wait()
            return c
        lax.fori_loop(0, item_rows_ref[item] // blk, body, 0)

    @pl.when((i == 0) & (f == 0))
    def _():
        xg_s[...] = jnp.zeros_like(xg_s)
        acc_s[pl.ds(0, blk), :] = jnp.zeros((blk, d), acc_s.dtype)
        n_used = n_used_ref[0]
        n_blocks = y_hbm.shape[0] // blk

        def tail_copy(j):
            return pltpu.make_async_copy(acc_s.at[pl.ds(0, blk)], y_hbm.at[pl.ds(pl.multiple_of(j * blk, blk), blk)],
                                         sem.at[1])

        def fill(j, c):
            tail_copy(j).start()
            return c
        lax.fori_loop(n_used, n_blocks, fill, 0)

        def drain(j, c):
            tail_copy(j).wait()
            return c
        lax.fori_loop(n_used, n_blocks, drain, 0)

        for k in range(MOE_TOPK):
            for b in range(n_batch):
                base = (k * n_batch + b) * l_real
                row0 = b * lp + n_pad

                def invert(g, c, base=base, row0=row0):
                    for j in range(GATHER_UNROLL):
                        p = g * GATHER_UNROLL + j
                        tok_ref[dest_ref[base + p]] = row0 + p
                    return c
                lax.fori_loop(0, l_real // GATHER_UNROLL, invert, 0)

        @pl.when(active)
        def _():
            issue_gather(0)

    @pl.when(active & (f == 0))
    def _():
        wait_gather(i)
        for lo, m in zip((0,) + EXPERT_ROW_VARIANTS, EXPERT_ROW_VARIANTS):
            @pl.when((n_rows > lo) & (n_rows <= m))
            def _():
                xb_s[pl.ds(0, m), :] = xg_s[pl.ds(0, m), :].astype(xb_s.dtype)

        @pl.when(i + 1 < n_items)
        def _():
            issue_gather(i + 1)

    @pl.when((f == 0) & (i >= 1) & (i - 1 < n_items) & jnp.logical_not(active))
    def _():
        wait_out(i - 1)

    for lo, m in zip((0,) + EXPERT_ROW_VARIANTS, EXPERT_ROW_VARIANTS):
        @pl.when(active & (n_rows > lo) & (n_rows <= m))
        def _():
            rows = pl.ds(0, m)
            x = xb_s[rows, :]
            a = jnp.dot(x, w1_ref[0].astype(MXU_DTYPE), preferred_element_type=jnp.float32)
            b = jnp.dot(x, w3_ref[0].astype(MXU_DTYPE), preferred_element_type=jnp.float32)
            hdn = (_silu(a) * b).astype(MXU_DTYPE)

            @pl.when((f == 0) & (i >= 1))
            def _():
                wait_out(i - 1)

            for c0 in range(0, d, EXPERT_OUT_TILE):
                cols = pl.ds(c0, EXPERT_OUT_TILE)
                part = jnp.dot(hdn, w2_ref[0, :, cols].astype(MXU_DTYPE), preferred_element_type=jnp.float32)

                @pl.when(f == 0)
                def _():
                    acc_s[rows, cols] = part

                @pl.when(f > 0)
                def _():
                    acc_s[rows, cols] += part

    @pl.when(f == nf - 1)
    def _():
        @pl.when(active)
        def _():
            issue_out(i)

        @pl.when(active & (i == pl.num_programs(0) - 1))
        def _():
            wait_out(i)


def _experts(u, sched, w1, w3, w2, n_items_max, p_len, layout):
    tp, d = u.shape
    ff = w1.shape[2]
    tf = EXPERT_FF_TILE
    nf = ff // tf
    n_items, n_used, item_e, item_start, item_rows, item_real, dest = sched
    assert layout[1] % GATHER_UNROLL == 0

    def w_in_map(i, f, n_items, n_used, item_e, *_):
        return (item_e[i], 0, jnp.where(i < n_items[0], f, nf - 1))

    def w_out_map(i, f, n_items, n_used, item_e, *_):
        return (item_e[i], jnp.where(i < n_items[0], f, nf - 1), 0)

    grid_spec = pltpu.PrefetchScalarGridSpec(
        num_scalar_prefetch=7,
        grid=(n_items_max, nf),
        in_specs=[
            pl.BlockSpec(memory_space=pl.ANY),
            pl.BlockSpec((1, d, tf), w_in_map),
            pl.BlockSpec((1, d, tf), w_in_map),
            pl.BlockSpec((1, tf, d), w_out_map),
        ],
        out_specs=pl.BlockSpec(memory_space=pl.ANY),
        scratch_shapes=[
            pltpu.VMEM((EXPERT_ROWS, d), jnp.float32),
            pltpu.VMEM((EXPERT_ROWS, d), MXU_DTYPE),
            pltpu.VMEM((EXPERT_ROWS, d), jnp.float32),
            pltpu.SMEM((p_len,), jnp.int32),
            pltpu.SemaphoreType.DMA((2,)),
        ],
    )
    return pl.pallas_call(
        functools.partial(_expert_kernel, layout),
        grid_spec=grid_spec,
        out_shape=jax.ShapeDtypeStruct((p_len, d), jnp.float32),
        compiler_params=_params(("arbitrary", "arbitrary")),
        name="experts",
    )(n_items, n_used, item_e, item_start, item_rows, item_real, dest, u, w1, w3, w2)


def _expert_schedule(info, counts, batch, lp, n_pad, n_items_max, p_len):
    i32 = jnp.int32
    eid = info[:MOE_TOPK].astype(i32)
    rank = info[MOE_TOPK:2 * MOE_TOPK].astype(i32)
    padded = (counts + MOE_BLOCK - 1) // MOE_BLOCK * MOE_BLOCK
    pad_end = jnp.cumsum(padded)
    pad_start = pad_end - padded
    experts = jnp.arange(MOE_EXPERTS, dtype=i32)[:, None, None]
    seg_start = jnp.sum(jnp.where(eid[None] == experts, pad_start[:, None, None], 0), axis=0)
    dest = (seg_start + rank).reshape(MOE_TOPK, batch, lp)[:, :, n_pad:]
    n_used = (pad_end[-1:] // MOE_BLOCK).astype(i32)

    chunks = (padded + EXPERT_ROWS - 1) // EXPERT_ROWS
    chunk_end = jnp.cumsum(chunks)
    item = jnp.arange(n_items_max, dtype=i32)
    item_e = jnp.minimum(jnp.sum((chunk_end[None, :] <= item[:, None]).astype(i32), axis=1), MOE_EXPERTS - 1)
    n_items = chunk_end[-1:].astype(i32)
    last_e = item_e[jnp.maximum(n_items[0] - 1, 0)]
    item_e = jnp.where(item < n_items[0], item_e, last_e)
    k = item - (chunk_end - chunks)[item_e]
    item_start = (pad_start[item_e] + k * EXPERT_ROWS).astype(i32)
    item_rows = jnp.clip(padded[item_e] - k * EXPERT_ROWS, 0, EXPERT_ROWS).astype(i32)
    item_real = jnp.clip(counts[item_e] - k * EXPERT_ROWS, 0, EXPERT_ROWS).astype(i32)
    live = item < n_items[0]
    item_start = jnp.where(live, item_start, 0)
    item_rows = jnp.where(live, item_rows, 0)
    item_real = jnp.where(live, item_real, 0)
    return (n_items, n_used, item_e, item_start, item_rows, item_real, dest.reshape(-1)), dest


def _final_kernel(tm, dest_ref, h_ref, gate_ref, g_ref, ys_hbm, o_ref, ybuf, sem):
    t = pl.program_id(0)
    n_tiles = pl.num_programs(0)

    def issue(tile, buf):
        n_x = n_tiles * tm
        for j in range(tm):
            for k in range(MOE_TOPK):
                pos = dest_ref[k * n_x + tile * tm + j]
                pltpu.make_async_copy(ys_hbm.at[pl.ds(pos, 1)], ybuf.at[buf, k, pl.ds(j, 1)], sem.at[buf]).start()

    @pl.when(t == 0)
    def _():
        issue(0, 0)

    @pl.when(t + 1 < n_tiles)
    def _():
        issue(t + 1, (t + 1) % 2)

    buf = t % 2
    for k in range(MOE_TOPK):
        pltpu.make_async_copy(ys_hbm.at[pl.ds(0, tm)], ybuf.at[buf, k], sem.at[buf]).wait()
    moe = gate_ref[:, 0:1] * ybuf[buf, 0] + gate_ref[:, 1:2] * ybuf[buf, 1]
    h = h_ref[...] + moe
    o_ref[...] = h * lax.rsqrt(jnp.mean(h * h, axis=-1, keepdims=True) + EPS) * g_ref[...]


def _final(h2, ys, dest_x, gates, g_final, batch, lp, seq):
    tp, d = h2.shape
    tm = SSD_CHUNK
    skip = (lp - seq) // tm
    per_b = seq // tm
    src = lambda w: pl.BlockSpec((tm, w), lambda t, dest: ((t // per_b) * (lp // tm) + skip + t % per_b, 0))
    grid_spec = pltpu.PrefetchScalarGridSpec(
        num_scalar_prefetch=1,
        grid=(batch * per_b,),
        in_specs=[src(d), src(MOE_TOPK), pl.BlockSpec((1, d), lambda t, dest: (0, 0)),
                  pl.BlockSpec(memory_space=pl.ANY)],
        out_specs=pl.BlockSpec((tm, d), lambda t, dest: (t, 0)),
        scratch_shapes=[pltpu.VMEM((2, MOE_TOPK, tm, d), jnp.float32), pltpu.SemaphoreType.DMA((2,))],
    )
    return pl.pallas_call(
        functools.partial(_final_kernel, tm),
        grid_spec=grid_spec,
        out_shape=jax.ShapeDtypeStruct((batch * seq, d), jnp.float32),
        compiler_params=_params(("arbitrary",)),
        name="final_norm",
    )(dest_x, h2, gates, g_final, ys)


def kernel(x, meta_tokens, norm_mix, w_in, ssd_conv_w, ssd_conv_b, ssd_dt_bias, ssd_a_log, ssd_d, ssd_norm, w_ssd_out, lru_conv_w, lru_conv_b, lru_wa, lru_ba, lru_wx, lru_bx, lru_lambda, w_lru_out, gate_bias, w_out, norm_ffn, w_router_group, w_router_expert, w_exp_gate, w_exp_up, w_exp_down, norm_final):
    batch, seq, d = x.shape
    depth = norm_mix.shape[0]
    assert depth == 1 and seq % SSD_CHUNK == 0
    l = N_META + seq
    lp = -(-l // SSD_CHUNK) * SSD_CHUNK
    n_pad = lp - l
    d_inner = d
    heads = d_inner // SSD_HEAD_DIM
    conv_dim = d_inner + 2 * SSD_GROUPS * SSD_STATE
    col_xbc = d_inner
    col_dt = col_xbc + conv_dim
    col_lx = col_dt + heads
    f32 = jnp.float32
    lyr = 0

    meta = jnp.broadcast_to(meta_tokens.astype(f32)[None], (batch, N_META, d))
    hp = jnp.concatenate([jnp.zeros((batch, n_pad, d), f32), meta, x], axis=1).reshape(batch * lp, d)

    w = w_in[lyr]
    w_all = w.astype(MXU_DTYPE)
    w_b = w_all[:, col_lx:]
    w_dt = jnp.zeros((d, LANES), MXU_DTYPE).at[:, :heads].set(w_all[:, col_dt:col_lx])
    dtb = jnp.zeros((1, LANES), f32).at[0, :heads].set(ssd_dt_bias[lyr])
    alog = jnp.zeros((1, LANES), f32).at[0, :heads].set(ssd_a_log[lyr])
    proj, dt, acum = _norm_inproj(hp, norm_mix[lyr][None, :], w_all, col_dt, w_b, w_dt, dtb, alog)
    n_main = col_dt + w_b.shape[1]
    proj3 = proj.reshape(batch, lp, n_main)
    lx_col = col_dt
    ly_col = lx_col + d
    gl_col = ly_col + d

    ys = _ssd(proj3, dt.reshape(batch, lp, LANES), acum.reshape(batch, lp, LANES), n_pad, ssd_conv_w[lyr],
              ssd_conv_b[lyr], ssd_d[lyr], ssd_norm[lyr], d_inner)
    yl = _lru(proj3, n_pad, lx_col, ly_col, lru_conv_w[lyr], lru_conv_b[lyr], lru_wa[lyr].astype(MXU_DTYPE),
              lru_ba[lyr], lru_wx[lyr].astype(MXU_DTYPE), lru_bx[lyr], lru_lambda[lyr], d)

    mixed = _merge(ys.reshape(batch * lp, d), yl.reshape(batch * lp, d), w_ssd_out[lyr].astype(MXU_DTYPE),
                   w_lru_out[lyr].astype(MXU_DTYPE), proj, gl_col, gate_bias[lyr])

    w_router = jnp.zeros((d, LANES), f32)
    w_router = w_router.at[:, :MOE_GROUPS].set(w_router_group[lyr])
    w_router = w_router.at[:, MOE_GROUPS:MOE_GROUPS + MOE_EXPERTS].set(w_router_expert[lyr])
    real_rows = jnp.asarray(((np.arange(batch * lp) % lp) >= n_pad).astype(np.float32)[:, None])
    h2, u2, gates, info, counts = _outproj_router(mixed, hp, w_out[lyr].astype(MXU_DTYPE), norm_ffn[lyr][None, :],
                                                  w_router.astype(MXU_DTYPE), real_rows)

    n_asg = batch * l * MOE_TOPK
    p_max = n_asg + MOE_EXPERTS * (MOE_BLOCK - 1)
    n_items_max = MOE_EXPERTS + p_max // EXPERT_ROWS
    p_len = -(-p_max // MOE_BLOCK) * MOE_BLOCK
    sched, dest = _expert_schedule(info, counts[0, :MOE_EXPERTS], batch, lp, n_pad, n_items_max, p_len)
    ff = w_exp_gate.shape[-1]
    ys_sorted = _experts(u2, sched, w_exp_gate.reshape(MOE_EXPERTS, d, ff), w_exp_up.reshape(MOE_EXPERTS, d, ff),
                         w_exp_down.reshape(MOE_EXPERTS, ff, d), n_items_max, p_len, (batch, l, lp, n_pad))

    dest_x = dest[:, :, N_META:].reshape(-1)
    out = _final(h2, ys_sorted, dest_x, gates, norm_final[None, :], batch, lp, seq)
    return out.reshape(batch, seq, d)
```

```python
import functools
import math

import numpy as np
import jax
import jax.numpy as jnp
from jax import lax
from jax.experimental import pallas as pl
from jax.experimental.pallas import tpu as pltpu

N_META = 16
CONV_K = 4
EPS = 1e-6
SSD_HEAD_DIM = 64
SSD_HEAD_SHIFT = 6
SSD_GROUPS = 8
SSD_STATE = 128
SSD_CHUNK = 128
LRU_HEADS = 8
LRU_C = 8.0
N_BRANCH = 2
MOE_GROUPS = 8
MOE_EXP_PER_GROUP = 8
MOE_EXPERTS = MOE_GROUPS * MOE_EXP_PER_GROUP
MOE_TOPK = 2
MOE_BLOCK = 64

LANES = 128
SUBLANES = 8
VMEM_LIMIT = 56 * 1024 * 1024
MXU_DTYPE = jnp.bfloat16

EXPERT_ROWS = 1024
EXPERT_ROW_VARIANTS = (256, 512, 576, 640, 1024)
EXPERT_FF_TILE = 512
EXPERT_OUT_TILE = 1024
GATHER_UNROLL = 16
SSD_UNROLL = 2


def _pick(n, options):
    for o in options:
        if n % o == 0:
            return o
    raise ValueError(f"no tile in {options} divides {n}")


def _params(sem, vmem=VMEM_LIMIT):
    return pltpu.CompilerParams(dimension_semantics=sem, vmem_limit_bytes=vmem)


def _dot(a, b):
    return jnp.dot(a.astype(MXU_DTYPE), b.astype(MXU_DTYPE), preferred_element_type=jnp.float32)


def _dot_exact_rhs(v, sel):
    sel = sel.astype(jnp.bfloat16)
    hi = v.astype(jnp.bfloat16)
    r1 = v - hi.astype(jnp.float32)
    mid = r1.astype(jnp.bfloat16)
    lo = (r1 - mid.astype(jnp.float32)).astype(jnp.bfloat16)
    f32 = jnp.float32
    return (jnp.dot(hi, sel, preferred_element_type=f32) + jnp.dot(mid, sel, preferred_element_type=f32)
            + jnp.dot(lo, sel, preferred_element_type=f32))


def _dot_exact_lhs(sel, v):
    sel = sel.astype(jnp.bfloat16)
    hi = v.astype(jnp.bfloat16)
    r1 = v - hi.astype(jnp.float32)
    mid = r1.astype(jnp.bfloat16)
    lo = (r1 - mid.astype(jnp.float32)).astype(jnp.bfloat16)
    f32 = jnp.float32
    return (jnp.dot(sel, hi, preferred_element_type=f32) + jnp.dot(sel, mid, preferred_element_type=f32)
            + jnp.dot(sel, lo, preferred_element_type=f32))


def _softplus(x):
    return jnp.maximum(x, 0.0) + jnp.log1p(jnp.exp(-jnp.abs(x)))


def _sigmoid(x):
    return 1.0 / (1.0 + jnp.exp(-x))


def _silu(x):
    return x * _sigmoid(x)


def _causal_conv_chunk(x_ref, w_ref, b_ref, c):
    q = SSD_CHUNK
    if isinstance(c, int) and c > 0:
        acc = b_ref[...] + w_ref[CONV_K - 1:CONV_K, :] * x_ref[0, pl.ds(c * q, q), :]
        for k in range(1, CONV_K):
            acc = acc + w_ref[CONV_K - 1 - k:CONV_K - k, :] * x_ref[0, pl.ds(c * q - k, q), :]
        return acc
    c = jnp.int32(c) if isinstance(c, int) else c
    r0 = pl.multiple_of(c * q, q)
    cur = x_ref[0, pl.ds(r0, q), :]
    prev = x_ref[0, pl.ds(pl.multiple_of(jnp.maximum(r0 - SUBLANES, 0), SUBLANES), SUBLANES), :]
    prev = jnp.where(c > 0, prev, 0.0)
    sub = lax.broadcasted_iota(jnp.int32, prev.shape, 0)
    acc = b_ref[...] + w_ref[CONV_K - 1:CONV_K, :] * cur
    for k in range(1, CONV_K):
        rolled = pltpu.roll(cur, k, 0)
        head = jnp.where(sub < k, pltpu.roll(prev, k, 0), rolled[:SUBLANES])
        shifted = jnp.concatenate([head, rolled[SUBLANES:]], axis=0)
        acc = acc + w_ref[CONV_K - 1 - k:CONV_K - k, :] * shifted
    return acc


def _norm_inproj_kernel(n_a, x_ref, g_ref, wa_ref, wb_ref, wdt_ref, dtb_ref, alog_ref, o_ref, dt_ref, acum_ref, xn_ref):
    j = pl.program_id(1)

    @pl.when(j == 0)
    def _():
        x = x_ref[...]
        y = x * lax.rsqrt(jnp.mean(x * x, axis=-1, keepdims=True) + EPS)
        xn_ref[...] = (y * g_ref[...]).astype(xn_ref.dtype)
        raw = jnp.dot(xn_ref[...], wdt_ref[...], preferred_element_type=jnp.float32)
        dt = _softplus(raw + dtb_ref[...])
        dt_ref[...] = dt
        adt = dt * (-jnp.exp(alog_ref[...]))
        q = SSD_CHUNK
        tri = (lax.broadcasted_iota(jnp.int32, (q, q), 0) >= lax.broadcasted_iota(jnp.int32, (q, q), 1))
        for c in range(x.shape[0] // q):
            acum_ref[c * q:(c + 1) * q, :] = _dot_exact_lhs(tri.astype(jnp.float32), adt[c * q:(c + 1) * q, :])

    @pl.when(j < n_a)
    def _():
        o_ref[...] = jnp.dot(xn_ref[...], wa_ref[...], preferred_element_type=jnp.float32)

    @pl.when(j >= n_a)
    def _():
        o_ref[...] = jnp.dot(xn_ref[...], wb_ref[...], preferred_element_type=jnp.float32)


def _norm_inproj(hp, g, w_a, cols_a, w_b, wdt, dtb, alog):
    tp, d = hp.shape
    n = cols_a + w_b.shape[1]
    tm = _pick(tp, (1024, 512, 256, 128))
    tn = math.gcd(_pick(cols_a, (1024, 512, 256, 128)), _pick(w_b.shape[1], (1024, 512, 256, 128)))
    n_a = cols_a // tn
    vec = pl.BlockSpec((1, LANES), lambda i, j: (0, 0))
    return pl.pallas_call(
        functools.partial(_norm_inproj_kernel, n_a),
        grid=(tp // tm, n // tn),
        in_specs=[
            pl.BlockSpec((tm, d), lambda i, j: (i, 0)),
            pl.BlockSpec((1, d), lambda i, j: (0, 0)),
            pl.BlockSpec((d, tn), lambda i, j: (0, jnp.minimum(j, n_a - 1))),
            pl.BlockSpec((d, tn), lambda i, j: (0, jnp.maximum(j - n_a, 0))),
            pl.BlockSpec((d, LANES), lambda i, j: (0, 0)),
            vec, vec,
        ],
        out_specs=[
            pl.BlockSpec((tm, tn), lambda i, j: (i, j)),
            pl.BlockSpec((tm, LANES), lambda i, j: (i, 0)),
            pl.BlockSpec((tm, LANES), lambda i, j: (i, 0)),
        ],
        out_shape=[
            jax.ShapeDtypeStruct((tp, n), jnp.float32),
            jax.ShapeDtypeStruct((tp, LANES), jnp.float32),
            jax.ShapeDtypeStruct((tp, LANES), jnp.float32),
        ],
        scratch_shapes=[pltpu.VMEM((tm, d), MXU_DTYPE)],
        compiler_params=_params(("parallel", "arbitrary")),
        name="norm_inproj",
    )(hp, g, w_a, w_b, wdt, dtb, alog)


def _ssd_kernel(n_pad, z_ref, xs_ref, b_ref, c_ref, dt_ref, acum_ref,
                cwx_ref, cwb_ref, cwc_ref, cbx_ref, cbb_ref, cbc_ref,
                d_ref, nw_ref, o_ref, state_s):
    g = pl.program_id(1)
    lp = xs_ref.shape[1]
    q = SSD_CHUNK
    gw = xs_ref.shape[2]
    hpg = gw // SSD_HEAD_DIM
    f32 = jnp.float32

    rows = lax.broadcasted_iota(jnp.int32, (LANES, gw), 0)
    cols = lax.broadcasted_iota(jnp.int32, (LANES, gw), 1)
    expand = (rows == g * hpg + lax.shift_right_logical(cols, SSD_HEAD_SHIFT)).astype(f32)

    state_s[...] = jnp.zeros_like(state_s)
    li = lax.broadcasted_iota(jnp.int32, (q, q), 0)
    si = lax.broadcasted_iota(jnp.int32, (q, q), 1)
    causal = li >= si
    lane = lax.broadcasted_iota(jnp.int32, (q, gw), 1)
    row = lax.broadcasted_iota(jnp.int32, (q, gw), 0)

    def chunk(c, carry):
        r0 = c * q
        sl = pl.ds(r0, q)
        xs = _silu(_causal_conv_chunk(xs_ref, cwx_ref, cbx_ref, c))
        bm = _silu(_causal_conv_chunk(b_ref, cwb_ref, cbb_ref, c))
        cm = _silu(_causal_conv_chunk(c_ref, cwc_ref, cbc_ref, c))
        dt_g = _dot_exact_rhs(dt_ref[0, sl, :], expand)
        acum = _dot_exact_rhs(acum_ref[0, sl, :], expand)
        xdt = xs * dt_g
        if r0 < n_pad:
            xdt = jnp.where(row + r0 >= n_pad, xdt, 0.0)
        acum_t = acum.T
        a_last = acum[q - 1:q, :]

        cb = lax.dot_general(cm.astype(MXU_DTYPE), bm.astype(MXU_DTYPE),
                             (((1,), (1,)), ((), ())), preferred_element_type=f32)
        y = d_ref[...] * xs
        for j in range(hpg):
            c0 = j * SSD_HEAD_DIM
            seg = acum[:, c0:c0 + 1] - acum_t[c0:c0 + 1, :]
            decay = jnp.exp(jnp.where(causal, seg, -jnp.inf))
            in_head = (lane >= c0) & (lane < c0 + SSD_HEAD_DIM)
            y = y + _dot(cb * decay, jnp.where(in_head, xdt, 0.0))
        x_end = xdt * jnp.exp(a_last - acum)

        state = state_s[...]
        y = y + _dot(cm, state) * jnp.exp(acum)
        state_s[...] = state * jnp.exp(a_last) + _dot(bm.T, x_end)

        yz = y * _silu(z_ref[0, sl, :])
        yn = yz * lax.rsqrt(jnp.mean(yz * yz, axis=-1, keepdims=True) + EPS)
        o_ref[0, sl, :] = (yn * nw_ref[...]).astype(o_ref.dtype)
        return carry

    for c in range(lp // q):
        chunk(c, 0)


def _ssd(proj3, dt3, acum3, n_pad, cw, cb, d_skip, norm_w, d_inner):
    b, lp, _ = proj3.shape
    gw = d_inner // SSD_GROUPS
    ns = SSD_STATE
    xs_blk0 = d_inner // gw
    b_blk0 = 2 * d_inner // ns
    c_blk0 = (2 * d_inner + SSD_GROUPS * ns) // ns

    cwx = cw[:, :d_inner]
    cwb = cw[:, d_inner:d_inner + SSD_GROUPS * ns]
    cwc = cw[:, d_inner + SSD_GROUPS * ns:]
    cbx = cb[None, :d_inner]
    cbb = cb[None, d_inner:d_inner + SSD_GROUPS * ns]
    cbc = cb[None, d_inner + SSD_GROUPS * ns:]
    d_ch = jnp.repeat(d_skip, SSD_HEAD_DIM)[None, :]
    nw = norm_w[None, :]

    seq = lambda w, off: pl.BlockSpec((1, lp, w), lambda i, g: (i, 0, off + g))
    head = pl.BlockSpec((1, lp, LANES), lambda i, g: (i, 0, 0))
    vec = lambda w: pl.BlockSpec((1, w), lambda i, g: (0, g))
    cwspec = lambda w: pl.BlockSpec((CONV_K, w), lambda i, g: (0, g))
    return pl.pallas_call(
        functools.partial(_ssd_kernel, n_pad),
        grid=(b, SSD_GROUPS),
        in_specs=[
            seq(gw, 0), seq(gw, xs_blk0), seq(ns, b_blk0), seq(ns, c_blk0), head, head,
            cwspec(gw), cwspec(ns), cwspec(ns), vec(gw), vec(ns), vec(ns),
            vec(gw), vec(gw),
        ],
        out_specs=pl.BlockSpec((1, lp, gw), lambda i, g: (i, 0, g)),
        out_shape=jax.ShapeDtypeStruct((b, lp, d_inner), MXU_DTYPE),
        scratch_shapes=[pltpu.VMEM((ns, gw), jnp.float32)],
        compiler_params=_params(("parallel", "parallel")),
        name="ssd",
    )(proj3, proj3, proj3, proj3, dt3, acum3, cwx, cwb, cwc, cbx, cbb, cbc, d_ch, nw)


def _lru_kernel(n_pad, lx_ref, ly_ref, cw_ref, cb_ref, wa_ref, ba_ref, wx_ref, bx_ref, lam_ref, o_ref):
    lp = lx_ref.shape[1]
    w = lx_ref.shape[2]
    q = SSD_CHUNK
    neg_c_softplus = (-LRU_C) * _softplus(-lam_ref[...])
    row = lax.broadcasted_iota(jnp.int32, (q, w), 0)
    sub = lax.broadcasted_iota(jnp.int32, (SUBLANES, w), 0)

    def chunk(c, h_prev):
        r0 = c * q
        sl = pl.ds(r0, q)
        xr = _causal_conv_chunk(lx_ref, cw_ref, cb_ref, c)
        gate_r = _sigmoid(_dot(xr, wa_ref[0]) + ba_ref[...])
        gate_i = _sigmoid(_dot(xr, wx_ref[0]) + bx_ref[...])
        log_a = gate_r * neg_c_softplus
        a = jnp.exp(log_a)
        mult = jnp.sqrt(jnp.tanh(-log_a) * (a * a + 1.0))
        u = mult * gate_i * xr
        if r0 < n_pad:
            u = jnp.where(row + r0 >= n_pad, u, 0.0)
        gate_y = jax.nn.gelu(ly_ref[0, sl, :])

        out = []
        for t in range(q // SUBLANES):
            rows8 = slice(t * SUBLANES, (t + 1) * SUBLANES)
            at, ut = a[rows8], u[rows8]
            for d in (1, 2, 4):
                keep = sub >= d
                ut = jnp.where(keep, at * pltpu.roll(ut, d, 0) + ut, ut)
                at = jnp.where(keep, at * pltpu.roll(at, d, 0), at)
            h = at * h_prev + ut
            h_prev = jnp.broadcast_to(h[SUBLANES - 1:SUBLANES, :], (SUBLANES, w))
            out.append(h * gate_y[rows8])
        o_ref[0, sl, :] = jnp.concatenate(out, axis=0).astype(o_ref.dtype)
        return h_prev

    h_carry = jnp.zeros((SUBLANES, w), jnp.float32)
    for c in range(lp // q):
        h_carry = chunk(c, h_carry)


def _lru(proj3, n_pad, lx_col, ly_col, cw, cb, wa, ba, wx, bx, lam, width):
    b, lp, _ = proj3.shape
    w = width // LRU_HEADS
    seq = lambda off: pl.BlockSpec((1, lp, w), lambda i, h: (i, 0, off + h))
    vec = pl.BlockSpec((1, w), lambda i, h: (0, h))
    mat = pl.BlockSpec((1, w, w), lambda i, h: (h, 0, 0))
    return pl.pallas_call(
        functools.partial(_lru_kernel, n_pad),
        grid=(b, LRU_HEADS),
        in_specs=[seq(lx_col // w), seq(ly_col // w),
                  pl.BlockSpec((CONV_K, w), lambda i, h: (0, h)), vec, mat, vec, mat, vec, vec],
        out_specs=pl.BlockSpec((1, lp, w), lambda i, h: (i, 0, h)),
        out_shape=jax.ShapeDtypeStruct((b, lp, width), MXU_DTYPE),
        compiler_params=_params(("parallel", "parallel")),
        name="rglru",
    )(proj3, proj3, cw, cb[None, :], wa, ba[None, :], wx, bx[None, :], lam[None, :])


def _merge_kernel(ys_ref, yl_ref, ws_ref, wl_ref, g0_ref, g1_ref, gb_ref, o_ref):
    y_ssd = jnp.dot(ys_ref[...], ws_ref[...], preferred_element_type=jnp.float32)
    y_lru = jnp.dot(yl_ref[...], wl_ref[...], preferred_element_type=jnp.float32)
    gate0 = _sigmoid(g0_ref[...] + gb_ref[0:1, :])
    gate1 = _sigmoid(g1_ref[...] + gb_ref[1:2, :])
    o_ref[...] = (gate0 * y_ssd + gate1 * y_lru).astype(o_ref.dtype)


def _merge(ys, yl, ws, wl, proj, gate_col, gate_bias):
    tp, d = ys.shape
    tm = _pick(tp, (1024, 512, 256, 128))
    tn = _pick(d, (512, 256, 128))
    g0 = gate_col // tn
    g1 = (gate_col + d) // tn
    return pl.pallas_call(
        _merge_kernel,
        grid=(tp // tm, d // tn),
        in_specs=[
            pl.BlockSpec((tm, d), lambda i, j: (i, 0)),
            pl.BlockSpec((tm, d), lambda i, j: (i, 0)),
            pl.BlockSpec((d, tn), lambda i, j: (0, j)),
            pl.BlockSpec((d, tn), lambda i, j: (0, j)),
            pl.BlockSpec((tm, tn), lambda i, j: (i, g0 + j)),
            pl.BlockSpec((tm, tn), lambda i, j: (i, g1 + j)),
            pl.BlockSpec((N_BRANCH, tn), lambda i, j: (0, j)),
        ],
        out_specs=pl.BlockSpec((tm, tn), lambda i, j: (i, j)),
        out_shape=jax.ShapeDtypeStruct((tp, d), MXU_DTYPE),
        compiler_params=_params(("parallel", "arbitrary")),
        name="merge",
    )(ys, yl, ws, wl, proj, proj, gate_bias)


def _first_index_of_max(p, valid, lane):
    pm = jnp.where(valid, p, -jnp.inf)
    top = jnp.max(pm, axis=-1, keepdims=True)
    idx = jnp.min(jnp.where(valid & (pm == top), lane, LANES), axis=-1, keepdims=True)
    return top, idx


def _masked_softmax(x, valid):
    m = jnp.max(jnp.where(valid, x, -jnp.inf), axis=-1, keepdims=True)
    e = jnp.where(valid, jnp.exp(x - m), 0.0)
    return e / jnp.sum(e, axis=-1, keepdims=True)


def _outproj_router_kernel(mix_ref, h_ref, wo_ref, g_ref, wr_ref, real_ref,
                           h2_ref, u_ref, gate_ref, info_ref, counts_ref, counts_s):
    @pl.when(pl.program_id(0) == 0)
    def _():
        counts_s[...] = jnp.zeros_like(counts_s)

    h2 = h_ref[...] + jnp.dot(mix_ref[...], wo_ref[...], preferred_element_type=jnp.float32)
    h2_ref[...] = h2
    u = h2 * lax.rsqrt(jnp.mean(h2 * h2, axis=-1, keepdims=True) + EPS) * g_ref[...]
    u_ref[...] = u
    logits = _dot(u, wr_ref[...])
    lane = lax.broadcasted_iota(jnp.int32, logits.shape, 1)

    g_prob = _masked_softmax(logits, lane < MOE_GROUPS)
    g_p, g_idx = _first_index_of_max(g_prob, lane < MOE_GROUPS, lane)

    e_lo = MOE_GROUPS + g_idx * MOE_EXP_PER_GROUP
    in_group = (lane >= e_lo) & (lane < e_lo + MOE_EXP_PER_GROUP)
    e_prob = _masked_softmax(logits, in_group)
    p1, i1 = _first_index_of_max(e_prob, in_group, lane)
    rest = in_group & (lane != i1)
    p2, i2 = _first_index_of_max(e_prob, rest, lane)
    denom = p1 + p2

    col = lax.broadcasted_iota(jnp.int32, gate_ref.shape, 1)
    e1 = i1 - MOE_GROUPS
    e2 = i2 - MOE_GROUPS
    gate_ref[...] = jnp.where(col == 0, g_p * p1 / denom, g_p * p2 / denom)

    m = logits.shape[0]
    hit1 = (lane == e1) & (real_ref[...] > 0.0)
    hit2 = (lane == e2) & (real_ref[...] > 0.0)
    onehot = jnp.where(hit1 | hit2, 1.0, 0.0)
    earlier = (lax.broadcasted_iota(jnp.int32, (m, m), 0) > lax.broadcasted_iota(jnp.int32, (m, m), 1))
    before = counts_s[...] + jnp.dot(earlier.astype(jnp.bfloat16), onehot.astype(jnp.bfloat16),
                                     preferred_element_type=jnp.float32)
    rank1 = jnp.sum(jnp.where(hit1, before, 0.0), axis=-1, keepdims=True)
    rank2 = jnp.sum(jnp.where(hit2, before, 0.0), axis=-1, keepdims=True)
    counts_s[...] += jnp.sum(onehot, axis=0, keepdims=True)
    counts_ref[...] = counts_s[...].astype(jnp.int32)
    f32 = jnp.float32
    info = jnp.where(lane == 0, e1.astype(f32), jnp.where(lane == 1, e2.astype(f32),
                     jnp.where(lane == 2, rank1, jnp.where(lane == 3, rank2, 0.0))))
    info_ref[...] = info.T[:SUBLANES, :]


def _outproj_router(mixed, hp, wo, g_ffn, wr, real_rows):
    tp, d = hp.shape
    tm = _pick(tp, (512, 256, 128))
    row = lambda w: pl.BlockSpec((tm, w), lambda i: (i, 0))
    return pl.pallas_call(
        _outproj_router_kernel,
        grid=(tp // tm,),
        in_specs=[row(d), row(d),
                  pl.BlockSpec((d, d), lambda i: (0, 0)),
                  pl.BlockSpec((1, d), lambda i: (0, 0)),
                  pl.BlockSpec((d, LANES), lambda i: (0, 0)),
                  row(1)],
        out_specs=[row(d), row(d), row(MOE_TOPK),
                   pl.BlockSpec((SUBLANES, tm), lambda i: (0, i)),
                   pl.BlockSpec((1, LANES), lambda i: (0, 0))],
        out_shape=[jax.ShapeDtypeStruct((tp, d), jnp.float32),
                   jax.ShapeDtypeStruct((tp, d), jnp.float32),
                   jax.ShapeDtypeStruct((tp, MOE_TOPK), jnp.float32),
                   jax.ShapeDtypeStruct((SUBLANES, tp), jnp.float32),
                   jax.ShapeDtypeStruct((1, LANES), jnp.int32)],
        scratch_shapes=[pltpu.VMEM((1, LANES), jnp.float32)],
        compiler_params=_params(("arbitrary",)),
        name="outproj_router",
    )(mixed, hp, wo, g_ffn, wr, real_rows)


def _wait_rows(n, make_copy):
    for k in range(EXPERT_ROWS.bit_length() - 1, -1, -1):
        @pl.when((n & (1 << k)) != 0)
        def _():
            make_copy(1 << k).wait()


def _expert_kernel(layout, n_items_ref, n_used_ref, item_e_ref, item_start_ref, item_rows_ref, item_real_ref,
                   dest_ref, u_hbm, w1_ref, w3_ref, w2_ref, y_hbm,
                   xg_s, xb_s, acc_s, tok_ref, sem):
    n_batch, l_real, lp, n_pad = layout
    i = pl.program_id(0)
    f = pl.program_id(1)
    nf = pl.num_programs(1)
    n_items = n_items_ref[0]
    active = i < n_items
    n_rows = item_rows_ref[i]
    d = acc_s.shape[1]
    blk = MOE_BLOCK

    def issue_gather(item):
        start = item_start_ref[item]
        n = item_real_ref[item]
        n_groups = n // GATHER_UNROLL

        def one(r):
            pltpu.make_async_copy(u_hbm.at[pl.ds(tok_ref[start + r], 1)], xg_s.at[pl.ds(r, 1)], sem.at[0]).start()

        def group(g, c):
            for j in range(GATHER_UNROLL):
                one(g * GATHER_UNROLL + j)
            return c
        lax.fori_loop(0, n_groups, group, 0)

        def single(r, c):
            one(r)
            return c
        lax.fori_loop(n_groups * GATHER_UNROLL, n, single, 0)

    def wait_gather(item):
        _wait_rows(item_real_ref[item],
                   lambda m: pltpu.make_async_copy(u_hbm.at[pl.ds(0, m)], xg_s.at[pl.ds(0, m)], sem.at[0]))

    def out_copy(item, j):
        r0 = pl.multiple_of(j * blk, blk)
        dst0 = pl.multiple_of(item_start_ref[item] + r0, blk)
        return pltpu.make_async_copy(acc_s.at[pl.ds(r0, blk)], y_hbm.at[pl.ds(dst0, blk)], sem.at[1])

    def issue_out(item):
        def body(j, c):
            out_copy(item, j).start()
            return c
        lax.fori_loop(0, item_rows_ref[item] // blk, body, 0)

    def wait_out(item):
        def body(j, c):
            out_copy(item, j).wait()
            return c
        lax.fori_loop(0, item_rows_ref[item] // blk, body, 0)

    @pl.when((i == 0) & (f == 0))
    def _():
        xg_s[...] = jnp.zeros_like(xg_s)
        acc_s[pl.ds(0, blk), :] = jnp.zeros((blk, d), acc_s.dtype)
        n_used = n_used_ref[0]
        n_blocks = y_hbm.shape[0] // blk

        def tail_copy(j):
            return pltpu.make_async_copy(acc_s.at[pl.ds(0, blk)], y_hbm.at[pl.ds(pl.multiple_of(j * blk, blk), blk)],
                                         sem.at[1])

        def fill(j, c):
            tail_copy(j).start()
            return c
        lax.fori_loop(n_used, n_blocks, fill, 0)

        def drain(j, c):
            tail_copy(j).wait()
            return c
        lax.fori_loop(n_used, n_blocks, drain, 0)

        for k in range(MOE_TOPK):
            for b in range(n_batch):
                base = (k * n_batch + b) * l_real
                row0 = b * lp + n_pad

                def invert(g, c, base=base, row0=row0):
                    for j in range(GATHER_UNROLL):
                        p = g * GATHER_UNROLL + j
                        tok_ref[dest_ref[base + p]] = row0 + p
                    return c
                lax.fori_loop(0, l_real // GATHER_UNROLL, invert, 0)

        @pl.when(active)
        def _():
            issue_gather(0)

    @pl.when(active & (f == 0))
    def _():
        wait_gather(i)
        for lo, m in zip((0,) + EXPERT_ROW_VARIANTS, EXPERT_ROW_VARIANTS):
            @pl.when((n_rows > lo) & (n_rows <= m))
            def _():
                xb_s[pl.ds(0, m), :] = xg_s[pl.ds(0, m), :].astype(xb_s.dtype)

        @pl.when(i + 1 < n_items)
        def _():
            issue_gather(i + 1)

    @pl.when((f == 0) & (i >= 1) & (i - 1 < n_items) & jnp.logical_not(active))
    def _():
        wait_out(i - 1)

    for lo, m in zip((0,) + EXPERT_ROW_VARIANTS, EXPERT_ROW_VARIANTS):
        @pl.when(active & (n_rows > lo) & (n_rows <= m))
        def _():
            rows = pl.ds(0, m)
            x = xb_s[rows, :]
            a = jnp.dot(x, w1_ref[0].astype(MXU_DTYPE), preferred_element_type=jnp.float32)
            b = jnp.dot(x, w3_ref[0].astype(MXU_DTYPE), preferred_element_type=jnp.float32)
            hdn = (_silu(a) * b).astype(MXU_DTYPE)

            @pl.when((f == 0) & (i >= 1))
            def _():
                wait_out(i - 1)

            for c0 in range(0, d, EXPERT_OUT_TILE):
                cols = pl.ds(c0, EXPERT_OUT_TILE)
                part = jnp.dot(hdn, w2_ref[0, :, cols].astype(MXU_DTYPE), preferred_element_type=jnp.float32)

                @pl.when(f == 0)
                def _():
                    acc_s[rows, cols] = part

                @pl.when(f > 0)
                def _():
                    acc_s[rows, cols] += part

    @pl.when(f == nf - 1)
    def _():
        @pl.when(active)
        def _():
            issue_out(i)

        @pl.when(active & (i == pl.num_programs(0) - 1))
        def _():
            wait_out(i)


def _experts(u, sched, w1, w3, w2, n_items_max, p_len, layout):
    tp, d = u.shape
    ff = w1.shape[2]
    tf = EXPERT_FF_TILE
    nf = ff // tf
    n_items, n_used, item_e, item_start, item_rows, item_real, dest = sched
    assert layout[1] % GATHER_UNROLL == 0

    def w_in_map(i, f, n_items, n_used, item_e, *_):
        return (item_e[i], 0, jnp.where(i < n_items[0], f, nf - 1))

    def w_out_map(i, f, n_items, n_used, item_e, *_):
        return (item_e[i], jnp.where(i < n_items[0], f, nf - 1), 0)

    grid_spec = pltpu.PrefetchScalarGridSpec(
        num_scalar_prefetch=7,
        grid=(n_items_max, nf),
        in_specs=[
            pl.BlockSpec(memory_space=pl.ANY),
            pl.BlockSpec((1, d, tf), w_in_map),
            pl.BlockSpec((1, d, tf), w_in_map),
            pl.BlockSpec((1, tf, d), w_out_map),
        ],
        out_specs=pl.BlockSpec(memory_space=pl.ANY),
        scratch_shapes=[
            pltpu.VMEM((EXPERT_ROWS, d), jnp.float32),
            pltpu.VMEM((EXPERT_ROWS, d), MXU_DTYPE),
            pltpu.VMEM((EXPERT_ROWS, d), jnp.float32),
            pltpu.SMEM((p_len,), jnp.int32),
            pltpu.SemaphoreType.DMA((2,)),
        ],
    )
    return pl.pallas_call(
        functools.partial(_expert_kernel, layout),
        grid_spec=grid_spec,
        out_shape=jax.ShapeDtypeStruct((p_len, d), jnp.float32),
        compiler_params=_params(("arbitrary", "arbitrary")),
        name="experts",
    )(n_items, n_used, item_e, item_start, item_rows, item_real, dest, u, w1, w3, w2)


def _expert_schedule(info, counts, batch, lp, n_pad, n_items_max, p_len):
    i32 = jnp.int32
    eid = info[:MOE_TOPK].astype(i32)
    rank = info[MOE_TOPK:2 * MOE_TOPK].astype(i32)
    padded = (counts + MOE_BLOCK - 1) // MOE_BLOCK * MOE_BLOCK
    pad_end = jnp.cumsum(padded)
    pad_start = pad_end - padded
    experts = jnp.arange(MOE_EXPERTS, dtype=i32)[:, None, None]
    seg_start = jnp.sum(jnp.where(eid[None] == experts, pad_start[:, None, None], 0), axis=0)
    dest = (seg_start + rank).reshape(MOE_TOPK, batch, lp)[:, :, n_pad:]
    n_used = (pad_end[-1:] // MOE_BLOCK).astype(i32)

    chunks = (padded + EXPERT_ROWS - 1) // EXPERT_ROWS
    chunk_end = jnp.cumsum(chunks)
    item = jnp.arange(n_items_max, dtype=i32)
    item_e = jnp.minimum(jnp.sum((chunk_end[None, :] <= item[:, None]).astype(i32), axis=1), MOE_EXPERTS - 1)
    n_items = chunk_end[-1:].astype(i32)
    last_e = item_e[jnp.maximum(n_items[0] - 1, 0)]
    item_e = jnp.where(item < n_items[0], item_e, last_e)
    k = item - (chunk_end - chunks)[item_e]
    item_start = (pad_start[item_e] + k * EXPERT_ROWS).astype(i32)
    item_rows = jnp.clip(padded[item_e] - k * EXPERT_ROWS, 0, EXPERT_ROWS).astype(i32)
    item_real = jnp.clip(counts[item_e] - k * EXPERT_ROWS, 0, EXPERT_ROWS).astype(i32)
    live = item < n_items[0]
    item_start = jnp.where(live, item_start, 0)
    item_rows = jnp.where(live, item_rows, 0)
    item_real = jnp.where(live, item_real, 0)
    return (n_items, n_used, item_e, item_start, item_rows, item_real, dest.reshape(-1)), dest


def _final_kernel(tm, dest_ref, h_ref, gate_ref, g_ref, ys_hbm, o_ref, ybuf, sem):
    t = pl.program_id(0)
    n_tiles = pl.num_programs(0)

    def issue(tile, buf):
        n_x = n_tiles * tm
        for j in range(tm):
            for k in range(MOE_TOPK):
                pos = dest_ref[k * n_x + tile * tm + j]
                pltpu.make_async_copy(ys_hbm.at[pl.ds(pos, 1)], ybuf.at[buf, k, pl.ds(j, 1)],
                                      sem.at[buf]).start(priority=(j + k) % 2)

    @pl.when(t == 0)
    def _():
        issue(0, 0)

    @pl.when(t + 1 < n_tiles)
    def _():
        issue(t + 1, (t + 1) % 2)

    buf = t % 2
    for k in range(MOE_TOPK):
        pltpu.make_async_copy(ys_hbm.at[pl.ds(0, tm)], ybuf.at[buf, k], sem.at[buf]).wait()
    moe = gate_ref[:, 0:1] * ybuf[buf, 0] + gate_ref[:, 1:2] * ybuf[buf, 1]
    h = h_ref[...] + moe
    o_ref[...] = h * lax.rsqrt(jnp.mean(h * h, axis=-1, keepdims=True) + EPS) * g_ref[...]


def _final(h2, ys, dest_x, gates, g_final, batch, lp, seq):
    tp, d = h2.shape
    tm = SSD_CHUNK
    skip = (lp - seq) // tm
    per_b = seq // tm
    src = lambda w: pl.BlockSpec((tm, w), lambda t, dest: ((t // per_b) * (lp // tm) + skip + t % per_b, 0))
    grid_spec = pltpu.PrefetchScalarGridSpec(
        num_scalar_prefetch=1,
        grid=(batch * per_b,),
        in_specs=[src(d), src(MOE_TOPK), pl.BlockSpec((1, d), lambda t, dest: (0, 0)),
                  pl.BlockSpec(memory_space=pl.ANY)],
        out_specs=pl.BlockSpec((tm, d), lambda t, dest: (t, 0)),
        scratch_shapes=[pltpu.VMEM((2, MOE_TOPK, tm, d), jnp.float32), pltpu.SemaphoreType.DMA((2,))],
    )
    return pl.pallas_call(
        functools.partial(_final_kernel, tm),
        grid_spec=grid_spec,
        out_shape=jax.ShapeDtypeStruct((batch * seq, d), jnp.float32),
        compiler_params=_params(("arbitrary",)),
        name="final_norm",
    )(dest_x, h2, gates, g_final, ys)


def kernel(x, meta_tokens, norm_mix, w_in, ssd_conv_w, ssd_conv_b, ssd_dt_bias, ssd_a_log, ssd_d, ssd_norm, w_ssd_out, lru_conv_w, lru_conv_b, lru_wa, lru_ba, lru_wx, lru_bx, lru_lambda, w_lru_out, gate_bias, w_out, norm_ffn, w_router_group, w_router_expert, w_exp_gate, w_exp_up, w_exp_down, norm_final):
    batch, seq, d = x.shape
    depth = norm_mix.shape[0]
    assert depth == 1 and seq % SSD_CHUNK == 0
    l = N_META + seq
    lp = -(-l // SSD_CHUNK) * SSD_CHUNK
    n_pad = lp - l
    d_inner = d
    heads = d_inner // SSD_HEAD_DIM
    conv_dim = d_inner + 2 * SSD_GROUPS * SSD_STATE
    col_xbc = d_inner
    col_dt = col_xbc + conv_dim
    col_lx = col_dt + heads
    f32 = jnp.float32
    lyr = 0

    meta = jnp.broadcast_to(meta_tokens.astype(f32)[None], (batch, N_META, d))
    hp = jnp.concatenate([jnp.zeros((batch, n_pad, d), f32), meta, x], axis=1).reshape(batch * lp, d)

    w = w_in[lyr]
    w_all = w.astype(MXU_DTYPE)
    w_b = w_all[:, col_lx:]
    w_dt = jnp.zeros((d, LANES), MXU_DTYPE).at[:, :heads].set(w_all[:, col_dt:col_lx])
    dtb = jnp.zeros((1, LANES), f32).at[0, :heads].set(ssd_dt_bias[lyr])
    alog = jnp.zeros((1, LANES), f32).at[0, :heads].set(ssd_a_log[lyr])
    proj, dt, acum = _norm_inproj(hp, norm_mix[lyr][None, :], w_all, col_dt, w_b, w_dt, dtb, alog)
    n_main = col_dt + w_b.shape[1]
    proj3 = proj.reshape(batch, lp, n_main)
    lx_col = col_dt
    ly_col = lx_col + d
    gl_col = ly_col + d

    ys = _ssd(proj3, dt.reshape(batch, lp, LANES), acum.reshape(batch, lp, LANES), n_pad, ssd_conv_w[lyr],
              ssd_conv_b[lyr], ssd_d[lyr], ssd_norm[lyr], d_inner)
    yl = _lru(proj3, n_pad, lx_col, ly_col, lru_conv_w[lyr], lru_conv_b[lyr], lru_wa[lyr].astype(MXU_DTYPE),
              lru_ba[lyr], lru_wx[lyr].astype(MXU_DTYPE), lru_bx[lyr], lru_lambda[lyr], d)

    mixed = _merge(ys.reshape(batch * lp, d), yl.reshape(batch * lp, d), w_ssd_out[lyr].astype(MXU_DTYPE),
                   w_lru_out[lyr].astype(MXU_DTYPE), proj, gl_col, gate_bias[lyr])

    w_router = jnp.zeros((d, LANES), f32)
    w_router = w_router.at[:, :MOE_GROUPS].set(w_router_group[lyr])
    w_router = w_router.at[:, MOE_GROUPS:MOE_GROUPS + MOE_EXPERTS].set(w_router_expert[lyr])
    real_rows = jnp.asarray(((np.arange(batch * lp) % lp) >= n_pad).astype(np.float32)[:, None])
    h2, u2, gates, info, counts = _outproj_router(mixed, hp, w_out[lyr].astype(MXU_DTYPE), norm_ffn[lyr][None, :],
                                                  w_router.astype(MXU_DTYPE), real_rows)

    n_asg = batch * l * MOE_TOPK
    p_max = n_asg + MOE_EXPERTS * (MOE_BLOCK - 1)
    n_items_max = MOE_EXPERTS + p_max // EXPERT_ROWS
    p_len = -(-p_max // MOE_BLOCK) * MOE_BLOCK
    sched, dest = _expert_schedule(info, counts[0, :MOE_EXPERTS], batch, lp, n_pad, n_items_max, p_len)
    ff = w_exp_gate.shape[-1]
    ys_sorted = _experts(u2, sched, w_exp_gate.reshape(MOE_EXPERTS, d, ff), w_exp_up.reshape(MOE_EXPERTS, d, ff),
                         w_exp_down.reshape(MOE_EXPERTS, ff, d), n_items_max, p_len, (batch, l, lp, n_pad))

    dest_x = dest[:, :, N_META:].reshape(-1)
    out = _final(h2, ys_sorted, dest_x, gates, norm_final[None, :], batch, lp, seq)
    return out.reshape(batch, seq, d)
```
